```python
import jax
import jax.numpy as jnp
from jax import lax
import numpy as np

D_MODEL = 1024
BATCH = 2
SEQ = 8192
DEPTH = 4

GRID_W = 64
CTX_LEN = 256
N_MIXERS = 4
N_MLA_LAYERS = (DEPTH + 3) // 4
N_GQA_LAYERS = (DEPTH + 2) // 4
N_NA_LAYERS = (DEPTH + 1) // 4
N_RET_LAYERS = DEPTH // 4
Q_BLOCK = 128
ROPE_THETA = 10000.0
EPS = 1e-6
MLA_HEADS = 8
MLA_Q_RANK = 384
MLA_KV_RANK = 256
MLA_NOPE = 128
MLA_ROPE = 64
MLA_QK = MLA_NOPE + MLA_ROPE
MLA_V = 128
GQA_HEADS = 8
GQA_KV_HEADS = 2
GQA_HEAD_DIM = 128
NA_HEADS = 16
NA_HEAD_DIM = 64
NA_WIN_ROWS = 8
NA_WIN_COLS = 16
RET_HEADS = 4
RET_QK_DIM = 256
RET_V_DIM = 512
RET_CHUNK = 128
FFN_HIDDEN = ((8 * D_MODEL + 3 * 256 - 1) // (3 * 256)) * 256
F32 = jnp.float32

kernel_name = 'hybrid_diffusion_interleaved_mixers'


def rms_norm(x, gain):
    xf = x.astype(F32)
    y = xf * lax.rsqrt(jnp.mean(xf * xf, axis=-1, keepdims=True) + EPS)
    return (y * gain.astype(F32)).astype(x.dtype)


def rope_tables(length, dim):
    t = jnp.arange(length)
    row = (t // GRID_W).astype(F32)
    col = (t % GRID_W).astype(F32)
    quarter = dim // 4
    inv_freq = ROPE_THETA ** (-jnp.arange(quarter, dtype=F32) / quarter)
    ang = jnp.concatenate([row[:, None] * inv_freq, col[:, None] * inv_freq], axis=-1)
    return jnp.cos(ang), jnp.sin(ang)


def apply_rope(x, cos, sin):
    half = x.shape[-1] // 2
    xf = x.astype(F32)
    x1, x2 = xf[..., :half], xf[..., half:]
    c, s = cos[None, :, None, :], sin[None, :, None, :]
    return jnp.concatenate([x1 * c - x2 * s, x1 * s + x2 * c], axis=-1).astype(x.dtype)


def dense_attention(q, k, v, scale):
    b, s, h, dq = q.shape
    kvh, dv = k.shape[2], v.shape[-1]
    g = h // kvh
    qb = q.reshape(b, s // Q_BLOCK, Q_BLOCK, kvh, g, dq).transpose(1, 0, 2, 3, 4, 5)

    def block(q_blk):
        sc = jnp.einsum('bqkgd,btkd->bkgqt', q_blk, k).astype(F32) * scale
        p = jax.nn.softmax(sc, axis=-1).astype(v.dtype)
        return jnp.einsum('bkgqt,btkd->bqkgd', p, v)

    o = lax.map(block, qb)
    return o.transpose(1, 0, 2, 3, 4, 5).reshape(b, s, h, dv)


def joint_attention(ql, kl, vl, qc, kc, vc, scale):
    ol = dense_attention(ql, jnp.concatenate([kc, kl], axis=1), jnp.concatenate([vc, vl], axis=1), scale)
    oc = dense_attention(qc, kc, vc, scale) if qc is not None else None
    return ol, oc


def merge_heads(o, w_o):
    return o.reshape(o.shape[0], o.shape[1], -1) @ w_o


def mla_mixer(xl, xc, w_down, q_lora_norm, kv_lora_norm, w_uq, w_ukv, q_norm, k_norm, w_o, need_ctx):
    b, s, _ = xl.shape
    cos, sin = rope_tables(s, MLA_ROPE)

    def queries(c_q):
        q = (rms_norm(c_q, q_lora_norm) @ w_uq).reshape(b, c_q.shape[1], MLA_HEADS, MLA_QK)
        return rms_norm(q, q_norm)

    def keys_values(c_kv, k_rope):
        l = c_kv.shape[1]
        kv = (rms_norm(c_kv, kv_lora_norm) @ w_ukv).reshape(b, l, MLA_HEADS, MLA_NOPE + MLA_V)
        k = jnp.concatenate([kv[..., :MLA_NOPE],
                             jnp.broadcast_to(k_rope[:, :, None, :], (b, l, MLA_HEADS, MLA_ROPE))], axis=-1)
        return rms_norm(k, k_norm), kv[..., MLA_NOPE:]

    def rotate(t):
        return jnp.concatenate([t[..., :MLA_NOPE], apply_rope(t[..., MLA_NOPE:], cos, sin)], axis=-1)

    dl = xl @ w_down
    ql = rotate(queries(dl[..., :MLA_Q_RANK]))
    kl, vl = keys_values(dl[..., MLA_Q_RANK:MLA_Q_RANK + MLA_KV_RANK], dl[..., MLA_Q_RANK + MLA_KV_RANK:])
    kl = rotate(kl)
    if need_ctx:
        dc = xc @ w_down
        qc = queries(dc[..., :MLA_Q_RANK])
        dkv = dc[..., MLA_Q_RANK:]
    else:
        qc = None
        dkv = xc @ w_down[:, MLA_Q_RANK:]
    kc, vc = keys_values(dkv[..., :MLA_KV_RANK], dkv[..., MLA_KV_RANK:])
    ol, oc = joint_attention(ql, kl, vl, qc, kc, vc, MLA_QK ** -0.5)
    return merge_heads(ol, w_o), (merge_heads(oc, w_o) if need_ctx else None)


def gqa_project(x, w_qkv, q_norm, k_norm, with_q):
    b, l, _ = x.shape
    nq = GQA_HEADS * GQA_HEAD_DIM
    if with_q:
        p = x @ w_qkv
        q = rms_norm(p[..., :nq].reshape(b, l, GQA_HEADS, GQA_HEAD_DIM), q_norm)
        p = p[..., nq:]
    else:
        q = None
        p = x @ w_qkv[:, nq:]
    kv = p.reshape(b, l, 2, GQA_KV_HEADS, GQA_HEAD_DIM)
    return q, rms_norm(kv[:, :, 0], k_norm), kv[:, :, 1]


def gqa_mixer(xl, xc, w_qkv, q_norm, k_norm, w_o, need_ctx):
    cos, sin = rope_tables(xl.shape[1], GQA_HEAD_DIM)
    ql, kl, vl = gqa_project(xl, w_qkv, q_norm, k_norm, True)
    ql, kl = apply_rope(ql, cos, sin), apply_rope(kl, cos, sin)
    qc, kc, vc = gqa_project(xc, w_qkv, q_norm, k_norm, need_ctx)
    ol, oc = joint_attention(ql, kl, vl, qc, kc, vc, GQA_HEAD_DIM ** -0.5)
    return merge_heads(ol, w_o), (merge_heads(oc, w_o) if need_ctx else None)


def na_project(x, w_qkv, q_norm, k_norm, with_q):
    b, l, _ = x.shape
    nq = NA_HEADS * NA_HEAD_DIM
    if with_q:
        p = x @ w_qkv
        q = rms_norm(p[..., :nq].reshape(b, l, NA_HEADS, NA_HEAD_DIM), q_norm)
        p = p[..., nq:]
    else:
        q = None
        p = x @ w_qkv[:, nq:]
    kv = p.reshape(b, l, 2, NA_HEADS, NA_HEAD_DIM)
    return q, rms_norm(kv[:, :, 0], k_norm), kv[:, :, 1]


def na_mixer(xl, xc, w_qkv, q_norm, k_norm, rpb, w_o, need_ctx):
    b, s, _ = xl.shape
    rows = s // GRID_W
    wr = min(NA_WIN_ROWS, rows)
    wc = NA_WIN_COLS
    scale = NA_HEAD_DIM ** -0.5
    ql, kl, vl = na_project(xl, w_qkv, q_norm, k_norm, True)
    qc, kc, vc = na_project(xc, w_qkv, q_norm, k_norm, need_ctx)
    qg = ql.reshape(b, rows, GRID_W, NA_HEADS, NA_HEAD_DIM).transpose(1, 0, 2, 3, 4)
    kg = kl.reshape(b, rows, GRID_W, NA_HEADS, NA_HEAD_DIM)
    vg = vl.reshape(b, rows, GRID_W, NA_HEADS, NA_HEAD_DIM)
    col = jnp.arange(GRID_W)
    col_start = jnp.clip(col - wc // 2, 0, GRID_W - wc)
    col_idx = col_start[:, None] + jnp.arange(wc)[None, :]
    col_rel = col_idx - col[:, None] + (NA_WIN_COLS - 1)

    def one_row(args):
        r, q_r = args
        r0 = jnp.clip(r - wr // 2, 0, rows - wr)
        k_win = lax.dynamic_slice_in_dim(kg, r0, wr, axis=1)[:, :, col_idx]
        v_win = lax.dynamic_slice_in_dim(vg, r0, wr, axis=1)[:, :, col_idx]
        row_rel = r0 + jnp.arange(wr) - r + (NA_WIN_ROWS - 1)
        bias = rpb[:, row_rel[:, None, None], col_rel[None, :, :]]
        s_loc = jnp.einsum('bqhd,bwqjhd->bhqwj', q_r, k_win).astype(F32) * scale
        s_loc = s_loc + bias.transpose(0, 2, 1, 3)[None].astype(F32)
        s_ctx = jnp.einsum('bqhd,bthd->bhqt', q_r, kc).astype(F32) * scale
        sc = jnp.concatenate([s_loc.reshape(b, NA_HEADS, GRID_W, wr * wc), s_ctx], axis=-1)
        p = jax.nn.softmax(sc, axis=-1).astype(vl.dtype)
        p_loc = p[..., :wr * wc].reshape(b, NA_HEADS, GRID_W, wr, wc)
        p_ctx = p[..., wr * wc:]
        return (jnp.einsum('bhqwj,bwqjhd->bqhd', p_loc, v_win)
                + jnp.einsum('bhqt,bthd->bqhd', p_ctx, vc))

    o = lax.map(one_row, (jnp.arange(rows), qg))
    ol = o.transpose(1, 0, 2, 3, 4).reshape(b, s, NA_HEADS * NA_HEAD_DIM) @ w_o
    oc = merge_heads(dense_attention(qc, kc, vc, scale), w_o) if need_ctx else None
    return ol, oc


def ret_project(x, w_qkvg, with_qg):
    b, l, _ = x.shape
    nk, nv = RET_HEADS * RET_QK_DIM, RET_HEADS * RET_V_DIM
    if with_qg:
        p = x @ w_qkvg
        q = p[..., :nk].reshape(b, l, RET_HEADS, RET_QK_DIM)
        g = p[..., 2 * nk + nv:]
        p = p[..., nk:2 * nk + nv]
    else:
        q = g = None
        p = x @ w_qkvg[:, nk:2 * nk + nv]
    k = p[..., :nk].reshape(b, l, RET_HEADS, RET_QK_DIM) * (RET_QK_DIM ** -0.5)
    v = p[..., nk:].reshape(b, l, RET_HEADS, RET_V_DIM)
    return q, k, v, g


def ret_context_state(k, v, log_decay, reverse):
    lc = k.shape[1]
    pos = jnp.arange(lc, dtype=F32)
    age = pos if reverse else (lc - 1 - pos)
    w = jnp.exp(age[:, None] * log_decay.astype(F32)[None, :])
    return jnp.einsum('bthk,bthv->bhkv', k.astype(F32) * w[None, :, :, None], v.astype(F32))


def chunk_retention(q, k, v, log_decay, init_state):
    b, l, h, dk = q.shape
    dv = v.shape[-1]
    n = l // RET_CHUNK
    ld = log_decay.astype(F32)
    i = jnp.arange(RET_CHUNK, dtype=F32)
    dist = i[:, None] - i[None, :]
    intra = jnp.where(dist >= 0, jnp.exp(jnp.maximum(dist, 0.0)[None] * ld[:, None, None]), 0.0)
    q_decay = jnp.exp((i + 1.0)[:, None] * ld[None, :])
    k_decay = jnp.exp((RET_CHUNK - 1.0 - i)[:, None] * ld[None, :])
    chunk_decay = jnp.exp(RET_CHUNK * ld)

    def split_chunks(t):
        return t.reshape(b, n, RET_CHUNK, h, t.shape[-1]).transpose(1, 0, 2, 3, 4)

    def step(state, blk):
        qc, kc, vc = blk
        qf, kf, vf = qc.astype(F32), kc.astype(F32), vc.astype(F32)
        sc = jnp.einsum('bqhk,bthk->bhqt', qf, kf) * intra[None]
        inner = jnp.einsum('bhqt,bthv->bqhv', sc, vf)
        cross = jnp.einsum('bqhk,bhkv->bqhv', qf, state) * q_decay[None, :, :, None]
        new_state = (state * chunk_decay[None, :, None, None]
                     + jnp.einsum('bthk,bthv->bhkv', kf * k_decay[None, :, :, None], vf))
        return new_state, (inner + cross).astype(q.dtype)

    _, out = lax.scan(step, init_state, (split_chunks(q), split_chunks(k), split_chunks(v)))
    return out.transpose(1, 0, 2, 3, 4).reshape(b, l, h, dv)


def ret_output(y, g, out_norm, w_o, dtype):
    b, l, h, dv = y.shape
    mu = jnp.mean(y, axis=-1, keepdims=True)
    var = jnp.mean(jnp.square(y - mu), axis=-1, keepdims=True)
    yn = ((y - mu) * lax.rsqrt(var + EPS)).reshape(b, l, h * dv) * out_norm.astype(F32)
    return (yn * jax.nn.silu(g.astype(F32))).astype(dtype) @ w_o


def ret_mixer(xl, xc, w_qkvg, log_decay_fwd, log_decay_bwd, out_norm, w_o, need_ctx):
    cos, sin = rope_tables(xl.shape[1], RET_QK_DIM)
    ql, kl, vl, gl = ret_project(xl, w_qkvg, True)
    ql, kl = apply_rope(ql, cos, sin), apply_rope(kl, cos, sin)
    qc, kc, vc, gc = ret_project(xc, w_qkvg, need_ctx)
    state_f = ret_context_state(kc, vc, log_decay_fwd, False)
    state_b = ret_context_state(kc, vc, log_decay_bwd, True)
    yf = chunk_retention(ql, kl, vl, log_decay_fwd, state_f)
    yb = jnp.flip(chunk_retention(jnp.flip(ql, 1), jnp.flip(kl, 1), jnp.flip(vl, 1),
                                  log_decay_bwd, state_b), 1)
    ol = ret_output(yf.astype(F32) + yb.astype(F32), gl, out_norm, w_o, xl.dtype)
    oc = None
    if need_ctx:
        lc = xc.shape[1]
        pos = jnp.arange(lc, dtype=F32)
        dist = pos[:, None] - pos[None, :]
        lf = log_decay_fwd.astype(F32)[:, None, None]
        lb = log_decay_bwd.astype(F32)[:, None, None]
        dec = (jnp.where(dist >= 0, jnp.exp(jnp.maximum(dist, 0.0)[None] * lf), 0.0)
               + jnp.where(dist <= 0, jnp.exp(jnp.maximum(-dist, 0.0)[None] * lb), 0.0))
        sc = jnp.einsum('bqhk,bthk->bhqt', qc, kc).astype(F32) * dec[None]
        yc = jnp.einsum('bhqt,bthv->bqhv', sc, vc.astype(F32))
        oc = ret_output(yc, gc, out_norm, w_o, xc.dtype)
    return ol, oc


def swiglu(x, w13, w2):
    a = x @ w13
    return (jax.nn.silu(a[..., :FFN_HIDDEN]) * a[..., FFN_HIDDEN:]) @ w2


def setup_inputs(seed: int = 0) -> dict:
    key = jax.random.key(seed)
    keys = iter(jax.random.split(key, 40))
    D = D_MODEL

    def nrm(shape, scale):
        return scale * jax.random.normal(next(keys), shape, F32)

    def gain(shape):
        return 1.0 + nrm(shape, 0.02)

    base_decay = jnp.log(1.0 - 2.0 ** (-5.0 - jnp.arange(RET_HEADS, dtype=F32)))
    nk, nv = RET_HEADS * RET_QK_DIM, RET_HEADS * RET_V_DIM
    return {
        'x': nrm((BATCH, SEQ, D), 1.0),
        'c': nrm((BATCH, D), 1.0),
        'ctx': nrm((BATCH, CTX_LEN, D), 1.0),
        'c_ctx': nrm((D,), 1.0),
        'mod_w': nrm((DEPTH, D, 6 * D), 0.5 * D ** -0.5),
        'mod_b': nrm((DEPTH, 6 * D), 0.01),
        'norm1': gain((DEPTH, D)),
        'norm2': gain((DEPTH, D)),
        'ffn_w13': nrm((DEPTH, D, 2 * FFN_HIDDEN), D ** -0.5),
        'ffn_w2': nrm((DEPTH, FFN_HIDDEN, D), FFN_HIDDEN ** -0.5),
        'mla_w_down': nrm((N_MLA_LAYERS, D, MLA_Q_RANK + MLA_KV_RANK + MLA_ROPE), D ** -0.5),
        'mla_q_lora_norm': gain((N_MLA_LAYERS, MLA_Q_RANK)),
        'mla_kv_lora_norm': gain((N_MLA_LAYERS, MLA_KV_RANK)),
        'mla_w_uq': nrm((N_MLA_LAYERS, MLA_Q_RANK, MLA_HEADS * MLA_QK), MLA_Q_RANK ** -0.5),
        'mla_w_ukv': nrm((N_MLA_LAYERS, MLA_KV_RANK, MLA_HEADS * (MLA_NOPE + MLA_V)), MLA_KV_RANK ** -0.5),
        'mla_q_norm': gain((N_MLA_LAYERS, MLA_QK)),
        'mla_k_norm': gain((N_MLA_LAYERS, MLA_QK)),
        'mla_w_o': nrm((N_MLA_LAYERS, MLA_HEADS * MLA_V, D), (MLA_HEADS * MLA_V) ** -0.5),
        'gqa_w_qkv': nrm((N_GQA_LAYERS, D, (GQA_HEADS + 2 * GQA_KV_HEADS) * GQA_HEAD_DIM), D ** -0.5),
        'gqa_q_norm': gain((N_GQA_LAYERS, GQA_HEAD_DIM)),
        'gqa_k_norm': gain((N_GQA_LAYERS, GQA_HEAD_DIM)),
        'gqa_w_o': nrm((N_GQA_LAYERS, GQA_HEADS * GQA_HEAD_DIM, D), (GQA_HEADS * GQA_HEAD_DIM) ** -0.5),
        'na_w_qkv': nrm((N_NA_LAYERS, D, 3 * NA_HEADS * NA_HEAD_DIM), D ** -0.5),
        'na_q_norm': gain((N_NA_LAYERS, NA_HEAD_DIM)),
        'na_k_norm': gain((N_NA_LAYERS, NA_HEAD_DIM)),
        'na_rpb': nrm((N_NA_LAYERS, NA_HEADS, 2 * NA_WIN_ROWS - 1, 2 * NA_WIN_COLS - 1), 0.1),
        'na_w_o': nrm((N_NA_LAYERS, NA_HEADS * NA_HEAD_DIM, D), (NA_HEADS * NA_HEAD_DIM) ** -0.5),
        'ret_w_qkvg': nrm((N_RET_LAYERS, D, 2 * nk + 2 * nv), D ** -0.5),
        'ret_log_decay_fwd': base_decay * (1.0 + nrm((N_RET_LAYERS, RET_HEADS), 0.05)),
        'ret_log_decay_bwd': base_decay * (1.0 + nrm((N_RET_LAYERS, RET_HEADS), 0.05)),
        'ret_out_norm': gain((N_RET_LAYERS, nv)),
        'ret_w_o': nrm((N_RET_LAYERS, nv, D), nv ** -0.5),
    }


def reference(x, c, ctx, c_ctx, mod_w, mod_b, norm1, norm2, ffn_w13, ffn_w2,
              mla_w_down, mla_q_lora_norm, mla_kv_lora_norm, mla_w_uq, mla_w_ukv, mla_q_norm, mla_k_norm,
              mla_w_o, gqa_w_qkv, gqa_q_norm, gqa_k_norm, gqa_w_o,
              na_w_qkv, na_q_norm, na_k_norm, na_rpb, na_w_o,
              ret_w_qkvg, ret_log_decay_fwd, ret_log_decay_bwd, ret_out_norm, ret_w_o):
    h, hc = x, ctx
    silu_c = jax.nn.silu(c)
    silu_cc = jax.nn.silu(c_ctx)
    for i in range(DEPTH):
        need_ctx = i < DEPTH - 1
        ml = (silu_c @ mod_w[i] + mod_b[i])[:, None, :]
        mc = silu_cc @ mod_w[i] + mod_b[i]
        sh1, sc1, g1, sh2, sc2, g2 = jnp.split(ml, 6, axis=-1)
        csh1, csc1, cg1, csh2, csc2, cg2 = jnp.split(mc, 6)
        xl = rms_norm(h, norm1[i]) * (1.0 + sc1) + sh1
        xc = rms_norm(hc, norm1[i]) * (1.0 + csc1) + csh1
        kind, j = i % N_MIXERS, i // N_MIXERS
        if kind == 0:
            ol, oc = mla_mixer(xl, xc, mla_w_down[j], mla_q_lora_norm[j], mla_kv_lora_norm[j], mla_w_uq[j],
                               mla_w_ukv[j], mla_q_norm[j], mla_k_norm[j], mla_w_o[j], need_ctx)
        elif kind == 1:
            ol, oc = gqa_mixer(xl, xc, gqa_w_qkv[j], gqa_q_norm[j], gqa_k_norm[j], gqa_w_o[j], need_ctx)
        elif kind == 2:
            ol, oc = na_mixer(xl, xc, na_w_qkv[j], na_q_norm[j], na_k_norm[j], na_rpb[j], na_w_o[j], need_ctx)
        else:
            ol, oc = ret_mixer(xl, xc, ret_w_qkvg[j], ret_log_decay_fwd[j], ret_log_decay_bwd[j],
                               ret_out_norm[j], ret_w_o[j], need_ctx)
        h = h + g1 * ol
        h = h + g2 * swiglu(rms_norm(h, norm2[i]) * (1.0 + sc2) + sh2, ffn_w13[i], ffn_w2[i])
        if need_ctx:
            hc = hc + cg1 * oc
            hc = hc + cg2 * swiglu(rms_norm(hc, norm2[i]) * (1.0 + csc2) + csh2, ffn_w13[i], ffn_w2[i])
    return h
```

```python
import functools

import numpy as np
import jax
import jax.numpy as jnp
from jax import lax
from jax.experimental import pallas as pl
from jax.experimental.pallas import tpu as pltpu

F32 = jnp.float32
BF16 = jnp.bfloat16

GRID_W = 64
ROPE_THETA = 10000.0
EPS = 1e-6
MLA_HEADS, MLA_Q_RANK, MLA_KV_RANK, MLA_NOPE, MLA_ROPE, MLA_V = 8, 384, 256, 128, 64, 128
MLA_QK = MLA_NOPE + MLA_ROPE
MLA_QK_PAD = 256
GQA_HEADS, GQA_KV_HEADS, GQA_HEAD_DIM = 8, 2, 128
NA_HEADS, NA_HEAD_DIM, NA_WIN_ROWS, NA_WIN_COLS = 16, 64, 8, 16
RET_HEADS, RET_QK_DIM, RET_V_DIM = 4, 256, 512

LANES = 128
V7X_VMEM_LIMIT_BYTES = 56 * 1024 * 1024
NEG_BIG = -1e30

NA_QROWS = 4
NA_KROWS = NA_QROWS + NA_WIN_ROWS


def _dot(a, b):
    return jnp.dot(a, b, preferred_element_type=F32)


def _dot_nt(a, b):
    return lax.dot_general(a, b, (((1,), (1,)), ((), ())), preferred_element_type=F32)


def _dot_tn(a, b):
    return lax.dot_general(a, b, (((0,), (0,)), ((), ())), preferred_element_type=F32)


def _silu(x):
    return x / (1.0 + jnp.exp(-x))


def _norm_mod(h, gain, scale, shift):
    y = h * lax.rsqrt(jnp.mean(h * h, axis=-1, keepdims=True) + EPS) * gain
    return y * (1.0 + scale) + shift


def _rms(x, denom):
    return x * lax.rsqrt(jnp.sum(x * x, axis=-1, keepdims=True) / denom + EPS)


def _compiler_params(semantics):
    return pltpu.CompilerParams(dimension_semantics=semantics, vmem_limit_bytes=V7X_VMEM_LIMIT_BYTES)


def _row_call(body, *, name, batch, length, tm, rows=(), pos=(), vecs=(), consts=(), outs=()):
    in_specs = []
    for a in rows:
        in_specs.append(pl.BlockSpec((None, tm, a.shape[-1]), lambda b, i: (b, i, 0)))
    for a in pos:
        in_specs.append(pl.BlockSpec((tm, a.shape[-1]), lambda b, i: (i, 0)))
    for a in vecs:
        if a.shape[0] == 1:
            in_specs.append(pl.BlockSpec((None, 1, a.shape[-1]), lambda b, i: (0, 0, 0)))
        else:
            in_specs.append(pl.BlockSpec((None, 1, a.shape[-1]), lambda b, i: (b, 0, 0)))
    for a in consts:
        nd = a.ndim
        in_specs.append(pl.BlockSpec(a.shape, lambda b, i, nd=nd: (0,) * nd, pipeline_mode=pl.Buffered(1)))
    out_shape = [jax.ShapeDtypeStruct((batch, length, f), dt) for f, dt in outs]
    out_specs = [pl.BlockSpec((None, tm, f), lambda b, i: (b, i, 0)) for f, _ in outs]
    res = pl.pallas_call(
        body,
        grid=(batch, length // tm),
        in_specs=in_specs,
        out_specs=out_specs,
        out_shape=out_shape,
        compiler_params=_compiler_params(("parallel", "parallel")),
        name=name,
    )(*rows, *pos, *vecs, *consts)
    return res


def _tile(length, pref):
    return pref if length % pref == 0 else length


def _mod_body(c_ref, w_ref, b_ref, o_ref):
    s = _silu(c_ref[...]).astype(BF16)
    o_ref[...] = _dot(s, w_ref[...].astype(BF16)) + b_ref[...]


def _modulation(c_rows, mod_w, mod_b):
    depth, d, n = mod_w.shape
    tn = 1536 if n % 1536 == 0 else n
    return pl.pallas_call(
        _mod_body,
        grid=(depth, n // tn),
        in_specs=[pl.BlockSpec(c_rows.shape, lambda l, j: (0, 0)),
                  pl.BlockSpec((None, d, tn), lambda l, j: (l, 0, j)),
                  pl.BlockSpec((None, 1, tn), lambda l, j: (l, 0, j))],
        out_specs=pl.BlockSpec((None, c_rows.shape[0], tn), lambda l, j: (l, 0, j)),
        out_shape=jax.ShapeDtypeStruct((depth, c_rows.shape[0], n), F32),
        compiler_params=_compiler_params(("parallel", "parallel")),
        name="modulation",
    )(c_rows, mod_w, mod_b.reshape(depth, 1, n))


def _rope_cos_sin(length, dim):
    t = jnp.arange(length)
    row = (t // GRID_W).astype(F32)
    col = (t % GRID_W).astype(F32)
    quarter = dim // 4
    inv_freq = ROPE_THETA ** (-jnp.arange(quarter, dtype=F32) / quarter)
    ang = jnp.concatenate([row[:, None] * inv_freq, col[:, None] * inv_freq], axis=-1)
    return jnp.cos(ang), jnp.sin(ang)


def _rope_tables_rolled(length, dim, lane_half):
    cos, sin = _rope_cos_sin(length, dim)
    pad = jnp.zeros((length, lane_half - dim // 2), F32)
    cos_t = jnp.concatenate([cos, pad, cos, pad], axis=-1)
    sin_t = jnp.concatenate([-sin, pad, sin, pad], axis=-1)
    return cos_t, sin_t


def _rope_rolled(x, cos_t, sin_t):
    return x * cos_t + pltpu.roll(x, x.shape[-1] // 2, 1) * sin_t


def _mla_proj_body(*refs, rope, with_q, d):
    it = iter(refs)
    h_ref = next(it)
    cos_ref = sin_ref = None
    if rope:
        cos_ref, sin_ref = next(it), next(it)
    mod_ref, n1_ref, wd_ref, gql_ref, gkl_ref, wuq_ref, wuk_ref, wuv_ref, gq_ref, gk_ref = (next(it) for _ in range(10))
    if with_q:
        q_ref = next(it)
    k_ref, v_ref = next(it), next(it)

    mod = mod_ref[...]
    x = _norm_mod(h_ref[...], n1_ref[...], mod[:, d:2 * d], mod[:, 0:d]).astype(BF16)
    dl = _dot(x, wd_ref[...])
    ckv = (_rms(dl[:, MLA_Q_RANK:MLA_Q_RANK + MLA_KV_RANK], MLA_KV_RANK) * gkl_ref[...]).astype(BF16)
    kr = dl[:, MLA_Q_RANK + MLA_KV_RANK:]
    kn = _dot(ckv, wuk_ref[...])
    v_ref[...] = _dot(ckv, wuv_ref[...]).astype(BF16)
    if rope:
        cos_t, sin_t = cos_ref[...], sin_ref[...]
    gk = gk_ref[...]
    kr_ss = jnp.sum(kr * kr, axis=-1, keepdims=True)
    for hh in range(MLA_HEADS):
        kh = kn[:, hh * MLA_NOPE:(hh + 1) * MLA_NOPE]
        r = lax.rsqrt((jnp.sum(kh * kh, axis=-1, keepdims=True) + kr_ss) / MLA_QK + EPS)
        k_ref[:, hh * MLA_QK_PAD:hh * MLA_QK_PAD + MLA_NOPE] = (kh * r * gk[:, :MLA_NOPE]).astype(BF16)
        krh = kr * r * gk[:, MLA_NOPE:]
        if rope:
            krh = _rope_rolled(krh, cos_t, sin_t)
        k_ref[:, hh * MLA_QK_PAD + MLA_NOPE:(hh + 1) * MLA_QK_PAD] = krh.astype(BF16)
    if with_q:
        cq = (_rms(dl[:, :MLA_Q_RANK], MLA_Q_RANK) * gql_ref[...]).astype(BF16)
        q = _dot(cq, wuq_ref[...])
        gq = gq_ref[...]
        scale = MLA_QK ** -0.5
        for hh in range(MLA_HEADS):
            qn = q[:, hh * MLA_QK_PAD:hh * MLA_QK_PAD + MLA_NOPE]
            qr = q[:, hh * MLA_QK_PAD + MLA_NOPE:(hh + 1) * MLA_QK_PAD]
            ss = jnp.sum(qn * qn, axis=-1, keepdims=True) + jnp.sum(qr * qr, axis=-1, keepdims=True)
            r = lax.rsqrt(ss / MLA_QK + EPS) * scale
            q_ref[:, hh * MLA_QK_PAD:hh * MLA_QK_PAD + MLA_NOPE] = (qn * r * gq[:, :MLA_NOPE]).astype(BF16)
            qr = qr * r * gq[:, MLA_NOPE:]
            if rope:
                qr = _rope_rolled(qr, cos_t, sin_t)
            q_ref[:, hh * MLA_QK_PAD + MLA_NOPE:(hh + 1) * MLA_QK_PAD] = qr.astype(BF16)


def _pad_rope_cols(r):
    half = MLA_ROPE // 2
    z = jnp.zeros(r.shape[:-1] + (LANES // 2 - half,), r.dtype)
    return jnp.concatenate([r[..., :half], z, r[..., half:], z], axis=-1)


def _mla_prepare(w_down, q_lora_norm, kv_lora_norm, w_uq, w_ukv, q_norm, k_norm):
    d = w_down.shape[0]
    nq = MLA_Q_RANK + MLA_KV_RANK
    wd = jnp.concatenate([w_down[:, :nq], _pad_rope_cols(w_down[:, nq:])], axis=1).astype(BF16)
    wq = w_uq.reshape(MLA_Q_RANK, MLA_HEADS, MLA_QK)
    wq = jnp.concatenate([wq[..., :MLA_NOPE], _pad_rope_cols(wq[..., MLA_NOPE:])], axis=-1)
    wq = wq.reshape(MLA_Q_RANK, MLA_HEADS * MLA_QK_PAD).astype(BF16)
    wkv = w_ukv.reshape(MLA_KV_RANK, MLA_HEADS, MLA_NOPE + MLA_V)
    wuk = wkv[..., :MLA_NOPE].reshape(MLA_KV_RANK, MLA_HEADS * MLA_NOPE).astype(BF16)
    wuv = wkv[..., MLA_NOPE:].reshape(MLA_KV_RANK, MLA_HEADS * MLA_V).astype(BF16)

    def pad_gain(g):
        return jnp.concatenate([g[:MLA_NOPE], _pad_rope_cols(g[MLA_NOPE:])]).reshape(1, MLA_QK_PAD)

    return (wd, q_lora_norm.reshape(1, -1), kv_lora_norm.reshape(1, -1), wq, wuk, wuv,
            pad_gain(q_norm), pad_gain(k_norm))


def _mla_project(h, mod, n1, prep, rope_tabs, with_q):
    b, l, d = h.shape
    tm = _tile(l, 512)
    rope = rope_tabs is not None
    outs = []
    if with_q:
        outs.append((MLA_HEADS * MLA_QK_PAD, BF16))
    outs += [(MLA_HEADS * MLA_QK_PAD, BF16), (MLA_HEADS * MLA_V, BF16)]
    res = _row_call(functools.partial(_mla_proj_body, rope=rope, with_q=with_q, d=d),
                    name="mla_proj", batch=b, length=l, tm=tm, rows=(h,),
                    pos=tuple(rope_tabs) if rope else (), vecs=(mod,), consts=(n1,) + tuple(prep), outs=outs)
    return res if with_q else [None] + list(res)


def _gqa_proj_body(*refs, rope, with_q, d):
    it = iter(refs)
    h_ref = next(it)
    cos_ref = sin_ref = None
    if rope:
        cos_ref, sin_ref = next(it), next(it)
    mod_ref, n1_ref, w_ref, gq_ref, gk_ref = (next(it) for _ in range(5))
    if with_q:
        q_ref = next(it)
    k_ref, v_ref = next(it), next(it)

    dh = GQA_HEAD_DIM
    nq = GQA_HEADS * dh
    nkv = GQA_KV_HEADS * dh
    mod = mod_ref[...]
    x = _norm_mod(h_ref[...], n1_ref[...], mod[:, d:2 * d], mod[:, 0:d]).astype(BF16)
    if rope:
        cos_t, sin_t = cos_ref[...], sin_ref[...]
    kv = _dot(x, w_ref[:, nq:])
    v_ref[...] = kv[:, nkv:].astype(BF16)
    for hh in range(GQA_KV_HEADS):
        kh = _rms(kv[:, hh * dh:(hh + 1) * dh], dh) * gk_ref[...]
        if rope:
            kh = _rope_rolled(kh, cos_t, sin_t)
        k_ref[:, hh * dh:(hh + 1) * dh] = kh.astype(BF16)
    if with_q:
        q = _dot(x, w_ref[:, :nq])
        gq = gq_ref[...] * (dh ** -0.5)
        for hh in range(GQA_HEADS):
            qh = _rms(q[:, hh * dh:(hh + 1) * dh], dh) * gq
            if rope:
                qh = _rope_rolled(qh, cos_t, sin_t)
            q_ref[:, hh * dh:(hh + 1) * dh] = qh.astype(BF16)


def _gqa_project(h, mod, n1, w_qkv, q_norm, k_norm, rope_tabs, with_q):
    b, l, d = h.shape
    tm = _tile(l, 512)
    rope = rope_tabs is not None
    outs = []
    if with_q:
        outs.append((GQA_HEADS * GQA_HEAD_DIM, BF16))
    outs += [(GQA_KV_HEADS * GQA_HEAD_DIM, BF16)] * 2
    res = _row_call(functools.partial(_gqa_proj_body, rope=rope, with_q=with_q, d=d),
                    name="gqa_proj", batch=b, length=l, tm=tm, rows=(h,),
                    pos=tuple(rope_tabs) if rope else (), vecs=(mod,),
                    consts=(n1, w_qkv, q_norm.reshape(1, -1), k_norm.reshape(1, -1)), outs=outs)
    return res if with_q else [None] + list(res)


def _flash_body(*refs, nseg, group, dqk, dv, tq, tks):
    q_ref = refs[0]
    seg_refs = refs[1:1 + 2 * nseg]
    o_ref = refs[1 + 2 * nseg]
    qs_ref, m_ref, l_ref, acc_ref = refs[2 + 2 * nseg:]

    for g in range(group):
        qs_ref[g * tq:(g + 1) * tq, :] = q_ref[:, g * dqk:(g + 1) * dqk]
    m_ref[...] = jnp.full(m_ref.shape, NEG_BIG, F32)
    l_ref[...] = jnp.zeros(l_ref.shape, F32)
    acc_ref[...] = jnp.zeros(acc_ref.shape, F32)

    def chunk(kc, vc):
        s = _dot_nt(qs_ref[...], kc)
        m_prev = m_ref[...]
        m_new = jnp.maximum(m_prev, jnp.max(s, axis=-1, keepdims=True))
        alpha = jnp.exp(m_prev - m_new)
        p = jnp.exp(s - m_new)
        l_ref[...] = alpha * l_ref[...] + jnp.sum(p, axis=-1, keepdims=True)
        acc_ref[...] = alpha * acc_ref[...] + _dot(p.astype(BF16), vc)
        m_ref[...] = m_new

    for sidx in range(nseg):
        k_ref, v_ref = seg_refs[2 * sidx], seg_refs[2 * sidx + 1]
        tk = tks[sidx]
        n = k_ref.shape[0] // tk
        if n == 1:
            chunk(k_ref[...], v_ref[...])
        else:
            def step(c, carry, k_ref=k_ref, v_ref=v_ref, tk=tk):
                off = pl.multiple_of(c * tk, tk)
                chunk(k_ref[pl.ds(off, tk), :], v_ref[pl.ds(off, tk), :])
                return carry
            lax.fori_loop(0, n, step, 0)

    out = acc_ref[...] * (1.0 / l_ref[...])
    for g in range(group):
        o_ref[:, g * dv:(g + 1) * dv] = out[g * tq:(g + 1) * tq, :].astype(o_ref.dtype)


def _flash_attention(q, segs, *, kv_heads, group, dqk, dv, tq, tk):
    b, lq, _ = q.shape
    tq = _tile(lq, tq)
    in_specs = [pl.BlockSpec((None, tq, group * dqk), lambda bb, h, i: (bb, i, h))]
    args = [q]
    tks = []
    for k, v in segs:
        lk = k.shape[1]
        tks.append(_tile(lk, tk))
        in_specs.append(pl.BlockSpec((None, lk, dqk), lambda bb, h, i: (bb, 0, h)))
        in_specs.append(pl.BlockSpec((None, lk, dv), lambda bb, h, i: (bb, 0, h)))
        args += [k, v]
    rows = group * tq
    return pl.pallas_call(
        functools.partial(_flash_body, nseg=len(segs), group=group, dqk=dqk, dv=dv, tq=tq, tks=tuple(tks)),
        grid=(b, kv_heads, lq // tq),
        in_specs=in_specs,
        out_specs=pl.BlockSpec((None, tq, group * dv), lambda bb, h, i: (bb, i, h)),
        out_shape=jax.ShapeDtypeStruct((b, lq, kv_heads * group * dv), BF16),
        scratch_shapes=[pltpu.VMEM((rows, dqk), BF16), pltpu.VMEM((rows, 1), F32), pltpu.VMEM((rows, 1), F32),
                        pltpu.VMEM((rows, dv), F32)],
        compiler_params=_compiler_params(("parallel", "parallel", "parallel")),
        name="flash_attention",
    )(*args)


def _out_proj_body(h_ref, o_ref, mod_ref, w_ref, out_ref, *, d, gate_idx):
    gate = mod_ref[...][:, gate_idx * d:(gate_idx + 1) * d]
    out_ref[...] = h_ref[...] + gate * _dot(o_ref[...], w_ref[...])


def _out_project(h, o, mod, w):
    b, l, d = h.shape
    tm = _tile(l, 512)
    return _row_call(functools.partial(_out_proj_body, d=d, gate_idx=2), name="out_proj", batch=b, length=l,
                     tm=tm, rows=(h, o), vecs=(mod,), consts=(w,), outs=[(d, F32)])[0]


def _ffn_body(h_ref, mod_ref, n2_ref, w13_ref, w2_ref, out_ref, *, d, hidden, chunks):
    h = h_ref[...]
    mod = mod_ref[...]
    x = _norm_mod(h, n2_ref[...], mod[:, 4 * d:5 * d], mod[:, 3 * d:4 * d]).astype(BF16)
    acc = None
    for c0, cw in chunks:
        a1 = _dot(x, w13_ref[:, c0:c0 + cw])
        a3 = _dot(x, w13_ref[:, hidden + c0:hidden + c0 + cw])
        part = _dot((_silu(a1) * a3).astype(BF16), w2_ref[c0:c0 + cw, :])
        acc = part if acc is None else acc + part
    out_ref[...] = h + mod[:, 5 * d:6 * d] * acc


def _hidden_chunks(hidden, width):
    chunks, c0 = [], 0
    while c0 < hidden:
        cw = min(width, hidden - c0)
        chunks.append((c0, cw))
        c0 += cw
    return tuple(chunks)


def _ffn(h, mod, n2, w13, w2):
    b, l, d = h.shape
    hidden = w2.shape[0]
    tm = _tile(l, 512)
    return _row_call(functools.partial(_ffn_body, d=d, hidden=hidden, chunks=_hidden_chunks(hidden, 1024)),
                     name="swiglu", batch=b, length=l, tm=tm, rows=(h,), vecs=(mod,), consts=(n2, w13, w2),
                     outs=[(d, F32)])[0]


def _na_proj_body(*refs, with_q, d):
    it = iter(refs)
    h_ref, mod_ref, n1_ref, w_ref, gq_ref, gk_ref = (next(it) for _ in range(6))
    if with_q:
        q_ref = next(it)
    k_ref, v_ref = next(it), next(it)

    nh = NA_HEADS * NA_HEAD_DIM
    mod = mod_ref[...]
    x = _norm_mod(h_ref[...], n1_ref[...], mod[:, d:2 * d], mod[:, 0:d]).astype(BF16)
    tm = x.shape[0]
    left = lax.broadcasted_iota(jnp.int32, (tm, LANES), 1) < NA_HEAD_DIM

    def pair_norm(t, gain2):
        sq = t * t
        sl = jnp.sum(jnp.where(left, sq, 0.0), axis=-1, keepdims=True)
        sr = jnp.sum(jnp.where(left, 0.0, sq), axis=-1, keepdims=True)
        r = jnp.where(left, lax.rsqrt(sl / NA_HEAD_DIM + EPS), lax.rsqrt(sr / NA_HEAD_DIM + EPS))
        return t * r * gain2

    v_ref[...] = _dot(x, w_ref[:, 2 * nh:]).astype(BF16)
    k = _dot(x, w_ref[:, nh:2 * nh])
    gk = gk_ref[...]
    for hp in range(nh // LANES):
        k_ref[:, hp * LANES:(hp + 1) * LANES] = pair_norm(k[:, hp * LANES:(hp + 1) * LANES], gk).astype(BF16)
    if with_q:
        q = _dot(x, w_ref[:, :nh])
        gq = gq_ref[...] * (NA_HEAD_DIM ** -0.5)
        for hp in range(nh // LANES):
            q_ref[:, hp * LANES:(hp + 1) * LANES] = pair_norm(q[:, hp * LANES:(hp + 1) * LANES], gq).astype(BF16)


def _na_project(h, mod, n1, w_qkv, q_norm, k_norm, with_q):
    b, l, d = h.shape
    tm = _tile(l, 512)
    nh = NA_HEADS * NA_HEAD_DIM
    outs = [(nh, BF16)] * (3 if with_q else 2)
    gq2 = jnp.concatenate([q_norm, q_norm]).reshape(1, LANES)
    gk2 = jnp.concatenate([k_norm, k_norm]).reshape(1, LANES)
    res = _row_call(functools.partial(_na_proj_body, with_q=with_q, d=d), name="na_proj", batch=b, length=l, tm=tm,
                    rows=(h,), vecs=(mod,), consts=(n1, w_qkv, gq2, gk2), outs=outs)
    return res if with_q else [None] + list(res)


def _na_bias_table(rpb, rows):
    wr, wc, w = NA_WIN_ROWS, NA_WIN_COLS, GRID_W
    nsteps = rows // NA_QROWS
    col = np.arange(w)
    col_start = np.clip(col - wc // 2, 0, w - wc)
    pats = []
    for j in (0, 1, nsteps - 1):
        base = int(np.clip(j * NA_QROWS - wr // 2, 0, rows - NA_KROWS))
        r = j * NA_QROWS + np.arange(NA_QROWS)
        r0 = np.clip(r - wr // 2, 0, rows - wr)
        krow = base + np.arange(NA_KROWS)
        row_ok = (krow[None, :] >= r0[:, None]) & (krow[None, :] < r0[:, None] + wr)
        row_rel = krow[None, :] - r[:, None] + (wr - 1)
        col_ok = (col[None, :] >= col_start[:, None]) & (col[None, :] < col_start[:, None] + wc)
        col_rel = col[None, :] - col[:, None] + (wc - 1)
        ok = row_ok[:, None, :, None] & col_ok[None, :, None, :]
        rr = np.broadcast_to(np.clip(row_rel, 0, 2 * wr - 2)[:, None, :, None], ok.shape)
        cr = np.broadcast_to(np.clip(col_rel, 0, 2 * wc - 2)[None, :, None, :], ok.shape)
        n_q, n_k = NA_QROWS * w, NA_KROWS * w
        pats.append((ok.reshape(n_q, n_k), rr.reshape(n_q, n_k), cr.reshape(n_q, n_k)))
    ok = np.stack([p[0] for p in pats])
    rr = np.stack([p[1] for p in pats])
    cr = np.stack([p[2] for p in pats])
    return jnp.where(ok[None], rpb[:, rr, cr], NEG_BIG).astype(F32)


def _na_attn_body(q_ref, k_ref, v_ref, kc_ref, vc_ref, bias_ref, o_ref, *, rows, steps_per_block):
    nq = NA_QROWS * GRID_W
    nk = NA_KROWS * GRID_W
    nsteps = rows // NA_QROWS
    rb = pl.program_id(2)
    left = lax.broadcasted_iota(jnp.int32, (nq, LANES), 1) < NA_HEAD_DIM
    head_masks = (left.astype(F32).astype(BF16), jnp.logical_not(left).astype(F32).astype(BF16))
    kc, vc = kc_ref[...], vc_ref[...]

    def step(it, carry):
        j = rb * steps_per_block + it
        pat = jnp.where(j == 0, 0, jnp.where(j == nsteps - 1, 2, 1))
        base = jnp.clip(j * NA_QROWS - NA_WIN_ROWS // 2, 0, rows - NA_KROWS)
        koff = pl.multiple_of(base * GRID_W, GRID_W)
        qoff = pl.multiple_of(it * nq, nq)
        q = q_ref[pl.ds(qoff, nq), :]
        kw = k_ref[pl.ds(koff, nk), :]
        vw = v_ref[pl.ds(koff, nk), :]
        outs = []
        for hh in range(2):
            qm = q * head_masks[hh]
            s_loc = _dot_nt(qm, kw) + bias_ref[hh, pat]
            s_ctx = _dot_nt(qm, kc)
            m = jnp.maximum(jnp.max(s_loc, axis=-1, keepdims=True), jnp.max(s_ctx, axis=-1, keepdims=True))
            p_loc = jnp.exp(s_loc - m)
            p_ctx = jnp.exp(s_ctx - m)
            denom = jnp.sum(p_loc, axis=-1, keepdims=True) + jnp.sum(p_ctx, axis=-1, keepdims=True)
            o = _dot(p_loc.astype(BF16), vw) + _dot(p_ctx.astype(BF16), vc)
            outs.append(o * (1.0 / denom))
        o_ref[pl.ds(qoff, nq), :] = jnp.where(left, outs[0], outs[1]).astype(o_ref.dtype)
        return carry

    lax.fori_loop(0, steps_per_block, step, 0)


def _na_attention(q, k, v, kc, vc, bias):
    b, s, nh = q.shape
    lc = kc.shape[1]
    rows = s // GRID_W
    nsteps = rows // NA_QROWS
    steps_per_block = 8 if nsteps % 8 == 0 else nsteps
    tq = steps_per_block * NA_QROWS * GRID_W
    nq, nk = bias.shape[2], bias.shape[3]
    return pl.pallas_call(
        functools.partial(_na_attn_body, rows=rows, steps_per_block=steps_per_block),
        grid=(nh // LANES, b, s // tq),
        in_specs=[pl.BlockSpec((None, tq, LANES), lambda hp, bb, i: (bb, i, hp)),
                  pl.BlockSpec((None, s, LANES), lambda hp, bb, i: (bb, 0, hp)),
                  pl.BlockSpec((None, s, LANES), lambda hp, bb, i: (bb, 0, hp)),
                  pl.BlockSpec((None, lc, LANES), lambda hp, bb, i: (bb, 0, hp)),
                  pl.BlockSpec((None, lc, LANES), lambda hp, bb, i: (bb, 0, hp)),
                  pl.BlockSpec((2, 3, nq, nk), lambda hp, bb, i: (hp, 0, 0, 0))],
        out_specs=pl.BlockSpec((None, tq, LANES), lambda hp, bb, i: (bb, i, hp)),
        out_shape=jax.ShapeDtypeStruct((b, s, nh), BF16),
        compiler_params=_compiler_params(("parallel", "parallel", "parallel")),
        name="na_attention",
    )(q, k, v, kc, vc, bias)


def _ret_proj_body(*refs, rope, with_qg, d):
    it = iter(refs)
    h_ref = next(it)
    cos_ref = sin_ref = None
    if rope:
        cos_ref, sin_ref = next(it), next(it)
    mod_ref, n1_ref, w_ref = (next(it) for _ in range(3))
    if with_qg:
        q_ref = next(it)
    k_ref, v_ref = next(it), next(it)
    if with_qg:
        g_ref = next(it)

    nk, nv, dk = RET_HEADS * RET_QK_DIM, RET_HEADS * RET_V_DIM, RET_QK_DIM
    mod = mod_ref[...]
    x = _norm_mod(h_ref[...], n1_ref[...], mod[:, d:2 * d], mod[:, 0:d]).astype(BF16)
    if rope:
        cos, sin = cos_ref[...], sin_ref[...]

    def rotate_store(t, ref, scale):
        for hh in range(RET_HEADS):
            x1 = t[:, hh * dk:hh * dk + dk // 2] * scale
            x2 = t[:, hh * dk + dk // 2:(hh + 1) * dk] * scale
            if rope:
                x1, x2 = x1 * cos - x2 * sin, x1 * sin + x2 * cos
            ref[:, hh * dk:hh * dk + dk // 2] = x1.astype(BF16)
            ref[:, hh * dk + dk // 2:(hh + 1) * dk] = x2.astype(BF16)

    rotate_store(_dot(x, w_ref[:, nk:2 * nk]), k_ref, dk ** -0.5)
    v_ref[...] = _dot(x, w_ref[:, 2 * nk:2 * nk + nv]).astype(BF16)
    if with_qg:
        rotate_store(_dot(x, w_ref[:, :nk]), q_ref, 1.0)
        g_ref[...] = _dot(x, w_ref[:, 2 * nk + nv:])


def _ret_project(h, mod, n1, w_qkvg, rope_tabs, with_qg):
    b, l, d = h.shape
    tm = _tile(l, 512)
    nk, nv = RET_HEADS * RET_QK_DIM, RET_HEADS * RET_V_DIM
    rope = rope_tabs is not None
    outs = [(nk, BF16), (nk, BF16), (nv, BF16), (nv, F32)] if with_qg else [(nk, BF16), (nv, BF16)]
    res = _row_call(functools.partial(_ret_proj_body, rope=rope, with_qg=with_qg, d=d), name="ret_proj", batch=b,
                    length=l, tm=tm, rows=(h,), pos=tuple(rope_tabs) if rope else (), vecs=(mod,),
                    consts=(n1, w_qkvg), outs=outs)
    return res if with_qg else [None, res[0], res[1], None]


def _ret_scan_body(*refs, emit_y, chunk, nchunks):
    if emit_y:
        ld_ref, q_ref, k_ref, v_ref, s0_ref, y_ref, sfin_ref, st_ref = refs
    else:
        ld_ref, k_ref, v_ref, s0_ref, sfin_ref, st_ref = refs
        q_ref = y_ref = None
    direction, hh, t = pl.program_id(0), pl.program_id(2), pl.program_id(3)
    ld = ld_ref[direction, hh]

    @pl.when(t == 0)
    def _():
        st_ref[...] = s0_ref[...]

    ri = lax.broadcasted_iota(jnp.int32, (chunk, chunk), 0)
    ci = lax.broadcasted_iota(jnp.int32, (chunk, chunk), 1)
    dist = jnp.where(direction == 0, ri - ci, ci - ri).astype(F32)
    intra = jnp.where(dist >= 0, jnp.exp(jnp.maximum(dist, 0.0) * ld), 0.0)
    pos = lax.broadcasted_iota(jnp.int32, (chunk, 1), 0)
    rank = jnp.where(direction == 0, pos, chunk - 1 - pos).astype(F32)
    q_decay = jnp.exp((rank + 1.0) * ld)
    k_decay = jnp.exp((chunk - 1.0 - rank) * ld)
    chunk_decay = jnp.exp(jnp.full((1, 1), float(chunk), F32) * ld)

    for c in range(nchunks):
        cc = jnp.where(direction == 0, c, nchunks - 1 - c)
        off = pl.multiple_of(cc * chunk, chunk)
        k = k_ref[pl.ds(off, chunk), :]
        v = v_ref[pl.ds(off, chunk), :]
        state = st_ref[...]
        if emit_y:
            q = q_ref[pl.ds(off, chunk), :]
            sc = _dot_nt(q, k) * intra
            inner = _dot(sc.astype(BF16), v)
            cross = _dot(q, state.astype(BF16)) * q_decay
            y_ref[pl.ds(off, chunk), :] = inner + cross
        kd = (k.astype(F32) * k_decay).astype(BF16)
        st_ref[...] = state * chunk_decay + _dot_tn(kd, v)

    @pl.when(t == pl.num_programs(3) - 1)
    def _():
        sfin_ref[...] = st_ref[...]


def _ret_scan(log_decay, q, k, v, init_state, emit_y):
    b, l, _ = k.shape
    dk, dv, nh = RET_QK_DIM, RET_V_DIM, RET_HEADS
    chunk = 256 if l % 256 == 0 else 128
    tt = 1024 if l % 1024 == 0 else chunk
    nchunks = tt // chunk
    nt = l // tt

    def tok(d, t):
        return jnp.where(d == 0, t, nt - 1 - t)

    in_specs = [pl.BlockSpec(memory_space=pltpu.SMEM)]
    args = [log_decay]
    if emit_y:
        in_specs.append(pl.BlockSpec((None, tt, dk), lambda d, bb, h, t: (bb, tok(d, t), h)))
        args.append(q)
    in_specs += [pl.BlockSpec((None, tt, dk), lambda d, bb, h, t: (bb, tok(d, t), h)),
                 pl.BlockSpec((None, tt, dv), lambda d, bb, h, t: (bb, tok(d, t), h)),
                 pl.BlockSpec((None, None, None, dk, dv), lambda d, bb, h, t: (d, bb, h, 0, 0))]
    args += [k, v, init_state]
    out_shape, out_specs = [], []
    if emit_y:
        out_shape.append(jax.ShapeDtypeStruct((b, l, 2 * nh * dv), F32))
        out_specs.append(pl.BlockSpec((None, tt, dv), lambda d, bb, h, t: (bb, tok(d, t), d * nh + h)))
    out_shape.append(jax.ShapeDtypeStruct((2, b, nh, dk, dv), F32))
    out_specs.append(pl.BlockSpec((None, None, None, dk, dv), lambda d, bb, h, t: (d, bb, h, 0, 0)))
    res = pl.pallas_call(
        functools.partial(_ret_scan_body, emit_y=emit_y, chunk=chunk, nchunks=nchunks),
        grid=(2, b, nh, nt),
        in_specs=in_specs,
        out_specs=out_specs,
        out_shape=out_shape,
        scratch_shapes=[pltpu.VMEM((dk, dv), F32)],
        compiler_params=_compiler_params(("parallel", "parallel", "parallel", "arbitrary")),
        name="ret_scan" if emit_y else "ret_ctx_state",
    )(*args)
    return (res[0], res[1]) if emit_y else (None, res[0])


def _ret_out_body(h_ref, y_ref, g_ref, mod_ref, gn_ref, w_ref, out_ref, *, d):
    nv, dv = RET_HEADS * RET_V_DIM, RET_V_DIM
    y2 = y_ref[...]
    y = y2[:, :nv] + y2[:, nv:]
    gate = _silu(g_ref[...])
    gn = gn_ref[...]
    acc = None
    for hh in range(RET_HEADS):
        yh = y[:, hh * dv:(hh + 1) * dv]
        mu = jnp.mean(yh, axis=-1, keepdims=True)
        yc = yh - mu
        var = jnp.mean(yc * yc, axis=-1, keepdims=True)
        z = yc * lax.rsqrt(var + EPS) * gn[:, hh * dv:(hh + 1) * dv] * gate[:, hh * dv:(hh + 1) * dv]
        part = _dot(z.astype(BF16), w_ref[hh * dv:(hh + 1) * dv, :])
        acc = part if acc is None else acc + part
    out_ref[...] = h_ref[...] + mod_ref[...][:, 2 * d:3 * d] * acc


def _ret_output(h, y, g, mod, out_norm, w_o):
    b, l, d = h.shape
    tm = _tile(l, 256)
    return _row_call(functools.partial(_ret_out_body, d=d), name="ret_out", batch=b, length=l, tm=tm,
                     rows=(h, y, g), vecs=(mod,), consts=(out_norm.reshape(1, -1), w_o), outs=[(d, F32)])[0]


def kernel(x, c, ctx, c_ctx, mod_w, mod_b, norm1, norm2, ffn_w13, ffn_w2, mla_w_down, mla_q_lora_norm, mla_kv_lora_norm, mla_w_uq, mla_w_ukv, mla_q_norm, mla_k_norm, mla_w_o, gqa_w_qkv, gqa_q_norm, gqa_k_norm, gqa_w_o, na_w_qkv, na_q_norm, na_k_norm, na_rpb, na_w_o, ret_w_qkvg, ret_log_decay_fwd, ret_log_decay_bwd, ret_out_norm, ret_w_o):
    b, s, d = x.shape
    lc = ctx.shape[1]
    depth = mod_w.shape[0]

    pad_rows = -(b + 1) % 16
    c_rows = jnp.concatenate([c, c_ctx[None, :], jnp.zeros((pad_rows, d), F32)], axis=0)
    mods = _modulation(c_rows, mod_w, mod_b)

    h, hc = x, ctx
    for i in range(depth):
        need_ctx = i < depth - 1
        mod_l = mods[i, :b].reshape(b, 1, 6 * d)
        mod_c = mods[i, b:b + 1].reshape(1, 1, 6 * d)
        n1 = norm1[i].reshape(1, d)
        n2 = norm2[i].reshape(1, d)
        kind, j = i % 4, i // 4
        if kind == 0:
            prep = _mla_prepare(mla_w_down[j], mla_q_lora_norm[j], mla_kv_lora_norm[j], mla_w_uq[j], mla_w_ukv[j],
                                mla_q_norm[j], mla_k_norm[j])
            tabs = _rope_tables_rolled(s, MLA_ROPE, LANES // 2)
            ql, kl, vl = _mla_project(h, mod_l, n1, prep, tabs, True)
            qc, kc, vc = _mla_project(hc, mod_c, n1, prep, None, need_ctx)
            attn = functools.partial(_flash_attention, kv_heads=MLA_HEADS, group=1, dqk=MLA_QK_PAD, dv=MLA_V,
                                     tq=512, tk=512)
            w_o = mla_w_o[j].astype(BF16)
        elif kind == 1:
            w_qkv = gqa_w_qkv[j].astype(BF16)
            tabs = _rope_tables_rolled(s, GQA_HEAD_DIM, GQA_HEAD_DIM // 2)
            ql, kl, vl = _gqa_project(h, mod_l, n1, w_qkv, gqa_q_norm[j], gqa_k_norm[j], tabs, True)
            qc, kc, vc = _gqa_project(hc, mod_c, n1, w_qkv, gqa_q_norm[j], gqa_k_norm[j], None, need_ctx)
            attn = functools.partial(_flash_attention, kv_heads=GQA_KV_HEADS, group=GQA_HEADS // GQA_KV_HEADS,
                                     dqk=GQA_HEAD_DIM, dv=GQA_HEAD_DIM, tq=256, tk=512)
            w_o = gqa_w_o[j].astype(BF16)
        elif kind == 2:
            w_qkv = na_w_qkv[j].astype(BF16)
            ql, kl, vl = _na_project(h, mod_l, n1, w_qkv, na_q_norm[j], na_k_norm[j], True)
            qc, kc, vc = _na_project(hc, mod_c, n1, w_qkv, na_q_norm[j], na_k_norm[j], need_ctx)
            w_o = na_w_o[j].astype(BF16)
        else:
            w_qkvg = ret_w_qkvg[j].astype(BF16)
            tabs = _rope_cos_sin(s, RET_QK_DIM)
            ql, kl, vl, gl = _ret_project(h, mod_l, n1, w_qkvg, tabs, True)
            qc, kc, vc, gc = _ret_project(hc, mod_c, n1, w_qkvg, None, need_ctx)
            w_o = ret_w_o[j].astype(BF16)

        if kind in (0, 1):
            ol = attn(ql, [(kc, vc), (kl, vl)])
            h = _out_project(h, ol, mod_l, w_o)
            if need_ctx:
                oc = attn(qc, [(kc, vc)])
                hc = _out_project(hc, oc, mod_c, w_o)
        elif kind == 2:
            bias = _na_bias_table(na_rpb[j], s // GRID_W)
            ol = _na_attention(ql, kl, vl, kc, vc, bias)
            h = _out_project(h, ol, mod_l, w_o)
            if need_ctx:
                def heads_major(t):
                    return t.reshape(b, lc, NA_HEADS, NA_HEAD_DIM).transpose(0, 2, 1, 3).reshape(
                        b * NA_HEADS, lc, NA_HEAD_DIM)
                oc = _flash_attention(heads_major(qc), [(heads_major(kc), heads_major(vc))], kv_heads=1, group=1,
                                      dqk=NA_HEAD_DIM, dv=NA_HEAD_DIM, tq=256, tk=256)
                oc = oc.reshape(b, NA_HEADS, lc, NA_HEAD_DIM).transpose(0, 2, 1, 3).reshape(b, lc, -1)
                hc = _out_project(hc, oc, mod_c, w_o)
        else:
            log_decay = jnp.stack([ret_log_decay_fwd[j], ret_log_decay_bwd[j]]).astype(F32)
            zero_state = jnp.zeros((2, b, RET_HEADS, RET_QK_DIM, RET_V_DIM), F32)
            _, ctx_state = _ret_scan(log_decay, None, kc, vc, zero_state, False)
            y, _ = _ret_scan(log_decay, ql, kl, vl, ctx_state, True)
            h = _ret_output(h, y, gl, mod_l, ret_out_norm[j], w_o)
            if need_ctx:
                raise NotImplementedError("context update after a retention layer is not needed at this depth")

        h = _ffn(h, mod_l, n2, ffn_w13[i].astype(BF16), ffn_w2[i].astype(BF16))
        if need_ctx:
            hc = _ffn(hc, mod_c, n2, ffn_w13[i].astype(BF16), ffn_w2[i].astype(BF16))
    return h
```

```python
import functools

import numpy as np
import jax
import jax.numpy as jnp
from jax import lax
from jax.experimental import pallas as pl
from jax.experimental.pallas import tpu as pltpu

F32 = jnp.float32
BF16 = jnp.bfloat16

GRID_W = 64
ROPE_THETA = 10000.0
EPS = 1e-6
MLA_HEADS, MLA_Q_RANK, MLA_KV_RANK, MLA_NOPE, MLA_ROPE, MLA_V = 8, 384, 256, 128, 64, 128
MLA_QK = MLA_NOPE + MLA_ROPE
MLA_QK_PAD = 256
GQA_HEADS, GQA_KV_HEADS, GQA_HEAD_DIM = 8, 2, 128
NA_HEADS, NA_HEAD_DIM, NA_WIN_ROWS, NA_WIN_COLS = 16, 64, 8, 16
RET_HEADS, RET_QK_DIM, RET_V_DIM = 4, 256, 512

LANES = 128
V7X_VMEM_LIMIT_BYTES = 56 * 1024 * 1024
NEG_BIG = -1e30

NA_QROWS = 4
NA_KROWS = NA_QROWS + NA_WIN_ROWS


def _dot(a, b):
    return jnp.dot(a, b, preferred_element_type=F32)


def _dot_nt(a, b):
    return lax.dot_general(a, b, (((1,), (1,)), ((), ())), preferred_element_type=F32)


def _dot_tn(a, b):
    return lax.dot_general(a, b, (((0,), (0,)), ((), ())), preferred_element_type=F32)


def _silu(x):
    return x / (1.0 + jnp.exp(-x))


def _norm_mod(h, gain, scale, shift):
    y = h * lax.rsqrt(jnp.mean(h * h, axis=-1, keepdims=True) + EPS) * gain
    return y * (1.0 + scale) + shift


def _rms(x, denom):
    return x * lax.rsqrt(jnp.sum(x * x, axis=-1, keepdims=True) / denom + EPS)


def _compiler_params(semantics):
    return pltpu.CompilerParams(dimension_semantics=semantics, vmem_limit_bytes=V7X_VMEM_LIMIT_BYTES)


def _row_call(body, *, name, batch, length, tm, rows=(), pos=(), vecs=(), consts=(), outs=()):
    outs = [o if len(o) == 5 else (o[0], o[1], length, 0, None) for o in outs]
    aliased = [(k, o[4]) for k, o in enumerate(outs) if o[4] is not None]
    n_in = len(rows) + len(pos) + len(vecs) + len(consts)
    if aliased:
        inner = body

        def body(*refs):
            inner(*refs[:n_in], *refs[n_in + len(aliased):])
    in_specs = []
    for a in rows:
        in_specs.append(pl.BlockSpec((None, tm, a.shape[-1]), lambda b, i: (b, i, 0)))
    for a in pos:
        in_specs.append(pl.BlockSpec((tm, a.shape[-1]), lambda b, i: (i, 0)))
    for a in vecs:
        if a.shape[0] == 1:
            in_specs.append(pl.BlockSpec((None, 1, a.shape[-1]), lambda b, i: (0, 0, 0)))
        else:
            in_specs.append(pl.BlockSpec((None, 1, a.shape[-1]), lambda b, i: (b, 0, 0)))
    for a in consts:
        nd = a.ndim
        in_specs.append(pl.BlockSpec(a.shape, lambda b, i, nd=nd: (0,) * nd, pipeline_mode=pl.Buffered(1)))
    in_specs += [pl.BlockSpec(memory_space=pl.ANY) for _ in aliased]
    out_shape = [jax.ShapeDtypeStruct((batch, total, f), dt) for f, dt, total, _, _ in outs]
    out_specs = [pl.BlockSpec((None, tm, f), lambda b, i, off=off // tm: (b, i + off, 0)) for f, _, _, off, _ in outs]
    assert all(off % tm == 0 for _, _, _, off, _ in outs)
    res = pl.pallas_call(
        body,
        grid=(batch, length // tm),
        in_specs=in_specs,
        out_specs=out_specs,
        out_shape=out_shape,
        input_output_aliases={n_in + a: k for a, (k, _) in enumerate(aliased)},
        compiler_params=_compiler_params(("parallel", "parallel")),
        name=name,
    )(*rows, *pos, *vecs, *consts, *[buf for _, buf in aliased])
    return res


def _tile(length, pref):
    return pref if length % pref == 0 else length


def _mod_body(c_ref, w_ref, b_ref, o_ref):
    s = _silu(c_ref[...]).astype(BF16)
    o_ref[...] = _dot(s, w_ref[...].astype(BF16)) + b_ref[...]


def _modulation(c_rows, mod_w, mod_b):
    depth, d, n = mod_w.shape
    tn = 1536 if n % 1536 == 0 else n
    return pl.pallas_call(
        _mod_body,
        grid=(depth, n // tn),
        in_specs=[pl.BlockSpec(c_rows.shape, lambda l, j: (0, 0)),
                  pl.BlockSpec((None, d, tn), lambda l, j: (l, 0, j)),
                  pl.BlockSpec((None, 1, tn), lambda l, j: (l, 0, j))],
        out_specs=pl.BlockSpec((None, c_rows.shape[0], tn), lambda l, j: (l, 0, j)),
        out_shape=jax.ShapeDtypeStruct((depth, c_rows.shape[0], n), F32),
        compiler_params=_compiler_params(("parallel", "parallel")),
        name="modulation",
    )(c_rows, mod_w, mod_b.reshape(depth, 1, n))


def _rope_cos_sin(length, dim):
    t = jnp.arange(length)
    row = (t // GRID_W).astype(F32)
    col = (t % GRID_W).astype(F32)
    quarter = dim // 4
    inv_freq = ROPE_THETA ** (-jnp.arange(quarter, dtype=F32) / quarter)
    ang = jnp.concatenate([row[:, None] * inv_freq, col[:, None] * inv_freq], axis=-1)
    return jnp.cos(ang), jnp.sin(ang)


def _rope_tables_rolled(length, dim, lane_half):
    cos, sin = _rope_cos_sin(length, dim)
    pad = jnp.zeros((length, lane_half - dim // 2), F32)
    cos_t = jnp.concatenate([cos, pad, cos, pad], axis=-1)
    sin_t = jnp.concatenate([-sin, pad, sin, pad], axis=-1)
    return cos_t, sin_t


def _rope_rolled(x, cos_t, sin_t):
    return x * cos_t + pltpu.roll(x, x.shape[-1] // 2, 1) * sin_t


def _mla_proj_body(*refs, rope, with_q, d):
    it = iter(refs)
    h_ref = next(it)
    cos_ref = sin_ref = None
    if rope:
        cos_ref, sin_ref = next(it), next(it)
    mod_ref, n1_ref, wd_ref, gql_ref, gkl_ref, wuq_ref, wuk_ref, wuv_ref, gq_ref, gk_ref = (next(it) for _ in range(10))
    if with_q:
        q_ref = next(it)
    k_ref, v_ref = next(it), next(it)

    mod = mod_ref[...]
    x = _norm_mod(h_ref[...], n1_ref[...], mod[:, d:2 * d], mod[:, 0:d]).astype(BF16)
    dl = _dot(x, wd_ref[...])
    ckv = (_rms(dl[:, MLA_Q_RANK:MLA_Q_RANK + MLA_KV_RANK], MLA_KV_RANK) * gkl_ref[...]).astype(BF16)
    kr = dl[:, MLA_Q_RANK + MLA_KV_RANK:]
    kn = _dot(ckv, wuk_ref[...])
    v_ref[...] = _dot(ckv, wuv_ref[...]).astype(BF16)
    if rope:
        cos_t, sin_t = cos_ref[...], sin_ref[...]
    gk = gk_ref[...]
    kr_ss = jnp.sum(kr * kr, axis=-1, keepdims=True)
    for hh in range(MLA_HEADS):
        kh = kn[:, hh * MLA_NOPE:(hh + 1) * MLA_NOPE]
        r = lax.rsqrt((jnp.sum(kh * kh, axis=-1, keepdims=True) + kr_ss) / MLA_QK + EPS)
        k_ref[:, hh * MLA_QK_PAD:hh * MLA_QK_PAD + MLA_NOPE] = (kh * r * gk[:, :MLA_NOPE]).astype(BF16)
        krh = kr * r * gk[:, MLA_NOPE:]
        if rope:
            krh = _rope_rolled(krh, cos_t, sin_t)
        k_ref[:, hh * MLA_QK_PAD + MLA_NOPE:(hh + 1) * MLA_QK_PAD] = krh.astype(BF16)
    if with_q:
        cq = (_rms(dl[:, :MLA_Q_RANK], MLA_Q_RANK) * gql_ref[...]).astype(BF16)
        q = _dot(cq, wuq_ref[...])
        gq = gq_ref[...]
        scale = MLA_QK ** -0.5
        for hh in range(MLA_HEADS):
            qn = q[:, hh * MLA_QK_PAD:hh * MLA_QK_PAD + MLA_NOPE]
            qr = q[:, hh * MLA_QK_PAD + MLA_NOPE:(hh + 1) * MLA_QK_PAD]
            ss = jnp.sum(qn * qn, axis=-1, keepdims=True) + jnp.sum(qr * qr, axis=-1, keepdims=True)
            r = lax.rsqrt(ss / MLA_QK + EPS) * scale
            q_ref[:, hh * MLA_QK_PAD:hh * MLA_QK_PAD + MLA_NOPE] = (qn * r * gq[:, :MLA_NOPE]).astype(BF16)
            qr = qr * r * gq[:, MLA_NOPE:]
            if rope:
                qr = _rope_rolled(qr, cos_t, sin_t)
            q_ref[:, hh * MLA_QK_PAD + MLA_NOPE:(hh + 1) * MLA_QK_PAD] = qr.astype(BF16)


def _pad_rope_cols(r):
    half = MLA_ROPE // 2
    z = jnp.zeros(r.shape[:-1] + (LANES // 2 - half,), r.dtype)
    return jnp.concatenate([r[..., :half], z, r[..., half:], z], axis=-1)


def _mla_prepare(w_down, q_lora_norm, kv_lora_norm, w_uq, w_ukv, q_norm, k_norm):
    d = w_down.shape[0]
    nq = MLA_Q_RANK + MLA_KV_RANK
    wd = jnp.concatenate([w_down[:, :nq], _pad_rope_cols(w_down[:, nq:])], axis=1).astype(BF16)
    wq = w_uq.reshape(MLA_Q_RANK, MLA_HEADS, MLA_QK)
    wq = jnp.concatenate([wq[..., :MLA_NOPE], _pad_rope_cols(wq[..., MLA_NOPE:])], axis=-1)
    wq = wq.reshape(MLA_Q_RANK, MLA_HEADS * MLA_QK_PAD).astype(BF16)
    wkv = w_ukv.reshape(MLA_KV_RANK, MLA_HEADS, MLA_NOPE + MLA_V)
    wuk = wkv[..., :MLA_NOPE].reshape(MLA_KV_RANK, MLA_HEADS * MLA_NOPE).astype(BF16)
    wuv = wkv[..., MLA_NOPE:].reshape(MLA_KV_RANK, MLA_HEADS * MLA_V).astype(BF16)

    def pad_gain(g):
        return jnp.concatenate([g[:MLA_NOPE], _pad_rope_cols(g[MLA_NOPE:])]).reshape(1, MLA_QK_PAD)

    return (wd, q_lora_norm.reshape(1, -1), kv_lora_norm.reshape(1, -1), wq, wuk, wuv,
            pad_gain(q_norm), pad_gain(k_norm))


def _kv_outs(widths, kv_dest):
    total, off, bufs = kv_dest
    return [(f, BF16, total, off, None if bufs is None else bufs[k]) for k, f in enumerate(widths)]


def _mla_project(h, mod, n1, prep, rope_tabs, with_q, kv_dest):
    b, l, d = h.shape
    tm = _tile(l, 512)
    rope = rope_tabs is not None
    outs = []
    if with_q:
        outs.append((MLA_HEADS * MLA_QK_PAD, BF16))
    outs += _kv_outs((MLA_HEADS * MLA_QK_PAD, MLA_HEADS * MLA_V), kv_dest)
    res = _row_call(functools.partial(_mla_proj_body, rope=rope, with_q=with_q, d=d),
                    name="mla_proj", batch=b, length=l, tm=tm, rows=(h,),
                    pos=tuple(rope_tabs) if rope else (), vecs=(mod,), consts=(n1,) + tuple(prep), outs=outs)
    return res if with_q else [None] + list(res)


def _gqa_proj_body(*refs, rope, with_q, d):
    it = iter(refs)
    h_ref = next(it)
    cos_ref = sin_ref = None
    if rope:
        cos_ref, sin_ref = next(it), next(it)
    mod_ref, n1_ref, w_ref, gq_ref, gk_ref = (next(it) for _ in range(5))
    if with_q:
        q_ref = next(it)
    k_ref, v_ref = next(it), next(it)

    dh = GQA_HEAD_DIM
    nq = GQA_HEADS * dh
    nkv = GQA_KV_HEADS * dh
    mod = mod_ref[...]
    x = _norm_mod(h_ref[...], n1_ref[...], mod[:, d:2 * d], mod[:, 0:d]).astype(BF16)
    if rope:
        cos_t, sin_t = cos_ref[...], sin_ref[...]
    kv = _dot(x, w_ref[:, nq:])
    v_ref[...] = kv[:, nkv:].astype(BF16)
    for hh in range(GQA_KV_HEADS):
        kh = _rms(kv[:, hh * dh:(hh + 1) * dh], dh) * gk_ref[...]
        if rope:
            kh = _rope_rolled(kh, cos_t, sin_t)
        k_ref[:, hh * dh:(hh + 1) * dh] = kh.astype(BF16)
    if with_q:
        q = _dot(x, w_ref[:, :nq])
        gq = gq_ref[...] * (dh ** -0.5)
        for hh in range(GQA_HEADS):
            qh = _rms(q[:, hh * dh:(hh + 1) * dh], dh) * gq
            if rope:
                qh = _rope_rolled(qh, cos_t, sin_t)
            q_ref[:, hh * dh:(hh + 1) * dh] = qh.astype(BF16)


def _gqa_project(h, mod, n1, w_qkv, q_norm, k_norm, rope_tabs, with_q, kv_dest):
    b, l, d = h.shape
    tm = _tile(l, 512)
    rope = rope_tabs is not None
    outs = []
    if with_q:
        outs.append((GQA_HEADS * GQA_HEAD_DIM, BF16))
    outs += _kv_outs((GQA_KV_HEADS * GQA_HEAD_DIM,) * 2, kv_dest)
    res = _row_call(functools.partial(_gqa_proj_body, rope=rope, with_q=with_q, d=d),
                    name="gqa_proj", batch=b, length=l, tm=tm, rows=(h,),
                    pos=tuple(rope_tabs) if rope else (), vecs=(mod,),
                    consts=(n1, w_qkv, q_norm.reshape(1, -1), k_norm.reshape(1, -1)), outs=outs)
    return res if with_q else [None] + list(res)


def _lane_fit(x, width):
    if width <= LANES:
        return x[:, :width]
    return jnp.concatenate([x] * (width // LANES), axis=1)


def _flash_body(q_ref, k_ref, v_ref, o_ref, qs_ref, s0, s1, p0, p1, a0, a1, m_ref, l_ref, acc_ref, *,
                group, dqk, dv, tq, tk, n):
    s_bufs, p_bufs, a_bufs = (s0, s1), (p0, p1), (a0, a1)
    for g in range(group):
        qs_ref[g * tq:(g + 1) * tq, :] = q_ref[:, g * dqk:(g + 1) * dqk]
    m_ref[...] = jnp.full(m_ref.shape, NEG_BIG, F32)
    l_ref[...] = jnp.zeros(l_ref.shape, F32)
    acc_ref[...] = jnp.zeros(acc_ref.shape, F32)

    def rows_of(j):
        return pl.ds(pl.multiple_of(j * tk, tk), tk)

    def scores(j, slot):
        s_bufs[slot][...] = _dot_nt(qs_ref[...], k_ref[rows_of(j), :])

    def softmax(slot):
        s = s_bufs[slot][...]
        m_prev = m_ref[...]
        m_new = jnp.maximum(m_prev, jnp.max(s, axis=-1, keepdims=True))
        alpha = jnp.exp(m_prev - m_new)
        p = jnp.exp(s - _lane_fit(m_new, tk))
        l_ref[...] = alpha * l_ref[...] + jnp.sum(p, axis=-1, keepdims=True)
        m_ref[...] = m_new
        a_bufs[slot][...] = alpha
        p_bufs[slot][...] = p.astype(BF16)

    def weighted_values(j, slot):
        alpha = _lane_fit(a_bufs[slot][...], dv)
        acc_ref[...] = alpha * acc_ref[...] + _dot(p_bufs[slot][...], v_ref[rows_of(j), :])

    def steady(j, slot):
        scores(j + 1, 1 - slot)
        softmax(slot)
        weighted_values(j - 1, 1 - slot)

    scores(0, 0)
    if n > 1:
        scores(1, 1)
        softmax(0)
        n_pairs = (n - 2) // 2

        def pair(jj, carry):
            steady(1 + 2 * jj, 1)
            steady(2 + 2 * jj, 0)
            return carry

        if n_pairs > 0:
            lax.fori_loop(0, n_pairs, pair, 0)
        if (n - 2) % 2:
            steady(n - 2, (n - 2) % 2)
        softmax((n - 1) % 2)
        weighted_values(n - 2, (n - 2) % 2)
    else:
        softmax(0)
    weighted_values(n - 1, (n - 1) % 2)

    out = acc_ref[...] * _lane_fit(1.0 / l_ref[...], dv)
    for g in range(group):
        o_ref[:, g * dv:(g + 1) * dv] = out[g * tq:(g + 1) * tq, :].astype(o_ref.dtype)


def _key_chunk(length):
    for tk in (768, 512, 256, 128):
        if length % tk == 0:
            return tk
    return length


def _flash_attention(q, k, v, *, kv_len, kv_block, kv_heads, group, dqk, dv, tq):
    b, lq, _ = q.shape
    tq = _tile(lq, tq)
    tk = _key_chunk(kv_len)
    rows = group * tq
    return pl.pallas_call(
        functools.partial(_flash_body, group=group, dqk=dqk, dv=dv, tq=tq, tk=tk, n=kv_len // tk),
        grid=(b, kv_heads, lq // tq),
        in_specs=[pl.BlockSpec((None, tq, group * dqk), lambda bb, h, i: (bb, i, h)),
                  pl.BlockSpec((None, kv_len, dqk), lambda bb, h, i: (bb, kv_block, h)),
                  pl.BlockSpec((None, kv_len, dv), lambda bb, h, i: (bb, kv_block, h))],
        out_specs=pl.BlockSpec((None, tq, group * dv), lambda bb, h, i: (bb, i, h)),
        out_shape=jax.ShapeDtypeStruct((b, lq, kv_heads * group * dv), BF16),
        scratch_shapes=[pltpu.VMEM((rows, dqk), BF16),
                        pltpu.VMEM((rows, tk), F32), pltpu.VMEM((rows, tk), F32),
                        pltpu.VMEM((rows, tk), BF16), pltpu.VMEM((rows, tk), BF16),
                        pltpu.VMEM((rows, LANES), F32), pltpu.VMEM((rows, LANES), F32),
                        pltpu.VMEM((rows, LANES), F32), pltpu.VMEM((rows, LANES), F32),
                        pltpu.VMEM((rows, dv), F32)],
        compiler_params=_compiler_params(("parallel", "parallel", "parallel")),
        name="flash_attention",
    )(q, k, v)


def _out_proj_body(h_ref, o_ref, mod_ref, w_ref, out_ref, *, d, gate_idx):
    gate = mod_ref[...][:, gate_idx * d:(gate_idx + 1) * d]
    out_ref[...] = h_ref[...] + gate * _dot(o_ref[...], w_ref[...])


def _out_project(h, o, mod, w):
    b, l, d = h.shape
    tm = _tile(l, 512)
    return _row_call(functools.partial(_out_proj_body, d=d, gate_idx=2), name="out_proj", batch=b, length=l,
                     tm=tm, rows=(h, o), vecs=(mod,), consts=(w,), outs=[(d, F32)])[0]


def _ffn_body(h_ref, mod_ref, n2_ref, w13_ref, w2_ref, out_ref, *, d, hidden, chunks):
    h = h_ref[...]
    mod = mod_ref[...]
    x = _norm_mod(h, n2_ref[...], mod[:, 4 * d:5 * d], mod[:, 3 * d:4 * d]).astype(BF16)
    acc = None
    for c0, cw in chunks:
        a1 = _dot(x, w13_ref[:, c0:c0 + cw])
        a3 = _dot(x, w13_ref[:, hidden + c0:hidden + c0 + cw])
        part = _dot((_silu(a1) * a3).astype(BF16), w2_ref[c0:c0 + cw, :])
        acc = part if acc is None else acc + part
    out_ref[...] = h + mod[:, 5 * d:6 * d] * acc


def _hidden_chunks(hidden, width):
    chunks, c0 = [], 0
    while c0 < hidden:
        cw = min(width, hidden - c0)
        chunks.append((c0, cw))
        c0 += cw
    return tuple(chunks)


def _ffn(h, mod, n2, w13, w2):
    b, l, d = h.shape
    hidden = w2.shape[0]
    tm = _tile(l, 512)
    return _row_call(functools.partial(_ffn_body, d=d, hidden=hidden, chunks=_hidden_chunks(hidden, 1024)),
                     name="swiglu", batch=b, length=l, tm=tm, rows=(h,), vecs=(mod,), consts=(n2, w13, w2),
                     outs=[(d, F32)])[0]


def _na_proj_body(*refs, with_q, d):
    it = iter(refs)
    h_ref, mod_ref, n1_ref, w_ref, gq_ref, gk_ref = (next(it) for _ in range(6))
    if with_q:
        q_ref = next(it)
    k_ref, v_ref = next(it), next(it)

    nh = NA_HEADS * NA_HEAD_DIM
    mod = mod_ref[...]
    x = _norm_mod(h_ref[...], n1_ref[...], mod[:, d:2 * d], mod[:, 0:d]).astype(BF16)
    tm = x.shape[0]
    left = lax.broadcasted_iota(jnp.int32, (tm, LANES), 1) < NA_HEAD_DIM

    def pair_norm(t, gain2):
        sq = t * t
        sl = jnp.sum(jnp.where(left, sq, 0.0), axis=-1, keepdims=True)
        sr = jnp.sum(jnp.where(left, 0.0, sq), axis=-1, keepdims=True)
        r = jnp.where(left, lax.rsqrt(sl / NA_HEAD_DIM + EPS), lax.rsqrt(sr / NA_HEAD_DIM + EPS))
        return t * r * gain2

    v_ref[...] = _dot(x, w_ref[:, 2 * nh:]).astype(BF16)
    k = _dot(x, w_ref[:, nh:2 * nh])
    gk = gk_ref[...]
    for hp in range(nh // LANES):
        k_ref[:, hp * LANES:(hp + 1) * LANES] = pair_norm(k[:, hp * LANES:(hp + 1) * LANES], gk).astype(BF16)
    if with_q:
        q = _dot(x, w_ref[:, :nh])
        gq = gq_ref[...] * (NA_HEAD_DIM ** -0.5)
        for hp in range(nh // LANES):
            q_ref[:, hp * LANES:(hp + 1) * LANES] = pair_norm(q[:, hp * LANES:(hp + 1) * LANES], gq).astype(BF16)


def _na_project(h, mod, n1, w_qkv, q_norm, k_norm, with_q, kv_dest):
    b, l, d = h.shape
    tm = _tile(l, 512)
    nh = NA_HEADS * NA_HEAD_DIM
    outs = ([(nh, BF16)] if with_q else []) + _kv_outs((nh, nh), kv_dest)
    gq2 = jnp.concatenate([q_norm, q_norm]).reshape(1, LANES)
    gk2 = jnp.concatenate([k_norm, k_norm]).reshape(1, LANES)
    res = _row_call(functools.partial(_na_proj_body, with_q=with_q, d=d), name="na_proj", batch=b, length=l, tm=tm,
                    rows=(h,), vecs=(mod,), consts=(n1, w_qkv, gq2, gk2), outs=outs)
    return res if with_q else [None] + list(res)


def _na_bias_table(rpb, rows):
    wr, wc, w = NA_WIN_ROWS, NA_WIN_COLS, GRID_W
    nh = rpb.shape[0]
    nsteps = rows // NA_QROWS
    col = np.arange(w)
    col_start = np.clip(col - wc // 2, 0, w - wc)
    col_ok = (col[None, :] >= col_start[:, None]) & (col[None, :] < col_start[:, None] + wc)
    cpad = w - wc
    p = jnp.pad(rpb.astype(F32), ((0, 0), (0, 0), (cpad, cpad)))
    e = jnp.stack([p[:, :, w - 1 - c:2 * w - 1 - c] for c in range(w)], axis=2)
    e = jnp.where(col_ok[None, None], e, NEG_BIG)
    rpad = NA_KROWS
    e = jnp.pad(e, ((0, 0), (rpad, rpad), (0, 0), (0, 0)))
    blocks, oks = [], []
    for j in (0, 1, nsteps - 1):
        base = int(np.clip(j * NA_QROWS - wr // 2, 0, rows - NA_KROWS))
        for a in range(NA_QROWS):
            r = j * NA_QROWS + a
            r0 = int(np.clip(r - wr // 2, 0, rows - wr))
            krow = base + np.arange(NA_KROWS)
            oks.append((krow >= r0) & (krow < r0 + wr))
            start = base - r + (wr - 1) + rpad
            blocks.append(e[:, start:start + NA_KROWS])
    t = jnp.stack(blocks, axis=1).reshape(nh, 3, NA_QROWS, NA_KROWS, w, w)
    ok = np.stack(oks).reshape(1, 3, NA_QROWS, NA_KROWS, 1, 1)
    t = jnp.where(ok, t, NEG_BIG)
    return t.transpose(0, 1, 2, 4, 3, 5).reshape(nh, 3, NA_QROWS * w, NA_KROWS * w)


def _na_attn_body(q_ref, k_ref, v_ref, kc_ref, vc_ref, bias_ref, o_ref, *, rows, steps_per_block):
    nq = NA_QROWS * GRID_W
    nk = NA_KROWS * GRID_W
    nsteps = rows // NA_QROWS
    rb = pl.program_id(2)
    left = lax.broadcasted_iota(jnp.int32, (nq, LANES), 1) < NA_HEAD_DIM
    head_masks = (left.astype(F32).astype(BF16), jnp.logical_not(left).astype(F32).astype(BF16))
    kc, vc = kc_ref[...], vc_ref[...]

    def step(it, carry):
        j = rb * steps_per_block + it
        pat = jnp.where(j == 0, 0, jnp.where(j == nsteps - 1, 2, 1))
        base = jnp.clip(j * NA_QROWS - NA_WIN_ROWS // 2, 0, rows - NA_KROWS)
        koff = pl.multiple_of(base * GRID_W, GRID_W)
        qoff = pl.multiple_of(it * nq, nq)
        q = q_ref[pl.ds(qoff, nq), :]
        kw = k_ref[pl.ds(koff, nk), :]
        vw = v_ref[pl.ds(koff, nk), :]
        outs = []
        for hh in range(2):
            qm = q * head_masks[hh]
            s_loc = _dot_nt(qm, kw) + bias_ref[hh, pat]
            s_ctx = _dot_nt(qm, kc)
            m = jnp.maximum(jnp.max(s_loc, axis=-1, keepdims=True), jnp.max(s_ctx, axis=-1, keepdims=True))
            p_loc = jnp.exp(s_loc - m)
            p_ctx = jnp.exp(s_ctx - m)
            denom = jnp.sum(p_loc, axis=-1, keepdims=True) + jnp.sum(p_ctx, axis=-1, keepdims=True)
            o = _dot(p_loc.astype(BF16), vw) + _dot(p_ctx.astype(BF16), vc)
            outs.append(o * (1.0 / denom))
        o_ref[pl.ds(qoff, nq), :] = jnp.where(left, outs[0], outs[1]).astype(o_ref.dtype)
        return carry

    lax.fori_loop(0, steps_per_block, step, 0)


def _na_attention(q, k, v, bias, lc):
    b, s, nh = q.shape
    rows = s // GRID_W
    nsteps = rows // NA_QROWS
    steps_per_block = 8 if nsteps % 8 == 0 else nsteps
    tq = steps_per_block * NA_QROWS * GRID_W
    nq, nk = bias.shape[2], bias.shape[3]
    return pl.pallas_call(
        functools.partial(_na_attn_body, rows=rows, steps_per_block=steps_per_block),
        grid=(nh // LANES, b, s // tq),
        in_specs=[pl.BlockSpec((None, tq, LANES), lambda hp, bb, i: (bb, i, hp)),
                  pl.BlockSpec((None, s, LANES), lambda hp, bb, i: (bb, 0, hp)),
                  pl.BlockSpec((None, s, LANES), lambda hp, bb, i: (bb, 0, hp)),
                  pl.BlockSpec((None, lc, LANES), lambda hp, bb, i: (bb, s // lc, hp)),
                  pl.BlockSpec((None, lc, LANES), lambda hp, bb, i: (bb, s // lc, hp)),
                  pl.BlockSpec((2, 3, nq, nk), lambda hp, bb, i: (hp, 0, 0, 0))],
        out_specs=pl.BlockSpec((None, tq, LANES), lambda hp, bb, i: (bb, i, hp)),
        out_shape=jax.ShapeDtypeStruct((b, s, nh), BF16),
        compiler_params=_compiler_params(("parallel", "parallel", "parallel")),
        name="na_attention",
    )(q, k, v, k, v, bias)


def _ret_proj_body(*refs, rope, with_qg, d):
    it = iter(refs)
    h_ref = next(it)
    cos_ref = sin_ref = None
    if rope:
        cos_ref, sin_ref = next(it), next(it)
    mod_ref, n1_ref, w_ref = (next(it) for _ in range(3))
    if with_qg:
        q_ref = next(it)
    k_ref, v_ref = next(it), next(it)
    if with_qg:
        g_ref = next(it)

    nk, nv, dk = RET_HEADS * RET_QK_DIM, RET_HEADS * RET_V_DIM, RET_QK_DIM
    mod = mod_ref[...]
    x = _norm_mod(h_ref[...], n1_ref[...], mod[:, d:2 * d], mod[:, 0:d]).astype(BF16)
    if rope:
        cos, sin = cos_ref[...], sin_ref[...]

    def rotate_store(t, ref, scale):
        for hh in range(RET_HEADS):
            x1 = t[:, hh * dk:hh * dk + dk // 2] * scale
            x2 = t[:, hh * dk + dk // 2:(hh + 1) * dk] * scale
            if rope:
                x1, x2 = x1 * cos - x2 * sin, x1 * sin + x2 * cos
            ref[:, hh * dk:hh * dk + dk // 2] = x1.astype(BF16)
            ref[:, hh * dk + dk // 2:(hh + 1) * dk] = x2.astype(BF16)

    rotate_store(_dot(x, w_ref[:, nk:2 * nk]), k_ref, dk ** -0.5)
    v_ref[...] = _dot(x, w_ref[:, 2 * nk:2 * nk + nv]).astype(BF16)
    if with_qg:
        rotate_store(_dot(x, w_ref[:, :nk]), q_ref, 1.0)
        g_ref[...] = _dot(x, w_ref[:, 2 * nk + nv:])


def _ret_project(h, mod, n1, w_qkvg, rope_tabs, with_qg, kv_dest):
    b, l, d = h.shape
    tm = _tile(l, 512)
    nk, nv = RET_HEADS * RET_QK_DIM, RET_HEADS * RET_V_DIM
    rope = rope_tabs is not None
    outs = ([(nk, BF16)] if with_qg else []) + _kv_outs((nk, nv), kv_dest) + ([(nv, F32)] if with_qg else [])
    res = _row_call(functools.partial(_ret_proj_body, rope=rope, with_qg=with_qg, d=d), name="ret_proj", batch=b,
                    length=l, tm=tm, rows=(h,), pos=tuple(rope_tabs) if rope else (), vecs=(mod,),
                    consts=(n1, w_qkvg), outs=outs)
    return res if with_qg else [None, res[0], res[1], None]


def _ret_scan_body(*refs, emit_y, chunk, nchunks):
    if emit_y:
        ld_ref, q_ref, k_ref, v_ref, s0_ref, y_ref, sfin_ref, st_ref = refs
    else:
        ld_ref, k_ref, v_ref, s0_ref, sfin_ref, st_ref = refs
        q_ref = y_ref = None
    direction, hh, t = pl.program_id(0), pl.program_id(2), pl.program_id(3)
    ld = ld_ref[direction, hh]

    @pl.when(t == 0)
    def _():
        st_ref[...] = s0_ref[...]

    ri = lax.broadcasted_iota(jnp.int32, (chunk, chunk), 0)
    ci = lax.broadcasted_iota(jnp.int32, (chunk, chunk), 1)
    dist = jnp.where(direction == 0, ri - ci, ci - ri).astype(F32)
    intra = jnp.where(dist >= 0, jnp.exp(jnp.maximum(dist, 0.0) * ld), 0.0)
    pos = lax.broadcasted_iota(jnp.int32, (chunk, 1), 0)
    rank = jnp.where(direction == 0, pos, chunk - 1 - pos).astype(F32)
    q_decay = jnp.exp((rank + 1.0) * ld)
    k_decay = jnp.exp((chunk - 1.0 - rank) * ld)
    chunk_decay = jnp.exp(jnp.full((1, 1), float(chunk), F32) * ld)

    for c in range(nchunks):
        cc = jnp.where(direction == 0, c, nchunks - 1 - c)
        off = pl.multiple_of(cc * chunk, chunk)
        k = k_ref[pl.ds(off, chunk), :]
        v = v_ref[pl.ds(off, chunk), :]
        state = st_ref[...]
        if emit_y:
            q = q_ref[pl.ds(off, chunk), :]
            sc = _dot_nt(q, k) * intra
            inner = _dot(sc.astype(BF16), v)
            cross = _dot(q, state.astype(BF16)) * q_decay
            y_ref[pl.ds(off, chunk), :] = inner + cross
        kd = (k.astype(F32) * k_decay).astype(BF16)
        st_ref[...] = state * chunk_decay + _dot_tn(kd, v)

    @pl.when(t == pl.num_programs(3) - 1)
    def _():
        sfin_ref[...] = st_ref[...]


def _ret_scan(log_decay, q, k, v, init_state, emit_y, row0, l):
    b = k.shape[0]
    dk, dv, nh = RET_QK_DIM, RET_V_DIM, RET_HEADS
    chunk = 256 if l % 256 == 0 else 128
    tt = 1024 if l % 1024 == 0 else chunk
    nchunks = tt // chunk
    nt = l // tt
    assert row0 % tt == 0

    def tok(d, t):
        return jnp.where(d == 0, t, nt - 1 - t)

    in_specs = [pl.BlockSpec(memory_space=pltpu.SMEM)]
    args = [log_decay]
    if emit_y:
        in_specs.append(pl.BlockSpec((None, tt, dk), lambda d, bb, h, t: (bb, tok(d, t), h)))
        args.append(q)
    in_specs += [pl.BlockSpec((None, tt, dk), lambda d, bb, h, t: (bb, row0 // tt + tok(d, t), h)),
                 pl.BlockSpec((None, tt, dv), lambda d, bb, h, t: (bb, row0 // tt + tok(d, t), h)),
                 pl.BlockSpec((None, None, None, dk, dv), lambda d, bb, h, t: (d, bb, h, 0, 0))]
    args += [k, v, init_state]
    out_shape, out_specs = [], []
    if emit_y:
        out_shape.append(jax.ShapeDtypeStruct((b, l, 2 * nh * dv), F32))
        out_specs.append(pl.BlockSpec((None, tt, dv), lambda d, bb, h, t: (bb, tok(d, t), d * nh + h)))
    out_shape.append(jax.ShapeDtypeStruct((2, b, nh, dk, dv), F32))
    out_specs.append(pl.BlockSpec((None, None, None, dk, dv), lambda d, bb, h, t: (d, bb, h, 0, 0)))
    res = pl.pallas_call(
        functools.partial(_ret_scan_body, emit_y=emit_y, chunk=chunk, nchunks=nchunks),
        grid=(2, b, nh, nt),
        in_specs=in_specs,
        out_specs=out_specs,
        out_shape=out_shape,
        scratch_shapes=[pltpu.VMEM((dk, dv), F32)],
        compiler_params=_compiler_params(("parallel", "parallel", "parallel", "arbitrary")),
        name="ret_scan" if emit_y else "ret_ctx_state",
    )(*args)
    return (res[0], res[1]) if emit_y else (None, res[0])


def _ret_out_body(h_ref, y_ref, g_ref, mod_ref, gn_ref, w_ref, out_ref, *, d):
    nv, dv = RET_HEADS * RET_V_DIM, RET_V_DIM
    y2 = y_ref[...]
    y = y2[:, :nv] + y2[:, nv:]
    gate = _silu(g_ref[...])
    gn = gn_ref[...]
    acc = None
    for hh in range(RET_HEADS):
        yh = y[:, hh * dv:(hh + 1) * dv]
        mu = jnp.mean(yh, axis=-1, keepdims=True)
        yc = yh - mu
        var = jnp.mean(yc * yc, axis=-1, keepdims=True)
        z = yc * lax.rsqrt(var + EPS) * gn[:, hh * dv:(hh + 1) * dv] * gate[:, hh * dv:(hh + 1) * dv]
        part = _dot(z.astype(BF16), w_ref[hh * dv:(hh + 1) * dv, :])
        acc = part if acc is None else acc + part
    out_ref[...] = h_ref[...] + mod_ref[...][:, 2 * d:3 * d] * acc


def _ret_output(h, y, g, mod, out_norm, w_o):
    b, l, d = h.shape
    tm = _tile(l, 256)
    return _row_call(functools.partial(_ret_out_body, d=d), name="ret_out", batch=b, length=l, tm=tm,
                     rows=(h, y, g), vecs=(mod,), consts=(out_norm.reshape(1, -1), w_o), outs=[(d, F32)])[0]


def kernel(x, c, ctx, c_ctx, mod_w, mod_b, norm1, norm2, ffn_w13, ffn_w2, mla_w_down, mla_q_lora_norm, mla_kv_lora_norm, mla_w_uq, mla_w_ukv, mla_q_norm, mla_k_norm, mla_w_o, gqa_w_qkv, gqa_q_norm, gqa_k_norm, gqa_w_o, na_w_qkv, na_q_norm, na_k_norm, na_rpb, na_w_o, ret_w_qkvg, ret_log_decay_fwd, ret_log_decay_bwd, ret_out_norm, ret_w_o):
    b, s, d = x.shape
    lc = ctx.shape[1]
    depth = mod_w.shape[0]

    pad_rows = -(b + 1) % 16
    c_rows = jnp.concatenate([c, c_ctx[None, :], jnp.zeros((pad_rows, d), F32)], axis=0)
    mods = _modulation(c_rows, mod_w, mod_b)

    h, hc = x, ctx
    for i in range(depth):
        need_ctx = i < depth - 1
        mod_l = mods[i, :b].reshape(b, 1, 6 * d)
        mod_c = mods[i, b:b + 1].reshape(1, 1, 6 * d)
        n1 = norm1[i].reshape(1, d)
        n2 = norm2[i].reshape(1, d)
        kind, j = i % 4, i // 4
        if kind == 0:
            prep = _mla_prepare(mla_w_down[j], mla_q_lora_norm[j], mla_kv_lora_norm[j], mla_w_uq[j], mla_w_ukv[j],
                                mla_q_norm[j], mla_k_norm[j])
            tabs = _rope_tables_rolled(s, MLA_ROPE, LANES // 2)
            ql, k_all, v_all = _mla_project(h, mod_l, n1, prep, tabs, True, (s + lc, 0, None))
            qc, k_all, v_all = _mla_project(hc, mod_c, n1, prep, None, need_ctx, (s + lc, s, (k_all, v_all)))
            attn = functools.partial(_flash_attention, kv_heads=MLA_HEADS, group=1, dqk=MLA_QK_PAD, dv=MLA_V, tq=512)
            w_o = mla_w_o[j].astype(BF16)
        elif kind == 1:
            w_qkv = gqa_w_qkv[j].astype(BF16)
            tabs = _rope_tables_rolled(s, GQA_HEAD_DIM, GQA_HEAD_DIM // 2)
            ql, k_all, v_all = _gqa_project(h, mod_l, n1, w_qkv, gqa_q_norm[j], gqa_k_norm[j], tabs, True,
                                            (s + lc, 0, None))
            qc, k_all, v_all = _gqa_project(hc, mod_c, n1, w_qkv, gqa_q_norm[j], gqa_k_norm[j], None, need_ctx,
                                            (s + lc, s, (k_all, v_all)))
            group = GQA_HEADS // GQA_KV_HEADS
            attn = functools.partial(_flash_attention, kv_heads=GQA_KV_HEADS, group=group, dqk=GQA_HEAD_DIM,
                                     dv=GQA_HEAD_DIM, tq=512 // group)
            w_o = gqa_w_o[j].astype(BF16)
        elif kind == 2:
            w_qkv = na_w_qkv[j].astype(BF16)
            ql, k_all, v_all = _na_project(h, mod_l, n1, w_qkv, na_q_norm[j], na_k_norm[j], True, (s + lc, 0, None))
            qc, k_all, v_all = _na_project(hc, mod_c, n1, w_qkv, na_q_norm[j], na_k_norm[j], need_ctx,
                                           (s + lc, s, (k_all, v_all)))
            w_o = na_w_o[j].astype(BF16)
        else:
            w_qkvg = ret_w_qkvg[j].astype(BF16)
            tabs = _rope_cos_sin(s, RET_QK_DIM)
            ql, k_all, v_all, gl = _ret_project(h, mod_l, n1, w_qkvg, tabs, True, (s + lc, 0, None))
            qc, k_all, v_all, gc = _ret_project(hc, mod_c, n1, w_qkvg, None, need_ctx, (s + lc, s, (k_all, v_all)))
            w_o = ret_w_o[j].astype(BF16)

        if kind in (0, 1):
            ol = attn(ql, k_all, v_all, kv_len=s + lc, kv_block=0)
            h = _out_project(h, ol, mod_l, w_o)
            if need_ctx:
                oc = attn(qc, k_all, v_all, kv_len=lc, kv_block=s // lc)
                hc = _out_project(hc, oc, mod_c, w_o)
        elif kind == 2:
            bias = _na_bias_table(na_rpb[j], s // GRID_W)
            ol = _na_attention(ql, k_all, v_all, bias, lc)
            h = _out_project(h, ol, mod_l, w_o)
            if need_ctx:
                def heads_major(t):
                    return t.reshape(b, lc, NA_HEADS, NA_HEAD_DIM).transpose(0, 2, 1, 3).reshape(
                        b * NA_HEADS, lc, NA_HEAD_DIM)
                oc = _flash_attention(heads_major(qc), heads_major(k_all[:, s:]), heads_major(v_all[:, s:]),
                                      kv_len=lc, kv_block=0, kv_heads=1, group=1, dqk=NA_HEAD_DIM, dv=NA_HEAD_DIM,
                                      tq=lc)
                oc = oc.reshape(b, NA_HEADS, lc, NA_HEAD_DIM).transpose(0, 2, 1, 3).reshape(b, lc, -1)
                hc = _out_project(hc, oc, mod_c, w_o)
        else:
            log_decay = jnp.stack([ret_log_decay_fwd[j], ret_log_decay_bwd[j]]).astype(F32)
            zero_state = jnp.zeros((2, b, RET_HEADS, RET_QK_DIM, RET_V_DIM), F32)
            _, ctx_state = _ret_scan(log_decay, None, k_all, v_all, zero_state, False, s, lc)
            y, _ = _ret_scan(log_decay, ql, k_all, v_all, ctx_state, True, 0, s)
            h = _ret_output(h, y, gl, mod_l, ret_out_norm[j], w_o)
            if need_ctx:
                raise NotImplementedError("context update after a retention layer is not needed at this depth")

        h = _ffn(h, mod_l, n2, ffn_w13[i].astype(BF16), ffn_w2[i].astype(BF16))
        if need_ctx:
            hc = _ffn(hc, mod_c, n2, ffn_w13[i].astype(BF16), ffn_w2[i].astype(BF16))
    return h
```

```python
import functools

import numpy as np
import jax
import jax.numpy as jnp
from jax import lax
from jax.experimental import pallas as pl
from jax.experimental.pallas import tpu as pltpu

F32 = jnp.float32
BF16 = jnp.bfloat16

GRID_W = 64
ROPE_THETA = 10000.0
EPS = 1e-6
MLA_HEADS, MLA_Q_RANK, MLA_KV_RANK, MLA_NOPE, MLA_ROPE, MLA_V = 8, 384, 256, 128, 64, 128
MLA_QK = MLA_NOPE + MLA_ROPE
MLA_QK_PAD = 256
GQA_HEADS, GQA_KV_HEADS, GQA_HEAD_DIM = 8, 2, 128
NA_HEADS, NA_HEAD_DIM, NA_WIN_ROWS, NA_WIN_COLS = 16, 64, 8, 16
RET_HEADS, RET_QK_DIM, RET_V_DIM = 4, 256, 512

LANES = 128
BF16_SUBLANES = 16
LOG2E = 1.4426950408889634
V7X_VMEM_LIMIT_BYTES = 56 * 1024 * 1024
NEG_BIG = -1e30

FLASH_BUFFERS = 3
FLASH_UNROLL_CHUNKS = 12
NA_QROWS = 4
NA_KROWS = NA_QROWS + NA_WIN_ROWS


def _dot(a, b):
    return jnp.dot(a, b, preferred_element_type=F32)


def _dot_nt(a, b):
    return lax.dot_general(a, b, (((1,), (1,)), ((), ())), preferred_element_type=F32)


def _dot_tn(a, b):
    return lax.dot_general(a, b, (((0,), (0,)), ((), ())), preferred_element_type=F32)


def _silu(x):
    return x / (1.0 + jnp.exp(-x))


def _norm_mod(h, gain, scale, shift):
    y = h * lax.rsqrt(jnp.mean(h * h, axis=-1, keepdims=True) + EPS) * gain
    return y * (1.0 + scale) + shift


def _rms(x, denom):
    return x * lax.rsqrt(jnp.sum(x * x, axis=-1, keepdims=True) / denom + EPS)


def _compiler_params(semantics):
    return pltpu.CompilerParams(dimension_semantics=semantics, vmem_limit_bytes=V7X_VMEM_LIMIT_BYTES)


def _row_call(body, *, name, batch, length, tm, rows=(), pos=(), vecs=(), consts=(), outs=()):
    outs = [o if len(o) == 5 else (o[0], o[1], length, 0, None) for o in outs]
    aliased = [(k, o[4]) for k, o in enumerate(outs) if o[4] is not None]
    n_in = len(rows) + len(pos) + len(vecs) + len(consts)
    if aliased:
        inner = body

        def body(*refs):
            inner(*refs[:n_in], *refs[n_in + len(aliased):])
    in_specs = []
    for a in rows:
        in_specs.append(pl.BlockSpec((None, tm, a.shape[-1]), lambda b, i: (b, i, 0)))
    for a in pos:
        in_specs.append(pl.BlockSpec((tm, a.shape[-1]), lambda b, i: (i, 0)))
    for a in vecs:
        if a.shape[0] == 1:
            in_specs.append(pl.BlockSpec((None, 1, a.shape[-1]), lambda b, i: (0, 0, 0)))
        else:
            in_specs.append(pl.BlockSpec((None, 1, a.shape[-1]), lambda b, i: (b, 0, 0)))
    for a in consts:
        nd = a.ndim
        in_specs.append(pl.BlockSpec(a.shape, lambda b, i, nd=nd: (0,) * nd, pipeline_mode=pl.Buffered(1)))
    in_specs += [pl.BlockSpec(memory_space=pl.ANY) for _ in aliased]
    out_shape = [jax.ShapeDtypeStruct((batch, total, f), dt) for f, dt, total, _, _ in outs]
    out_specs = [pl.BlockSpec((None, tm, f), lambda b, i, off=off // tm: (b, i + off, 0)) for f, _, _, off, _ in outs]
    assert all(off % tm == 0 for _, _, _, off, _ in outs)
    res = pl.pallas_call(
        body,
        grid=(batch, length // tm),
        in_specs=in_specs,
        out_specs=out_specs,
        out_shape=out_shape,
        input_output_aliases={n_in + a: k for a, (k, _) in enumerate(aliased)},
        compiler_params=_compiler_params(("parallel", "parallel")),
        name=name,
    )(*rows, *pos, *vecs, *consts, *[buf for _, buf in aliased])
    return res


def _tile(length, pref):
    return pref if length % pref == 0 else length


def _mod_body(c_ref, w_ref, b_ref, o_ref):
    s = _silu(c_ref[...]).astype(BF16)
    o_ref[...] = _dot(s, w_ref[...].astype(BF16)) + b_ref[...]


def _modulation(c_rows, mod_w, mod_b):
    depth, d, n = mod_w.shape
    tn = 1536 if n % 1536 == 0 else n
    return pl.pallas_call(
        _mod_body,
        grid=(depth, n // tn),
        in_specs=[pl.BlockSpec(c_rows.shape, lambda l, j: (0, 0)),
                  pl.BlockSpec((None, d, tn), lambda l, j: (l, 0, j)),
                  pl.BlockSpec((None, 1, tn), lambda l, j: (l, 0, j))],
        out_specs=pl.BlockSpec((None, c_rows.shape[0], tn), lambda l, j: (l, 0, j)),
        out_shape=jax.ShapeDtypeStruct((depth, c_rows.shape[0], n), F32),
        compiler_params=_compiler_params(("parallel", "parallel")),
        name="modulation",
    )(c_rows, mod_w, mod_b.reshape(depth, 1, n))


def _rope_cos_sin(length, dim):
    t = jnp.arange(length)
    row = (t // GRID_W).astype(F32)
    col = (t % GRID_W).astype(F32)
    quarter = dim // 4
    inv_freq = ROPE_THETA ** (-jnp.arange(quarter, dtype=F32) / quarter)
    ang = jnp.concatenate([row[:, None] * inv_freq, col[:, None] * inv_freq], axis=-1)
    return jnp.cos(ang), jnp.sin(ang)


def _rope_tables_rolled(length, dim, lane_half):
    cos, sin = _rope_cos_sin(length, dim)
    pad = jnp.zeros((length, lane_half - dim // 2), F32)
    cos_t = jnp.concatenate([cos, pad, cos, pad], axis=-1)
    sin_t = jnp.concatenate([-sin, pad, sin, pad], axis=-1)
    return cos_t, sin_t


def _rope_rolled(x, cos_t, sin_t):
    return x * cos_t + pltpu.roll(x, x.shape[-1] // 2, 1) * sin_t


def _mla_proj_body(*refs, rope, with_q, d):
    it = iter(refs)
    h_ref = next(it)
    cos_ref = sin_ref = None
    if rope:
        cos_ref, sin_ref = next(it), next(it)
    mod_ref, n1_ref, wd_ref, gql_ref, gkl_ref, wuq_ref, wuk_ref, wuv_ref, gq_ref, gk_ref = (next(it) for _ in range(10))
    if with_q:
        q_ref = next(it)
    k_ref, v_ref = next(it), next(it)

    mod = mod_ref[...]
    x = _norm_mod(h_ref[...], n1_ref[...], mod[:, d:2 * d], mod[:, 0:d]).astype(BF16)
    dl = _dot(x, wd_ref[...])
    ckv = (_rms(dl[:, MLA_Q_RANK:MLA_Q_RANK + MLA_KV_RANK], MLA_KV_RANK) * gkl_ref[...]).astype(BF16)
    kr = dl[:, MLA_Q_RANK + MLA_KV_RANK:]
    kn = _dot(ckv, wuk_ref[...])
    v_ref[...] = _dot(ckv, wuv_ref[...]).astype(BF16)
    if rope:
        cos_t, sin_t = cos_ref[...], sin_ref[...]
    gk = gk_ref[...]
    kr_ss = jnp.sum(kr * kr, axis=-1, keepdims=True)
    for hh in range(MLA_HEADS):
        kh = kn[:, hh * MLA_NOPE:(hh + 1) * MLA_NOPE]
        r = lax.rsqrt((jnp.sum(kh * kh, axis=-1, keepdims=True) + kr_ss) / MLA_QK + EPS)
        k_ref[:, hh * MLA_QK_PAD:hh * MLA_QK_PAD + MLA_NOPE] = (kh * r * gk[:, :MLA_NOPE]).astype(BF16)
        krh = kr * r * gk[:, MLA_NOPE:]
        if rope:
            krh = _rope_rolled(krh, cos_t, sin_t)
        k_ref[:, hh * MLA_QK_PAD + MLA_NOPE:(hh + 1) * MLA_QK_PAD] = krh.astype(BF16)
    if with_q:
        cq = (_rms(dl[:, :MLA_Q_RANK], MLA_Q_RANK) * gql_ref[...]).astype(BF16)
        q = _dot(cq, wuq_ref[...])
        gq = gq_ref[...]
        scale = MLA_QK ** -0.5 * LOG2E
        for hh in range(MLA_HEADS):
            qn = q[:, hh * MLA_QK_PAD:hh * MLA_QK_PAD + MLA_NOPE]
            qr = q[:, hh * MLA_QK_PAD + MLA_NOPE:(hh + 1) * MLA_QK_PAD]
            ss = jnp.sum(qn * qn, axis=-1, keepdims=True) + jnp.sum(qr * qr, axis=-1, keepdims=True)
            r = lax.rsqrt(ss / MLA_QK + EPS) * scale
            q_ref[:, hh * MLA_QK_PAD:hh * MLA_QK_PAD + MLA_NOPE] = (qn * r * gq[:, :MLA_NOPE]).astype(BF16)
            qr = qr * r * gq[:, MLA_NOPE:]
            if rope:
                qr = _rope_rolled(qr, cos_t, sin_t)
            q_ref[:, hh * MLA_QK_PAD + MLA_NOPE:(hh + 1) * MLA_QK_PAD] = qr.astype(BF16)


def _pad_rope_cols(r):
    half = MLA_ROPE // 2
    z = jnp.zeros(r.shape[:-1] + (LANES // 2 - half,), r.dtype)
    return jnp.concatenate([r[..., :half], z, r[..., half:], z], axis=-1)


def _mla_prepare(w_down, q_lora_norm, kv_lora_norm, w_uq, w_ukv, q_norm, k_norm):
    d = w_down.shape[0]
    nq = MLA_Q_RANK + MLA_KV_RANK
    wd = jnp.concatenate([w_down[:, :nq], _pad_rope_cols(w_down[:, nq:])], axis=1).astype(BF16)
    wq = w_uq.reshape(MLA_Q_RANK, MLA_HEADS, MLA_QK)
    wq = jnp.concatenate([wq[..., :MLA_NOPE], _pad_rope_cols(wq[..., MLA_NOPE:])], axis=-1)
    wq = wq.reshape(MLA_Q_RANK, MLA_HEADS * MLA_QK_PAD).astype(BF16)
    wkv = w_ukv.reshape(MLA_KV_RANK, MLA_HEADS, MLA_NOPE + MLA_V)
    wuk = wkv[..., :MLA_NOPE].reshape(MLA_KV_RANK, MLA_HEADS * MLA_NOPE).astype(BF16)
    wuv = wkv[..., MLA_NOPE:].reshape(MLA_KV_RANK, MLA_HEADS * MLA_V).astype(BF16)

    def pad_gain(g):
        return jnp.concatenate([g[:MLA_NOPE], _pad_rope_cols(g[MLA_NOPE:])]).reshape(1, MLA_QK_PAD)

    return (wd, q_lora_norm.reshape(1, -1), kv_lora_norm.reshape(1, -1), wq, wuk, wuv,
            pad_gain(q_norm), pad_gain(k_norm))


def _kv_outs(widths, kv_dest):
    total, off, bufs = kv_dest
    return [(f, BF16, total, off, None if bufs is None else bufs[k]) for k, f in enumerate(widths)]


def _mla_project(h, mod, n1, prep, rope_tabs, with_q, kv_dest):
    b, l, d = h.shape
    tm = _tile(l, 512)
    rope = rope_tabs is not None
    outs = []
    if with_q:
        outs.append((MLA_HEADS * MLA_QK_PAD, BF16))
    outs += _kv_outs((MLA_HEADS * MLA_QK_PAD, MLA_HEADS * MLA_V), kv_dest)
    res = _row_call(functools.partial(_mla_proj_body, rope=rope, with_q=with_q, d=d),
                    name="mla_proj", batch=b, length=l, tm=tm, rows=(h,),
                    pos=tuple(rope_tabs) if rope else (), vecs=(mod,), consts=(n1,) + tuple(prep), outs=outs)
    return res if with_q else [None] + list(res)


def _gqa_proj_body(*refs, rope, with_q, d):
    it = iter(refs)
    h_ref = next(it)
    cos_ref = sin_ref = None
    if rope:
        cos_ref, sin_ref = next(it), next(it)
    mod_ref, n1_ref, w_ref, gq_ref, gk_ref = (next(it) for _ in range(5))
    if with_q:
        q_ref = next(it)
    k_ref, v_ref = next(it), next(it)

    dh = GQA_HEAD_DIM
    nq = GQA_HEADS * dh
    nkv = GQA_KV_HEADS * dh
    mod = mod_ref[...]
    x = _norm_mod(h_ref[...], n1_ref[...], mod[:, d:2 * d], mod[:, 0:d]).astype(BF16)
    if rope:
        cos_t, sin_t = cos_ref[...], sin_ref[...]
    kv = _dot(x, w_ref[:, nq:])
    v_ref[...] = kv[:, nkv:].astype(BF16)
    for hh in range(GQA_KV_HEADS):
        kh = _rms(kv[:, hh * dh:(hh + 1) * dh], dh) * gk_ref[...]
        if rope:
            kh = _rope_rolled(kh, cos_t, sin_t)
        k_ref[:, hh * dh:(hh + 1) * dh] = kh.astype(BF16)
    if with_q:
        q = _dot(x, w_ref[:, :nq])
        gq = gq_ref[...] * (dh ** -0.5 * LOG2E)
        for hh in range(GQA_HEADS):
            qh = _rms(q[:, hh * dh:(hh + 1) * dh], dh) * gq
            if rope:
                qh = _rope_rolled(qh, cos_t, sin_t)
            q_ref[:, hh * dh:(hh + 1) * dh] = qh.astype(BF16)


def _gqa_project(h, mod, n1, w_qkv, q_norm, k_norm, rope_tabs, with_q, kv_dest):
    b, l, d = h.shape
    tm = _tile(l, 512)
    rope = rope_tabs is not None
    outs = []
    if with_q:
        outs.append((GQA_HEADS * GQA_HEAD_DIM, BF16))
    outs += _kv_outs((GQA_KV_HEADS * GQA_HEAD_DIM,) * 2, kv_dest)
    res = _row_call(functools.partial(_gqa_proj_body, rope=rope, with_q=with_q, d=d),
                    name="gqa_proj", batch=b, length=l, tm=tm, rows=(h,),
                    pos=tuple(rope_tabs) if rope else (), vecs=(mod,),
                    consts=(n1, w_qkv, q_norm.reshape(1, -1), k_norm.reshape(1, -1)), outs=outs)
    return res if with_q else [None] + list(res)


def _flash_body(q_ref, k_ref, vt_ref, o_ref, qs_ref, *scratch, group, dqk, dv, tq, tk, n, nbuf):
    s_bufs, c_bufs, p_bufs, a_bufs = (scratch[k * nbuf:(k + 1) * nbuf] for k in range(4))
    m_ref, acc_ref = scratch[4 * nbuf:]
    for g in range(group):
        qs_ref[g * tq:(g + 1) * tq, :] = q_ref[:, g * dqk:(g + 1) * dqk]
    m_ref[...] = jnp.full(m_ref.shape, NEG_BIG, F32)
    acc_ref[...] = jnp.zeros(acc_ref.shape, F32)

    def scores(j, slot):
        rows = pl.ds(pl.multiple_of(j * tk, tk), tk)
        s = _dot_nt(k_ref[rows, :], qs_ref[...])
        s_bufs[slot][...] = s
        c_bufs[slot][...] = jnp.max(s, axis=0, keepdims=True)

    def softmax(slot):
        m_prev = m_ref[...]
        m_new = jnp.maximum(m_prev, c_bufs[slot][...])
        a_bufs[slot][...] = jnp.exp2(m_prev - m_new)
        m_ref[...] = m_new
        p_bufs[slot][...] = jnp.exp2(s_bufs[slot][...] - m_new).astype(BF16)

    def weighted_values(j, slot):
        acc_ref[...] = a_bufs[slot][...] * acc_ref[...] + _dot(vt_ref[j], p_bufs[slot][...])

    def steady(j, r):
        scores(j + 1, (r + 1) % nbuf)
        softmax(r)
        weighted_values(j - 1, (r - 1) % nbuf)

    scores(0, 0)
    if n > 1:
        scores(1, 1 % nbuf)
        softmax(0)
        n_rounds = 0 if n <= FLASH_UNROLL_CHUNKS else (n - 2) // nbuf

        def one_round(t, carry):
            for u in range(nbuf):
                steady(1 + nbuf * t + u, (1 + u) % nbuf)
            return carry

        if n_rounds > 0:
            lax.fori_loop(0, n_rounds, one_round, 0)
        for j in range(1 + nbuf * n_rounds, n - 1):
            steady(j, j % nbuf)
        softmax((n - 1) % nbuf)
        weighted_values(n - 2, (n - 2) % nbuf)
    else:
        softmax(0)
    weighted_values(n - 1, (n - 1) % nbuf)

    out = (acc_ref[0:dv, :] * (1.0 / acc_ref[dv:dv + 1, :])).T
    for g in range(group):
        o_ref[:, g * dv:(g + 1) * dv] = out[g * tq:(g + 1) * tq, :].astype(o_ref.dtype)


def _key_chunk(length):
    for tk in (768, 512, 256, 128):
        if length % tk == 0:
            return tk
    return length


def _flash_attention(q, k, v, *, kv_len, kv_block, kv_heads, group, dqk, dv, tq):
    b, lq, _ = q.shape
    tq = _tile(lq, tq)
    tk = _key_chunk(kv_len)
    n = kv_len // tk
    rows = group * tq
    vt = v[:, kv_block * kv_len:(kv_block + 1) * kv_len].reshape(b, n, tk, kv_heads, dv).transpose(0, 3, 1, 4, 2)
    extra = jnp.zeros((b, kv_heads, n, BF16_SUBLANES, tk), BF16).at[:, :, :, 0, :].set(1.0)
    vt = jnp.concatenate([vt, extra], axis=3)
    dve = dv + BF16_SUBLANES
    nbuf = min(FLASH_BUFFERS, n)
    stage_bufs = ([pltpu.VMEM((tk, rows), F32)] * nbuf
                  + [pltpu.VMEM((1, rows), F32)] * nbuf
                  + [pltpu.VMEM((tk, rows), BF16)] * nbuf
                  + [pltpu.VMEM((1, rows), F32)] * nbuf)
    return pl.pallas_call(
        functools.partial(_flash_body, group=group, dqk=dqk, dv=dv, tq=tq, tk=tk, n=n, nbuf=nbuf),
        grid=(b, kv_heads, lq // tq),
        in_specs=[pl.BlockSpec((None, tq, group * dqk), lambda bb, h, i: (bb, i, h)),
                  pl.BlockSpec((None, kv_len, dqk), lambda bb, h, i: (bb, kv_block, h)),
                  pl.BlockSpec((None, None, n, dve, tk), lambda bb, h, i: (bb, h, 0, 0, 0))],
        out_specs=pl.BlockSpec((None, tq, group * dv), lambda bb, h, i: (bb, i, h)),
        out_shape=jax.ShapeDtypeStruct((b, lq, kv_heads * group * dv), BF16),
        scratch_shapes=[pltpu.VMEM((rows, dqk), BF16)] + stage_bufs + [
            pltpu.VMEM((1, rows), F32),
            pltpu.VMEM((dve, rows), F32)],
        compiler_params=_compiler_params(("parallel", "parallel", "parallel")),
        name="flash_attention",
    )(q, k, vt)


def _ffn_body(*refs, d, hidden, chunks, with_mixer_out):
    if with_mixer_out:
        h_ref, o_ref, mod_ref, wo_ref, n2_ref, w13_ref, w2_ref, out_ref = refs
    else:
        h_ref, mod_ref, n2_ref, w13_ref, w2_ref, out_ref = refs
    h = h_ref[...]
    mod = mod_ref[...]
    if with_mixer_out:
        h = h + mod[:, 2 * d:3 * d] * _dot(o_ref[...], wo_ref[...])
    x = _norm_mod(h, n2_ref[...], mod[:, 4 * d:5 * d], mod[:, 3 * d:4 * d]).astype(BF16)
    acc = None
    for c0, cw in chunks:
        a1 = _dot(x, w13_ref[:, c0:c0 + cw])
        a3 = _dot(x, w13_ref[:, hidden + c0:hidden + c0 + cw])
        part = _dot((_silu(a1) * a3).astype(BF16), w2_ref[c0:c0 + cw, :])
        acc = part if acc is None else acc + part
    out_ref[...] = h + mod[:, 5 * d:6 * d] * acc


def _hidden_chunks(hidden, width):
    chunks, c0 = [], 0
    while c0 < hidden:
        cw = min(width, hidden - c0)
        chunks.append((c0, cw))
        c0 += cw
    return tuple(chunks)


def _ffn(h, mod, n2, w13, w2, mixer_out=None, w_o=None):
    b, l, d = h.shape
    hidden = w2.shape[0]
    tm = _tile(l, 512)
    fused = mixer_out is not None
    return _row_call(functools.partial(_ffn_body, d=d, hidden=hidden, chunks=_hidden_chunks(hidden, 1024),
                                       with_mixer_out=fused),
                     name="swiglu", batch=b, length=l, tm=tm, rows=(h, mixer_out) if fused else (h,), vecs=(mod,),
                     consts=((w_o,) if fused else ()) + (n2, w13, w2), outs=[(d, F32)])[0]


def _na_proj_body(*refs, with_q, d):
    it = iter(refs)
    h_ref, mod_ref, n1_ref, w_ref, gq_ref, gk_ref = (next(it) for _ in range(6))
    if with_q:
        q_ref = next(it)
    k_ref, v_ref = next(it), next(it)

    nh = NA_HEADS * NA_HEAD_DIM
    mod = mod_ref[...]
    x = _norm_mod(h_ref[...], n1_ref[...], mod[:, d:2 * d], mod[:, 0:d]).astype(BF16)
    tm = x.shape[0]
    left = lax.broadcasted_iota(jnp.int32, (tm, LANES), 1) < NA_HEAD_DIM

    def pair_norm(t, gain2):
        sq = t * t
        sl = jnp.sum(jnp.where(left, sq, 0.0), axis=-1, keepdims=True)
        sr = jnp.sum(jnp.where(left, 0.0, sq), axis=-1, keepdims=True)
        r = jnp.where(left, lax.rsqrt(sl / NA_HEAD_DIM + EPS), lax.rsqrt(sr / NA_HEAD_DIM + EPS))
        return t * r * gain2

    v_ref[...] = _dot(x, w_ref[:, 2 * nh:]).astype(BF16)
    k = _dot(x, w_ref[:, nh:2 * nh])
    gk = gk_ref[...]
    for hp in range(nh // LANES):
        k_ref[:, hp * LANES:(hp + 1) * LANES] = pair_norm(k[:, hp * LANES:(hp + 1) * LANES], gk).astype(BF16)
    if with_q:
        q = _dot(x, w_ref[:, :nh])
        gq = gq_ref[...] * (NA_HEAD_DIM ** -0.5 * LOG2E)
        for hp in range(nh // LANES):
            q_ref[:, hp * LANES:(hp + 1) * LANES] = pair_norm(q[:, hp * LANES:(hp + 1) * LANES], gq).astype(BF16)


def _na_project(h, mod, n1, w_qkv, q_norm, k_norm, with_q, kv_dest):
    b, l, d = h.shape
    tm = _tile(l, 512)
    nh = NA_HEADS * NA_HEAD_DIM
    outs = ([(nh, BF16)] if with_q else []) + _kv_outs((nh, nh), kv_dest)
    gq2 = jnp.concatenate([q_norm, q_norm]).reshape(1, LANES)
    gk2 = jnp.concatenate([k_norm, k_norm]).reshape(1, LANES)
    res = _row_call(functools.partial(_na_proj_body, with_q=with_q, d=d), name="na_proj", batch=b, length=l, tm=tm,
                    rows=(h,), vecs=(mod,), consts=(n1, w_qkv, gq2, gk2), outs=outs)
    return res if with_q else [None] + list(res)


def _na_bias_table(rpb, rows):
    wr, wc, w = NA_WIN_ROWS, NA_WIN_COLS, GRID_W
    nh = rpb.shape[0]
    nsteps = rows // NA_QROWS
    col = np.arange(w)
    col_start = np.clip(col - wc // 2, 0, w - wc)
    col_ok = (col[None, :] >= col_start[:, None]) & (col[None, :] < col_start[:, None] + wc)
    cpad = w - wc
    p = jnp.pad(rpb.astype(F32) * LOG2E, ((0, 0), (0, 0), (cpad, cpad)))
    e = jnp.stack([p[:, :, w - 1 - c:2 * w - 1 - c] for c in range(w)], axis=2)
    e = jnp.where(col_ok[None, None], e, NEG_BIG)
    rpad = NA_KROWS
    e = jnp.pad(e, ((0, 0), (rpad, rpad), (0, 0), (0, 0)))
    blocks, oks = [], []
    for j in (0, 1, nsteps - 1):
        base = int(np.clip(j * NA_QROWS - wr // 2, 0, rows - NA_KROWS))
        for a in range(NA_QROWS):
            r = j * NA_QROWS + a
            r0 = int(np.clip(r - wr // 2, 0, rows - wr))
            krow = base + np.arange(NA_KROWS)
            oks.append((krow >= r0) & (krow < r0 + wr))
            start = base - r + (wr - 1) + rpad
            blocks.append(e[:, start:start + NA_KROWS])
    t = jnp.stack(blocks, axis=1).reshape(nh, 3, NA_QROWS, NA_KROWS, w, w)
    ok = np.stack(oks).reshape(1, 3, NA_QROWS, NA_KROWS, 1, 1)
    t = jnp.where(ok, t, NEG_BIG)
    return t.transpose(0, 1, 2, 4, 3, 5).reshape(nh, 3, NA_QROWS * w, NA_KROWS * w)


def _na_attn_body(q_ref, k_ref, v_ref, kc_ref, vc_ref, bias_ref, o_ref, *, rows, steps_per_block):
    nq = NA_QROWS * GRID_W
    nk = NA_KROWS * GRID_W
    nsteps = rows // NA_QROWS
    rb = pl.program_id(2)
    left = lax.broadcasted_iota(jnp.int32, (nq, LANES), 1) < NA_HEAD_DIM
    head_masks = (left.astype(F32).astype(BF16), jnp.logical_not(left).astype(F32).astype(BF16))
    kc, vc = kc_ref[...], vc_ref[...]

    def step(it, carry):
        j = rb * steps_per_block + it
        pat = jnp.where(j == 0, 0, jnp.where(j == nsteps - 1, 2, 1))
        base = jnp.clip(j * NA_QROWS - NA_WIN_ROWS // 2, 0, rows - NA_KROWS)
        koff = pl.multiple_of(base * GRID_W, GRID_W)
        qoff = pl.multiple_of(it * nq, nq)
        q = q_ref[pl.ds(qoff, nq), :]
        kw = k_ref[pl.ds(koff, nk), :]
        vw = v_ref[pl.ds(koff, nk), :]
        outs = []
        for hh in range(2):
            qm = q * head_masks[hh]
            s_loc = _dot_nt(qm, kw) + bias_ref[hh, pat]
            s_ctx = _dot_nt(qm, kc)
            m = jnp.maximum(jnp.max(s_loc, axis=-1, keepdims=True), jnp.max(s_ctx, axis=-1, keepdims=True))
            p_loc = jnp.exp2(s_loc - m)
            p_ctx = jnp.exp2(s_ctx - m)
            denom = jnp.sum(p_loc, axis=-1, keepdims=True) + jnp.sum(p_ctx, axis=-1, keepdims=True)
            o = _dot(p_loc.astype(BF16), vw) + _dot(p_ctx.astype(BF16), vc)
            outs.append(o * (1.0 / denom))
        o_ref[pl.ds(qoff, nq), :] = jnp.where(left, outs[0], outs[1]).astype(o_ref.dtype)
        return carry

    lax.fori_loop(0, steps_per_block, step, 0)


def _na_attention(q, k, v, bias, lc):
    b, s, nh = q.shape
    rows = s // GRID_W
    nsteps = rows // NA_QROWS
    steps_per_block = 8 if nsteps % 8 == 0 else nsteps
    tq = steps_per_block * NA_QROWS * GRID_W
    nq, nk = bias.shape[2], bias.shape[3]
    return pl.pallas_call(
        functools.partial(_na_attn_body, rows=rows, steps_per_block=steps_per_block),
        grid=(nh // LANES, b, s // tq),
        in_specs=[pl.BlockSpec((None, tq, LANES), lambda hp, bb, i: (bb, i, hp)),
                  pl.BlockSpec((None, s, LANES), lambda hp, bb, i: (bb, 0, hp)),
                  pl.BlockSpec((None, s, LANES), lambda hp, bb, i: (bb, 0, hp)),
                  pl.BlockSpec((None, lc, LANES), lambda hp, bb, i: (bb, s // lc, hp)),
                  pl.BlockSpec((None, lc, LANES), lambda hp, bb, i: (bb, s // lc, hp)),
                  pl.BlockSpec((2, 3, nq, nk), lambda hp, bb, i: (hp, 0, 0, 0))],
        out_specs=pl.BlockSpec((None, tq, LANES), lambda hp, bb, i: (bb, i, hp)),
        out_shape=jax.ShapeDtypeStruct((b, s, nh), BF16),
        compiler_params=_compiler_params(("parallel", "parallel", "parallel")),
        name="na_attention",
    )(q, k, v, k, v, bias)


def _ret_proj_body(*refs, rope, with_qg, d):
    it = iter(refs)
    h_ref = next(it)
    cos_ref = sin_ref = None
    if rope:
        cos_ref, sin_ref = next(it), next(it)
    mod_ref, n1_ref, w_ref = (next(it) for _ in range(3))
    if with_qg:
        q_ref = next(it)
    k_ref, v_ref = next(it), next(it)
    if with_qg:
        g_ref = next(it)

    nk, nv, dk = RET_HEADS * RET_QK_DIM, RET_HEADS * RET_V_DIM, RET_QK_DIM
    mod = mod_ref[...]
    x = _norm_mod(h_ref[...], n1_ref[...], mod[:, d:2 * d], mod[:, 0:d]).astype(BF16)
    if rope:
        cos, sin = cos_ref[...], sin_ref[...]

    def rotate_store(t, ref, scale):
        for hh in range(RET_HEADS):
            x1 = t[:, hh * dk:hh * dk + dk // 2] * scale
            x2 = t[:, hh * dk + dk // 2:(hh + 1) * dk] * scale
            if rope:
                x1, x2 = x1 * cos - x2 * sin, x1 * sin + x2 * cos
            ref[:, hh * dk:hh * dk + dk // 2] = x1.astype(BF16)
            ref[:, hh * dk + dk // 2:(hh + 1) * dk] = x2.astype(BF16)

    rotate_store(_dot(x, w_ref[:, nk:2 * nk]), k_ref, dk ** -0.5)
    v_ref[...] = _dot(x, w_ref[:, 2 * nk:2 * nk + nv]).astype(BF16)
    if with_qg:
        rotate_store(_dot(x, w_ref[:, :nk]), q_ref, 1.0)
        g_ref[...] = _dot(x, w_ref[:, 2 * nk + nv:])


def _ret_project(h, mod, n1, w_qkvg, rope_tabs, with_qg, kv_dest):
    b, l, d = h.shape
    tm = _tile(l, 512)
    nk, nv = RET_HEADS * RET_QK_DIM, RET_HEADS * RET_V_DIM
    rope = rope_tabs is not None
    outs = ([(nk, BF16)] if with_qg else []) + _kv_outs((nk, nv), kv_dest) + ([(nv, F32)] if with_qg else [])
    res = _row_call(functools.partial(_ret_proj_body, rope=rope, with_qg=with_qg, d=d), name="ret_proj", batch=b,
                    length=l, tm=tm, rows=(h,), pos=tuple(rope_tabs) if rope else (), vecs=(mod,),
                    consts=(n1, w_qkvg), outs=outs)
    return res if with_qg else [None, res[0], res[1], None]


def _ret_scan_body(*refs, emit_y, chunk, nchunks):
    if emit_y:
        ld_ref, q_ref, k_ref, v_ref, s0_ref, y_ref, sfin_ref, st_ref = refs
    else:
        ld_ref, k_ref, v_ref, s0_ref, sfin_ref, st_ref = refs
        q_ref = y_ref = None
    direction, hh, t = pl.program_id(0), pl.program_id(2), pl.program_id(3)
    ld = ld_ref[direction, hh]

    @pl.when(t == 0)
    def _():
        st_ref[...] = s0_ref[...]

    ri = lax.broadcasted_iota(jnp.int32, (chunk, chunk), 0)
    ci = lax.broadcasted_iota(jnp.int32, (chunk, chunk), 1)
    dist = jnp.where(direction == 0, ri - ci, ci - ri).astype(F32)
    intra = jnp.where(dist >= 0, jnp.exp(jnp.maximum(dist, 0.0) * ld), 0.0)
    pos = lax.broadcasted_iota(jnp.int32, (chunk, 1), 0)
    rank = jnp.where(direction == 0, pos, chunk - 1 - pos).astype(F32)
    q_decay = jnp.exp((rank + 1.0) * ld)
    k_decay = jnp.exp((chunk - 1.0 - rank) * ld)
    chunk_decay = jnp.exp(jnp.full((1, 1), float(chunk), F32) * ld)

    for c in range(nchunks):
        cc = jnp.where(direction == 0, c, nchunks - 1 - c)
        off = pl.multiple_of(cc * chunk, chunk)
        k = k_ref[pl.ds(off, chunk), :]
        v = v_ref[pl.ds(off, chunk), :]
        state = st_ref[...]
        if emit_y:
            q = q_ref[pl.ds(off, chunk), :]
            sc = _dot_nt(q, k) * intra
            inner = _dot(sc.astype(BF16), v)
            cross = _dot(q, state.astype(BF16)) * q_decay
            y_ref[pl.ds(off, chunk), :] = inner + cross
        kd = (k.astype(F32) * k_decay).astype(BF16)
        st_ref[...] = state * chunk_decay + _dot_tn(kd, v)

    @pl.when(t == pl.num_programs(3) - 1)
    def _():
        sfin_ref[...] = st_ref[...]


def _ret_scan(log_decay, q, k, v, init_state, emit_y, row0, l):
    b = k.shape[0]
    dk, dv, nh = RET_QK_DIM, RET_V_DIM, RET_HEADS
    chunk = 256 if l % 256 == 0 else 128
    tt = 1024 if l % 1024 == 0 else chunk
    nchunks = tt // chunk
    nt = l // tt
    assert row0 % tt == 0

    def tok(d, t):
        return jnp.where(d == 0, t, nt - 1 - t)

    in_specs = [pl.BlockSpec(memory_space=pltpu.SMEM)]
    args = [log_decay]
    if emit_y:
        in_specs.append(pl.BlockSpec((None, tt, dk), lambda d, bb, h, t: (bb, tok(d, t), h)))
        args.append(q)
    in_specs += [pl.BlockSpec((None, tt, dk), lambda d, bb, h, t: (bb, row0 // tt + tok(d, t), h)),
                 pl.BlockSpec((None, tt, dv), lambda d, bb, h, t: (bb, row0 // tt + tok(d, t), h)),
                 pl.BlockSpec((None, None, None, dk, dv), lambda d, bb, h, t: (d, bb, h, 0, 0))]
    args += [k, v, init_state]
    out_shape, out_specs = [], []
    if emit_y:
        out_shape.append(jax.ShapeDtypeStruct((b, l, 2 * nh * dv), F32))
        out_specs.append(pl.BlockSpec((None, tt, dv), lambda d, bb, h, t: (bb, tok(d, t), d * nh + h)))
    out_shape.append(jax.ShapeDtypeStruct((2, b, nh, dk, dv), F32))
    out_specs.append(pl.BlockSpec((None, None, None, dk, dv), lambda d, bb, h, t: (d, bb, h, 0, 0)))
    res = pl.pallas_call(
        functools.partial(_ret_scan_body, emit_y=emit_y, chunk=chunk, nchunks=nchunks),
        grid=(2, b, nh, nt),
        in_specs=in_specs,
        out_specs=out_specs,
        out_shape=out_shape,
        scratch_shapes=[pltpu.VMEM((dk, dv), F32)],
        compiler_params=_compiler_params(("parallel", "parallel", "parallel", "arbitrary")),
        name="ret_scan" if emit_y else "ret_ctx_state",
    )(*args)
    return (res[0], res[1]) if emit_y else (None, res[0])


def _ret_out_body(h_ref, y_ref, g_ref, mod_ref, gn_ref, w_ref, out_ref, *, d):
    nv, dv = RET_HEADS * RET_V_DIM, RET_V_DIM
    y2 = y_ref[...]
    y = y2[:, :nv] + y2[:, nv:]
    gate = _silu(g_ref[...])
    gn = gn_ref[...]
    acc = None
    for hh in range(RET_HEADS):
        yh = y[:, hh * dv:(hh + 1) * dv]
        mu = jnp.mean(yh, axis=-1, keepdims=True)
        yc = yh - mu
        var = jnp.mean(yc * yc, axis=-1, keepdims=True)
        z = yc * lax.rsqrt(var + EPS) * gn[:, hh * dv:(hh + 1) * dv] * gate[:, hh * dv:(hh + 1) * dv]
        part = _dot(z.astype(BF16), w_ref[hh * dv:(hh + 1) * dv, :])
        acc = part if acc is None else acc + part
    out_ref[...] = h_ref[...] + mod_ref[...][:, 2 * d:3 * d] * acc


def _ret_output(h, y, g, mod, out_norm, w_o):
    b, l, d = h.shape
    tm = _tile(l, 256)
    return _row_call(functools.partial(_ret_out_body, d=d), name="ret_out", batch=b, length=l, tm=tm,
                     rows=(h, y, g), vecs=(mod,), consts=(out_norm.reshape(1, -1), w_o), outs=[(d, F32)])[0]


def kernel(x, c, ctx, c_ctx, mod_w, mod_b, norm1, norm2, ffn_w13, ffn_w2, mla_w_down, mla_q_lora_norm, mla_kv_lora_norm, mla_w_uq, mla_w_ukv, mla_q_norm, mla_k_norm, mla_w_o, gqa_w_qkv, gqa_q_norm, gqa_k_norm, gqa_w_o, na_w_qkv, na_q_norm, na_k_norm, na_rpb, na_w_o, ret_w_qkvg, ret_log_decay_fwd, ret_log_decay_bwd, ret_out_norm, ret_w_o):
    b, s, d = x.shape
    lc = ctx.shape[1]
    depth = mod_w.shape[0]

    pad_rows = -(b + 1) % 16
    c_rows = jnp.concatenate([c, c_ctx[None, :], jnp.zeros((pad_rows, d), F32)], axis=0)
    mods = _modulation(c_rows, mod_w, mod_b)

    h, hc = x, ctx
    for i in range(depth):
        need_ctx = i < depth - 1
        mod_l = mods[i, :b].reshape(b, 1, 6 * d)
        mod_c = mods[i, b:b + 1].reshape(1, 1, 6 * d)
        n1 = norm1[i].reshape(1, d)
        n2 = norm2[i].reshape(1, d)
        kind, j = i % 4, i // 4
        if kind == 0:
            prep = _mla_prepare(mla_w_down[j], mla_q_lora_norm[j], mla_kv_lora_norm[j], mla_w_uq[j], mla_w_ukv[j],
                                mla_q_norm[j], mla_k_norm[j])
            tabs = _rope_tables_rolled(s, MLA_ROPE, LANES // 2)
            ql, k_all, v_all = _mla_project(h, mod_l, n1, prep, tabs, True, (s + lc, 0, None))
            qc, k_all, v_all = _mla_project(hc, mod_c, n1, prep, None, need_ctx, (s + lc, s, (k_all, v_all)))
            attn = functools.partial(_flash_attention, kv_heads=MLA_HEADS, group=1, dqk=MLA_QK_PAD, dv=MLA_V, tq=512)
            w_o = mla_w_o[j].astype(BF16)
        elif kind == 1:
            w_qkv = gqa_w_qkv[j].astype(BF16)
            tabs = _rope_tables_rolled(s, GQA_HEAD_DIM, GQA_HEAD_DIM // 2)
            ql, k_all, v_all = _gqa_project(h, mod_l, n1, w_qkv, gqa_q_norm[j], gqa_k_norm[j], tabs, True,
                                            (s + lc, 0, None))
            qc, k_all, v_all = _gqa_project(hc, mod_c, n1, w_qkv, gqa_q_norm[j], gqa_k_norm[j], None, need_ctx,
                                            (s + lc, s, (k_all, v_all)))
            group = GQA_HEADS // GQA_KV_HEADS
            attn = functools.partial(_flash_attention, kv_heads=GQA_KV_HEADS, group=group, dqk=GQA_HEAD_DIM,
                                     dv=GQA_HEAD_DIM, tq=512 // group)
            w_o = gqa_w_o[j].astype(BF16)
        elif kind == 2:
            w_qkv = na_w_qkv[j].astype(BF16)
            ql, k_all, v_all = _na_project(h, mod_l, n1, w_qkv, na_q_norm[j], na_k_norm[j], True, (s + lc, 0, None))
            qc, k_all, v_all = _na_project(hc, mod_c, n1, w_qkv, na_q_norm[j], na_k_norm[j], need_ctx,
                                           (s + lc, s, (k_all, v_all)))
            w_o = na_w_o[j].astype(BF16)
        else:
            w_qkvg = ret_w_qkvg[j].astype(BF16)
            tabs = _rope_cos_sin(s, RET_QK_DIM)
            ql, k_all, v_all, gl = _ret_project(h, mod_l, n1, w_qkvg, tabs, True, (s + lc, 0, None))
            qc, k_all, v_all, gc = _ret_project(hc, mod_c, n1, w_qkvg, None, need_ctx, (s + lc, s, (k_all, v_all)))
            w_o = ret_w_o[j].astype(BF16)

        oc = None
        if kind in (0, 1):
            ol = attn(ql, k_all, v_all, kv_len=s + lc, kv_block=0)
            if need_ctx:
                oc = attn(qc, k_all, v_all, kv_len=lc, kv_block=s // lc)
        elif kind == 2:
            bias = _na_bias_table(na_rpb[j], s // GRID_W)
            ol = _na_attention(ql, k_all, v_all, bias, lc)
            if need_ctx:
                def heads_major(t):
                    return t.reshape(b, lc, NA_HEADS, NA_HEAD_DIM).transpose(0, 2, 1, 3).reshape(
                        b * NA_HEADS, lc, NA_HEAD_DIM)
                oc = _flash_attention(heads_major(qc), heads_major(k_all[:, s:]), heads_major(v_all[:, s:]),
                                      kv_len=lc, kv_block=0, kv_heads=1, group=1, dqk=NA_HEAD_DIM, dv=NA_HEAD_DIM,
                                      tq=lc)
                oc = oc.reshape(b, NA_HEADS, lc, NA_HEAD_DIM).transpose(0, 2, 1, 3).reshape(b, lc, -1)
        else:
            log_decay = jnp.stack([ret_log_decay_fwd[j], ret_log_decay_bwd[j]]).astype(F32)
            zero_state = jnp.zeros((2, b, RET_HEADS, RET_QK_DIM, RET_V_DIM), F32)
            _, ctx_state = _ret_scan(log_decay, None, k_all, v_all, zero_state, False, s, lc)
            y, _ = _ret_scan(log_decay, ql, k_all, v_all, ctx_state, True, 0, s)
            h = _ret_output(h, y, gl, mod_l, ret_out_norm[j], w_o)
            ol = None
            if need_ctx:
                raise NotImplementedError("context update after a retention layer is not needed at this depth")

        w13, w2 = ffn_w13[i].astype(BF16), ffn_w2[i].astype(BF16)
        h = _ffn(h, mod_l, n2, w13, w2, ol, w_o if ol is not None else None)
        if need_ctx:
            hc = _ffn(hc, mod_c, n2, w13, w2, oc, w_o)
    return h
```

```python
import functools

import numpy as np
import jax
import jax.numpy as jnp
from jax import lax
from jax.experimental import pallas as pl
from jax.experimental.pallas import tpu as pltpu

F32 = jnp.float32
BF16 = jnp.bfloat16

GRID_W = 64
ROPE_THETA = 10000.0
EPS = 1e-6
MLA_HEADS, MLA_Q_RANK, MLA_KV_RANK, MLA_NOPE, MLA_ROPE, MLA_V = 8, 384, 256, 128, 64, 128
MLA_QK = MLA_NOPE + MLA_ROPE
MLA_QK_PAD = 256
GQA_HEADS, GQA_KV_HEADS, GQA_HEAD_DIM = 8, 2, 128
NA_HEADS, NA_HEAD_DIM, NA_WIN_ROWS, NA_WIN_COLS = 16, 64, 8, 16
RET_HEADS, RET_QK_DIM, RET_V_DIM = 4, 256, 512

LANES = 128
BF16_SUBLANES = 16
LOG2E = 1.4426950408889634
V7X_VMEM_LIMIT_BYTES = 56 * 1024 * 1024
NEG_BIG = -1e30

FLASH_BUFFERS = 3
FLASH_UNROLL_CHUNKS = 12
NA_QROWS = 4
NA_KROWS = NA_QROWS + NA_WIN_ROWS


def _dot(a, b):
    return jnp.dot(a, b, preferred_element_type=F32)


def _dot_nt(a, b):
    return lax.dot_general(a, b, (((1,), (1,)), ((), ())), preferred_element_type=F32)


def _dot_tn(a, b):
    return lax.dot_general(a, b, (((0,), (0,)), ((), ())), preferred_element_type=F32)


def _silu(x):
    return x / (1.0 + jnp.exp(-x))


def _norm_mod(h, gain, scale, shift):
    y = h * lax.rsqrt(jnp.mean(h * h, axis=-1, keepdims=True) + EPS) * gain
    return y * (1.0 + scale) + shift


def _rms(x, denom):
    return x * lax.rsqrt(jnp.sum(x * x, axis=-1, keepdims=True) / denom + EPS)


def _compiler_params(semantics):
    return pltpu.CompilerParams(dimension_semantics=semantics, vmem_limit_bytes=V7X_VMEM_LIMIT_BYTES)


def _row_call(body, *, name, batch, length, tm, rows=(), pos=(), vecs=(), consts=(), outs=()):
    outs = [o if len(o) == 5 else (o[0], o[1], length, 0, None) for o in outs]
    aliased = [(k, o[4]) for k, o in enumerate(outs) if o[4] is not None]
    n_in = len(rows) + len(pos) + len(vecs) + len(consts)
    if aliased:
        inner = body

        def body(*refs):
            inner(*refs[:n_in], *refs[n_in + len(aliased):])
    in_specs = []
    for a in rows:
        in_specs.append(pl.BlockSpec((None, tm, a.shape[-1]), lambda b, i: (b, i, 0)))
    for a in pos:
        in_specs.append(pl.BlockSpec((tm, a.shape[-1]), lambda b, i: (i, 0)))
    for a in vecs:
        if a.shape[0] == 1:
            in_specs.append(pl.BlockSpec((None, 1, a.shape[-1]), lambda b, i: (0, 0, 0)))
        else:
            in_specs.append(pl.BlockSpec((None, 1, a.shape[-1]), lambda b, i: (b, 0, 0)))
    for a in consts:
        nd = a.ndim
        in_specs.append(pl.BlockSpec(a.shape, lambda b, i, nd=nd: (0,) * nd, pipeline_mode=pl.Buffered(1)))
    in_specs += [pl.BlockSpec(memory_space=pl.ANY) for _ in aliased]
    out_shape = [jax.ShapeDtypeStruct((batch, total, f), dt) for f, dt, total, _, _ in outs]
    out_specs = [pl.BlockSpec((None, tm, f), lambda b, i, off=off // tm: (b, i + off, 0)) for f, _, _, off, _ in outs]
    assert all(off % tm == 0 for _, _, _, off, _ in outs)
    res = pl.pallas_call(
        body,
        grid=(batch, length // tm),
        in_specs=in_specs,
        out_specs=out_specs,
        out_shape=out_shape,
        input_output_aliases={n_in + a: k for a, (k, _) in enumerate(aliased)},
        compiler_params=_compiler_params(("parallel", "parallel")),
        name=name,
    )(*rows, *pos, *vecs, *consts, *[buf for _, buf in aliased])
    return res


def _tile(length, pref):
    return pref if length % pref == 0 else length


def _mod_body(c_ref, w_ref, b_ref, o_ref):
    s = _silu(c_ref[...]).astype(BF16)
    o_ref[...] = _dot(s, w_ref[...].astype(BF16)) + b_ref[...]


def _modulation(c_rows, mod_w, mod_b):
    depth, d, n = mod_w.shape
    tn = 1536 if n % 1536 == 0 else n
    return pl.pallas_call(
        _mod_body,
        grid=(depth, n // tn),
        in_specs=[pl.BlockSpec(c_rows.shape, lambda l, j: (0, 0)),
                  pl.BlockSpec((None, d, tn), lambda l, j: (l, 0, j)),
                  pl.BlockSpec((None, 1, tn), lambda l, j: (l, 0, j))],
        out_specs=pl.BlockSpec((None, c_rows.shape[0], tn), lambda l, j: (l, 0, j)),
        out_shape=jax.ShapeDtypeStruct((depth, c_rows.shape[0], n), F32),
        compiler_params=_compiler_params(("parallel", "parallel")),
        name="modulation",
    )(c_rows, mod_w, mod_b.reshape(depth, 1, n))


def _rope_cos_sin(length, dim):
    t = jnp.arange(length)
    row = (t // GRID_W).astype(F32)
    col = (t % GRID_W).astype(F32)
    quarter = dim // 4
    inv_freq = ROPE_THETA ** (-jnp.arange(quarter, dtype=F32) / quarter)
    ang = jnp.concatenate([row[:, None] * inv_freq, col[:, None] * inv_freq], axis=-1)
    return jnp.cos(ang), jnp.sin(ang)


def _rope_tables_rolled(length, dim, lane_half):
    cos, sin = _rope_cos_sin(length, dim)
    pad = jnp.zeros((length, lane_half - dim // 2), F32)
    cos_t = jnp.concatenate([cos, pad, cos, pad], axis=-1)
    sin_t = jnp.concatenate([-sin, pad, sin, pad], axis=-1)
    return cos_t, sin_t


def _rope_rolled(x, cos_t, sin_t):
    return x * cos_t + pltpu.roll(x, x.shape[-1] // 2, 1) * sin_t


def _mla_proj_body(*refs, rope, with_q, d):
    it = iter(refs)
    h_ref = next(it)
    cos_ref = sin_ref = None
    if rope:
        cos_ref, sin_ref = next(it), next(it)
    mod_ref, n1_ref, wd_ref, gql_ref, gkl_ref, wuq_ref, wuk_ref, wuv_ref, gq_ref, gk_ref = (next(it) for _ in range(10))
    if with_q:
        q_ref = next(it)
    k_ref, v_ref = next(it), next(it)

    mod = mod_ref[...]
    x = _norm_mod(h_ref[...], n1_ref[...], mod[:, d:2 * d], mod[:, 0:d]).astype(BF16)
    dl = _dot(x, wd_ref[...])
    ckv = (_rms(dl[:, MLA_Q_RANK:MLA_Q_RANK + MLA_KV_RANK], MLA_KV_RANK) * gkl_ref[...]).astype(BF16)
    kr = dl[:, MLA_Q_RANK + MLA_KV_RANK:]
    kn = _dot(ckv, wuk_ref[...])
    v_ref[...] = _dot(ckv, wuv_ref[...]).astype(BF16)
    if rope:
        cos_t, sin_t = cos_ref[...], sin_ref[...]
    gk = gk_ref[...]
    kr_ss = jnp.sum(kr * kr, axis=-1, keepdims=True)
    for hh in range(MLA_HEADS):
        kh = kn[:, hh * MLA_NOPE:(hh + 1) * MLA_NOPE]
        r = lax.rsqrt((jnp.sum(kh * kh, axis=-1, keepdims=True) + kr_ss) / MLA_QK + EPS)
        k_ref[:, hh * MLA_QK_PAD:hh * MLA_QK_PAD + MLA_NOPE] = (kh * r * gk[:, :MLA_NOPE]).astype(BF16)
        krh = kr * r * gk[:, MLA_NOPE:]
        if rope:
            krh = _rope_rolled(krh, cos_t, sin_t)
        k_ref[:, hh * MLA_QK_PAD + MLA_NOPE:(hh + 1) * MLA_QK_PAD] = krh.astype(BF16)
    if with_q:
        cq = (_rms(dl[:, :MLA_Q_RANK], MLA_Q_RANK) * gql_ref[...]).astype(BF16)
        q = _dot(cq, wuq_ref[...])
        gq = gq_ref[...]
        scale = MLA_QK ** -0.5 * LOG2E
        for hh in range(MLA_HEADS):
            qn = q[:, hh * MLA_QK_PAD:hh * MLA_QK_PAD + MLA_NOPE]
            qr = q[:, hh * MLA_QK_PAD + MLA_NOPE:(hh + 1) * MLA_QK_PAD]
            ss = jnp.sum(qn * qn, axis=-1, keepdims=True) + jnp.sum(qr * qr, axis=-1, keepdims=True)
            r = lax.rsqrt(ss / MLA_QK + EPS) * scale
            q_ref[:, hh * MLA_QK_PAD:hh * MLA_QK_PAD + MLA_NOPE] = (qn * r * gq[:, :MLA_NOPE]).astype(BF16)
            qr = qr * r * gq[:, MLA_NOPE:]
            if rope:
                qr = _rope_rolled(qr, cos_t, sin_t)
            q_ref[:, hh * MLA_QK_PAD + MLA_NOPE:(hh + 1) * MLA_QK_PAD] = qr.astype(BF16)


def _pad_rope_cols(r):
    half = MLA_ROPE // 2
    z = jnp.zeros(r.shape[:-1] + (LANES // 2 - half,), r.dtype)
    return jnp.concatenate([r[..., :half], z, r[..., half:], z], axis=-1)


def _mla_prepare(w_down, q_lora_norm, kv_lora_norm, w_uq, w_ukv, q_norm, k_norm):
    d = w_down.shape[0]
    nq = MLA_Q_RANK + MLA_KV_RANK
    wd = jnp.concatenate([w_down[:, :nq], _pad_rope_cols(w_down[:, nq:])], axis=1).astype(BF16)
    wq = w_uq.reshape(MLA_Q_RANK, MLA_HEADS, MLA_QK)
    wq = jnp.concatenate([wq[..., :MLA_NOPE], _pad_rope_cols(wq[..., MLA_NOPE:])], axis=-1)
    wq = wq.reshape(MLA_Q_RANK, MLA_HEADS * MLA_QK_PAD).astype(BF16)
    wkv = w_ukv.reshape(MLA_KV_RANK, MLA_HEADS, MLA_NOPE + MLA_V)
    wuk = wkv[..., :MLA_NOPE].reshape(MLA_KV_RANK, MLA_HEADS * MLA_NOPE).astype(BF16)
    wuv = wkv[..., MLA_NOPE:].reshape(MLA_KV_RANK, MLA_HEADS * MLA_V).astype(BF16)

    def pad_gain(g):
        return jnp.concatenate([g[:MLA_NOPE], _pad_rope_cols(g[MLA_NOPE:])]).reshape(1, MLA_QK_PAD)

    return (wd, q_lora_norm.reshape(1, -1), kv_lora_norm.reshape(1, -1), wq, wuk, wuv,
            pad_gain(q_norm), pad_gain(k_norm))


def _kv_outs(widths, kv_dest, batch):
    total, off, bufs = kv_dest
    if bufs is None:
        bufs = [jnp.zeros((batch, total, f), BF16) for f in widths]
    return [(f, BF16, total, off, bufs[k]) for k, f in enumerate(widths)]


def _mla_project(h, mod, n1, prep, rope_tabs, with_q, kv_dest):
    b, l, d = h.shape
    tm = _tile(l, 512)
    rope = rope_tabs is not None
    outs = []
    if with_q:
        outs.append((MLA_HEADS * MLA_QK_PAD, BF16))
    outs += _kv_outs((MLA_HEADS * MLA_QK_PAD, MLA_HEADS * MLA_V), kv_dest, b)
    res = _row_call(functools.partial(_mla_proj_body, rope=rope, with_q=with_q, d=d),
                    name="mla_proj", batch=b, length=l, tm=tm, rows=(h,),
                    pos=tuple(rope_tabs) if rope else (), vecs=(mod,), consts=(n1,) + tuple(prep), outs=outs)
    return res if with_q else [None] + list(res)


def _gqa_proj_body(*refs, rope, with_q, d):
    it = iter(refs)
    h_ref = next(it)
    cos_ref = sin_ref = None
    if rope:
        cos_ref, sin_ref = next(it), next(it)
    mod_ref, n1_ref, w_ref, gq_ref, gk_ref = (next(it) for _ in range(5))
    if with_q:
        q_ref = next(it)
    k_ref, v_ref = next(it), next(it)

    dh = GQA_HEAD_DIM
    nq = GQA_HEADS * dh
    nkv = GQA_KV_HEADS * dh
    mod = mod_ref[...]
    x = _norm_mod(h_ref[...], n1_ref[...], mod[:, d:2 * d], mod[:, 0:d]).astype(BF16)
    if rope:
        cos_t, sin_t = cos_ref[...], sin_ref[...]
    kv = _dot(x, w_ref[:, nq:])
    v_ref[...] = kv[:, nkv:].astype(BF16)
    for hh in range(GQA_KV_HEADS):
        kh = _rms(kv[:, hh * dh:(hh + 1) * dh], dh) * gk_ref[...]
        if rope:
            kh = _rope_rolled(kh, cos_t, sin_t)
        k_ref[:, hh * dh:(hh + 1) * dh] = kh.astype(BF16)
    if with_q:
        q = _dot(x, w_ref[:, :nq])
        gq = gq_ref[...] * (dh ** -0.5 * LOG2E)
        for hh in range(GQA_HEADS):
            qh = _rms(q[:, hh * dh:(hh + 1) * dh], dh) * gq
            if rope:
                qh = _rope_rolled(qh, cos_t, sin_t)
            q_ref[:, hh * dh:(hh + 1) * dh] = qh.astype(BF16)


def _gqa_project(h, mod, n1, w_qkv, q_norm, k_norm, rope_tabs, with_q, kv_dest):
    b, l, d = h.shape
    tm = _tile(l, 512)
    rope = rope_tabs is not None
    outs = []
    if with_q:
        outs.append((GQA_HEADS * GQA_HEAD_DIM, BF16))
    outs += _kv_outs((GQA_KV_HEADS * GQA_HEAD_DIM,) * 2, kv_dest, b)
    res = _row_call(functools.partial(_gqa_proj_body, rope=rope, with_q=with_q, d=d),
                    name="gqa_proj", batch=b, length=l, tm=tm, rows=(h,),
                    pos=tuple(rope_tabs) if rope else (), vecs=(mod,),
                    consts=(n1, w_qkv, q_norm.reshape(1, -1), k_norm.reshape(1, -1)), outs=outs)
    return res if with_q else [None] + list(res)


def _flash_body(q_ref, k_ref, vt_ref, o_ref, qs_ref, *scratch, group, dqk, dv, tq, tk, n, nbuf):
    s_bufs, c_bufs, p_bufs, a_bufs = (scratch[k * nbuf:(k + 1) * nbuf] for k in range(4))
    m_ref, acc_ref = scratch[4 * nbuf:]
    for g in range(group):
        qs_ref[g * tq:(g + 1) * tq, :] = q_ref[:, g * dqk:(g + 1) * dqk]
    m_ref[...] = jnp.full(m_ref.shape, NEG_BIG, F32)
    acc_ref[...] = jnp.zeros(acc_ref.shape, F32)

    def scores(j, slot):
        rows = pl.ds(pl.multiple_of(j * tk, tk), tk)
        s = _dot_nt(k_ref[rows, :], qs_ref[...])
        s_bufs[slot][...] = s
        c_bufs[slot][...] = jnp.max(s, axis=0, keepdims=True)

    def softmax(slot):
        m_prev = m_ref[...]
        m_new = jnp.maximum(m_prev, c_bufs[slot][...])
        a_bufs[slot][...] = jnp.exp2(m_prev - m_new)
        m_ref[...] = m_new
        p_bufs[slot][...] = jnp.exp2(s_bufs[slot][...] - m_new).astype(BF16)

    def weighted_values(j, slot):
        acc_ref[...] = a_bufs[slot][...] * acc_ref[...] + _dot(vt_ref[j], p_bufs[slot][...])

    def steady(j, r):
        scores(j + 1, (r + 1) % nbuf)
        softmax(r)
        weighted_values(j - 1, (r - 1) % nbuf)

    scores(0, 0)
    if n > 1:
        scores(1, 1 % nbuf)
        softmax(0)
        n_rounds = 0 if n <= FLASH_UNROLL_CHUNKS else (n - 2) // nbuf

        def one_round(t, carry):
            for u in range(nbuf):
                steady(1 + nbuf * t + u, (1 + u) % nbuf)
            return carry

        if n_rounds > 0:
            lax.fori_loop(0, n_rounds, one_round, 0)
        for j in range(1 + nbuf * n_rounds, n - 1):
            steady(j, j % nbuf)
        softmax((n - 1) % nbuf)
        weighted_values(n - 2, (n - 2) % nbuf)
    else:
        softmax(0)
    weighted_values(n - 1, (n - 1) % nbuf)

    out = (acc_ref[0:dv, :] * (1.0 / acc_ref[dv:dv + 1, :])).T
    for g in range(group):
        o_ref[:, g * dv:(g + 1) * dv] = out[g * tq:(g + 1) * tq, :].astype(o_ref.dtype)


def _with_ones_row(vt):
    extra = jnp.zeros(vt.shape[:-2] + (BF16_SUBLANES, vt.shape[-1]), vt.dtype).at[..., 0, :].set(1.0)
    return jnp.concatenate([vt, extra], axis=-2)


def _key_chunk(length):
    for tk in (768, 512, 256, 128):
        if length % tk == 0:
            return tk
    return length


def _flash_attention(q, k, v, *, kv_len, kv_block, kv_heads, group, dqk, dv, tq):
    b, lq, _ = q.shape
    tq = _tile(lq, tq)
    tk = _key_chunk(kv_len)
    n = kv_len // tk
    rows = group * tq
    vt = _with_ones_row(
        v[:, kv_block * kv_len:(kv_block + 1) * kv_len].reshape(b, n, tk, kv_heads, dv).transpose(0, 3, 1, 4, 2))
    dve = dv + BF16_SUBLANES
    nbuf = min(FLASH_BUFFERS, n)
    stage_bufs = ([pltpu.VMEM((tk, rows), F32)] * nbuf
                  + [pltpu.VMEM((1, rows), F32)] * nbuf
                  + [pltpu.VMEM((tk, rows), BF16)] * nbuf
                  + [pltpu.VMEM((1, rows), F32)] * nbuf)
    return pl.pallas_call(
        functools.partial(_flash_body, group=group, dqk=dqk, dv=dv, tq=tq, tk=tk, n=n, nbuf=nbuf),
        grid=(b, kv_heads, lq // tq),
        in_specs=[pl.BlockSpec((None, tq, group * dqk), lambda bb, h, i: (bb, i, h)),
                  pl.BlockSpec((None, kv_len, dqk), lambda bb, h, i: (bb, kv_block, h)),
                  pl.BlockSpec((None, None, n, dve, tk), lambda bb, h, i: (bb, h, 0, 0, 0))],
        out_specs=pl.BlockSpec((None, tq, group * dv), lambda bb, h, i: (bb, i, h)),
        out_shape=jax.ShapeDtypeStruct((b, lq, kv_heads * group * dv), BF16),
        scratch_shapes=[pltpu.VMEM((rows, dqk), BF16)] + stage_bufs + [
            pltpu.VMEM((1, rows), F32),
            pltpu.VMEM((dve, rows), F32)],
        compiler_params=_compiler_params(("parallel", "parallel", "parallel")),
        name="flash_attention",
    )(q, k, vt)


def _ffn_body(*refs, d, hidden, chunks, with_mixer_out):
    if with_mixer_out:
        h_ref, o_ref, mod_ref, wo_ref, n2_ref, w13_ref, w2_ref, out_ref = refs
    else:
        h_ref, mod_ref, n2_ref, w13_ref, w2_ref, out_ref = refs
    h = h_ref[...]
    mod = mod_ref[...]
    if with_mixer_out:
        h = h + mod[:, 2 * d:3 * d] * _dot(o_ref[...], wo_ref[...])
    x = _norm_mod(h, n2_ref[...], mod[:, 4 * d:5 * d], mod[:, 3 * d:4 * d]).astype(BF16)
    acc = None
    for c0, cw in chunks:
        a1 = _dot(x, w13_ref[:, c0:c0 + cw])
        a3 = _dot(x, w13_ref[:, hidden + c0:hidden + c0 + cw])
        part = _dot((_silu(a1) * a3).astype(BF16), w2_ref[c0:c0 + cw, :])
        acc = part if acc is None else acc + part
    out_ref[...] = h + mod[:, 5 * d:6 * d] * acc


def _hidden_chunks(hidden, width):
    chunks, c0 = [], 0
    while c0 < hidden:
        cw = min(width, hidden - c0)
        chunks.append((c0, cw))
        c0 += cw
    return tuple(chunks)


def _ffn(h, mod, n2, w13, w2, mixer_out=None, w_o=None):
    b, l, d = h.shape
    hidden = w2.shape[0]
    tm = _tile(l, 512)
    fused = mixer_out is not None
    return _row_call(functools.partial(_ffn_body, d=d, hidden=hidden, chunks=_hidden_chunks(hidden, 1024),
                                       with_mixer_out=fused),
                     name="swiglu", batch=b, length=l, tm=tm, rows=(h, mixer_out) if fused else (h,), vecs=(mod,),
                     consts=((w_o,) if fused else ()) + (n2, w13, w2), outs=[(d, F32)])[0]


def _na_proj_body(*refs, with_q, d):
    it = iter(refs)
    h_ref, mod_ref, n1_ref, w_ref, gq_ref, gk_ref = (next(it) for _ in range(6))
    if with_q:
        q_ref = next(it)
    k_ref, v_ref = next(it), next(it)

    nh = NA_HEADS * NA_HEAD_DIM
    mod = mod_ref[...]
    x = _norm_mod(h_ref[...], n1_ref[...], mod[:, d:2 * d], mod[:, 0:d]).astype(BF16)
    tm = x.shape[0]
    left = lax.broadcasted_iota(jnp.int32, (tm, LANES), 1) < NA_HEAD_DIM

    def pair_norm(t, gain2):
        sq = t * t
        sl = jnp.sum(jnp.where(left, sq, 0.0), axis=-1, keepdims=True)
        sr = jnp.sum(jnp.where(left, 0.0, sq), axis=-1, keepdims=True)
        r = jnp.where(left, lax.rsqrt(sl / NA_HEAD_DIM + EPS), lax.rsqrt(sr / NA_HEAD_DIM + EPS))
        return t * r * gain2

    v_ref[...] = _dot(x, w_ref[:, 2 * nh:]).astype(BF16)
    k = _dot(x, w_ref[:, nh:2 * nh])
    gk = gk_ref[...]
    for hp in range(nh // LANES):
        k_ref[:, hp * LANES:(hp + 1) * LANES] = pair_norm(k[:, hp * LANES:(hp + 1) * LANES], gk).astype(BF16)
    if with_q:
        q = _dot(x, w_ref[:, :nh])
        gq = gq_ref[...] * (NA_HEAD_DIM ** -0.5 * LOG2E)
        for hp in range(nh // LANES):
            q_ref[:, hp * LANES:(hp + 1) * LANES] = pair_norm(q[:, hp * LANES:(hp + 1) * LANES], gq).astype(BF16)


def _na_project(h, mod, n1, w_qkv, q_norm, k_norm, with_q, kv_dest):
    b, l, d = h.shape
    tm = _tile(l, 512)
    nh = NA_HEADS * NA_HEAD_DIM
    outs = ([(nh, BF16)] if with_q else []) + _kv_outs((nh, nh), kv_dest, b)
    gq2 = jnp.concatenate([q_norm, q_norm]).reshape(1, LANES)
    gk2 = jnp.concatenate([k_norm, k_norm]).reshape(1, LANES)
    res = _row_call(functools.partial(_na_proj_body, with_q=with_q, d=d), name="na_proj", batch=b, length=l, tm=tm,
                    rows=(h,), vecs=(mod,), consts=(n1, w_qkv, gq2, gk2), outs=outs)
    return res if with_q else [None] + list(res)


def _na_bias_table(rpb, rows):
    wr, wc, w = NA_WIN_ROWS, NA_WIN_COLS, GRID_W
    nh = rpb.shape[0]
    nsteps = rows // NA_QROWS
    col = np.arange(w)
    col_start = np.clip(col - wc // 2, 0, w - wc)
    col_ok = (col[None, :] >= col_start[:, None]) & (col[None, :] < col_start[:, None] + wc)
    cpad = w - wc
    pf = jnp.flip(jnp.pad(rpb.astype(F32) * LOG2E, ((0, 0), (0, 0), (cpad, cpad))), axis=-1)
    e = jnp.stack([pf[:, :, w - 1 - kc:2 * w - 1 - kc] for kc in range(w)], axis=2)
    e = jnp.where(col_ok.T[None, None], e, NEG_BIG)
    rpad = NA_KROWS
    e = jnp.pad(e, ((0, 0), (rpad, rpad), (0, 0), (0, 0)))
    e = e.reshape(nh // 2, 2, 2 * wr - 1 + 2 * rpad, w, w)
    pats = []
    for j in (0, 1, nsteps - 1):
        base = int(np.clip(j * NA_QROWS - wr // 2, 0, rows - NA_KROWS))
        blocks = []
        for hd in range(2):
            for a in range(NA_QROWS):
                r = j * NA_QROWS + a
                r0 = int(np.clip(r - wr // 2, 0, rows - wr))
                krow = base + np.arange(NA_KROWS)
                ok = ((krow >= r0) & (krow < r0 + wr)).reshape(1, NA_KROWS, 1, 1)
                start = base - r + (wr - 1) + rpad
                blocks.append(jnp.where(ok, e[:, hd, start:start + NA_KROWS], NEG_BIG))
        pats.append(jnp.stack(blocks, axis=3))
    t = jnp.stack(pats, axis=1)
    return t.reshape(nh // 2, 3, NA_KROWS * w, 2 * NA_QROWS * w)


def _na_attn_body(q_ref, k_ref, kc_ref, vt_ref, bias_ref, o_ref, *scratch, rows, steps, nbuf):
    s_bufs, c_bufs, p_bufs = (scratch[k * nbuf:(k + 1) * nbuf] for k in range(3))
    nq = NA_QROWS * GRID_W
    nkw = NA_KROWS * GRID_W
    lc = kc_ref.shape[0]
    blk = vt_ref.shape[-1]
    nsteps = rows // NA_QROWS
    rb = pl.program_id(2)
    left = lax.broadcasted_iota(jnp.int32, (nq, LANES), 1) < NA_HEAD_DIM
    head_masks = (left.astype(F32).astype(BF16), jnp.logical_not(left).astype(F32).astype(BF16))
    top_half = lax.broadcasted_iota(jnp.int32, (LANES, nq), 0) < NA_HEAD_DIM

    def window_base(it):
        j = rb * steps + it
        return j, jnp.clip(j * NA_QROWS - NA_WIN_ROWS // 2, 0, rows - NA_KROWS)

    def scores(it, slot):
        j, base = window_base(it)
        pat = jnp.where(j == 0, 0, jnp.where(j == nsteps - 1, 2, 1))
        koff = pl.multiple_of(base * GRID_W, NA_QROWS * GRID_W)
        q = q_ref[it * nq:(it + 1) * nq, :]
        q2 = jnp.concatenate([q * head_masks[0], q * head_masks[1]], axis=0)
        s_loc = _dot_nt(k_ref[pl.ds(koff, nkw), :], q2) + bias_ref[pat]
        s_ctx = _dot_nt(kc_ref[...], q2)
        s_bufs[slot][0:nkw, :] = s_loc
        s_bufs[slot][nkw:, :] = s_ctx
        c_bufs[slot][...] = jnp.maximum(jnp.max(s_loc, axis=0, keepdims=True), jnp.max(s_ctx, axis=0, keepdims=True))

    def softmax(slot):
        p_bufs[slot][...] = jnp.exp2(s_bufs[slot][...] - c_bufs[slot][...]).astype(BF16)

    def values(it, slot):
        _, base = window_base(it)
        blk0 = base * GRID_W // blk
        p_ref = p_bufs[slot]
        acc = None
        for t in range(nkw // blk):
            part = _dot(vt_ref[blk0 + t], p_ref[t * blk:(t + 1) * blk, :])
            acc = part if acc is None else acc + part
        for t in range(lc // blk):
            acc = acc + _dot(vt_ref[rows * GRID_W // blk + t], p_ref[nkw + t * blk:nkw + (t + 1) * blk, :])
        o = acc[0:LANES, :] * (1.0 / acc[LANES:LANES + 1, :])
        pair = jnp.where(top_half, o[:, :nq], o[:, nq:])
        o_ref[it * nq:(it + 1) * nq, :] = pair.T.astype(o_ref.dtype)

    scores(0, 0)
    for it in range(steps):
        if it + 1 < steps:
            scores(it + 1, (it + 1) % nbuf)
        softmax(it % nbuf)
        if it >= 1:
            values(it - 1, (it - 1) % nbuf)
    values(steps - 1, (steps - 1) % nbuf)


def _na_attention(q, k, v, bias, lc):
    b, s, nh = q.shape
    rows = s // GRID_W
    nsteps = rows // NA_QROWS
    steps = 8 if nsteps % 8 == 0 else nsteps
    tq = steps * NA_QROWS * GRID_W
    nq = NA_QROWS * GRID_W
    nk = NA_KROWS * GRID_W + lc
    blk = NA_QROWS * GRID_W
    nblk = (s + lc) // blk
    vt = _with_ones_row(v.reshape(b, nblk, blk, nh // LANES, LANES).transpose(0, 3, 1, 4, 2))
    nbuf = min(FLASH_BUFFERS, steps)
    stage_bufs = ([pltpu.VMEM((nk, 2 * nq), F32)] * nbuf + [pltpu.VMEM((1, 2 * nq), F32)] * nbuf
                  + [pltpu.VMEM((nk, 2 * nq), BF16)] * nbuf)
    return pl.pallas_call(
        functools.partial(_na_attn_body, rows=rows, steps=steps, nbuf=nbuf),
        grid=(nh // LANES, b, s // tq),
        in_specs=[pl.BlockSpec((None, tq, LANES), lambda hp, bb, i: (bb, i, hp)),
                  pl.BlockSpec((None, s, LANES), lambda hp, bb, i: (bb, 0, hp)),
                  pl.BlockSpec((None, lc, LANES), lambda hp, bb, i: (bb, s // lc, hp)),
                  pl.BlockSpec((None, None, nblk, LANES + BF16_SUBLANES, blk), lambda hp, bb, i: (bb, hp, 0, 0, 0)),
                  pl.BlockSpec((None, 3, NA_KROWS * GRID_W, 2 * nq), lambda hp, bb, i: (hp, 0, 0, 0))],
        out_specs=pl.BlockSpec((None, tq, LANES), lambda hp, bb, i: (bb, i, hp)),
        out_shape=jax.ShapeDtypeStruct((b, s, nh), BF16),
        scratch_shapes=stage_bufs,
        compiler_params=_compiler_params(("parallel", "parallel", "parallel")),
        name="na_attention",
    )(q, k, k, vt, bias)


def _ret_proj_body(*refs, rope, with_qg, d):
    it = iter(refs)
    h_ref = next(it)
    cos_ref = sin_ref = None
    if rope:
        cos_ref, sin_ref = next(it), next(it)
    mod_ref, n1_ref, w_ref = (next(it) for _ in range(3))
    if with_qg:
        q_ref = next(it)
    k_ref, v_ref = next(it), next(it)
    if with_qg:
        g_ref = next(it)

    nk, nv, dk = RET_HEADS * RET_QK_DIM, RET_HEADS * RET_V_DIM, RET_QK_DIM
    mod = mod_ref[...]
    x = _norm_mod(h_ref[...], n1_ref[...], mod[:, d:2 * d], mod[:, 0:d]).astype(BF16)
    if rope:
        cos, sin = cos_ref[...], sin_ref[...]

    def rotate_store(t, ref, scale):
        for hh in range(RET_HEADS):
            x1 = t[:, hh * dk:hh * dk + dk // 2] * scale
            x2 = t[:, hh * dk + dk // 2:(hh + 1) * dk] * scale
            if rope:
                x1, x2 = x1 * cos - x2 * sin, x1 * sin + x2 * cos
            ref[:, hh * dk:hh * dk + dk // 2] = x1.astype(BF16)
            ref[:, hh * dk + dk // 2:(hh + 1) * dk] = x2.astype(BF16)

    rotate_store(_dot(x, w_ref[:, nk:2 * nk]), k_ref, dk ** -0.5)
    v_ref[...] = _dot(x, w_ref[:, 2 * nk:2 * nk + nv]).astype(BF16)
    if with_qg:
        rotate_store(_dot(x, w_ref[:, :nk]), q_ref, 1.0)
        g_ref[...] = _dot(x, w_ref[:, 2 * nk + nv:])


def _ret_project(h, mod, n1, w_qkvg, rope_tabs, with_qg, kv_dest):
    b, l, d = h.shape
    tm = _tile(l, 512)
    nk, nv = RET_HEADS * RET_QK_DIM, RET_HEADS * RET_V_DIM
    rope = rope_tabs is not None
    outs = ([(nk, BF16)] if with_qg else []) + _kv_outs((nk, nv), kv_dest, b) + ([(nv, F32)] if with_qg else [])
    res = _row_call(functools.partial(_ret_proj_body, rope=rope, with_qg=with_qg, d=d), name="ret_proj", batch=b,
                    length=l, tm=tm, rows=(h,), pos=tuple(rope_tabs) if rope else (), vecs=(mod,),
                    consts=(n1, w_qkvg), outs=outs)
    return res if with_qg else [None, res[0], res[1], None]


def _ret_scan_body(*refs, emit_y, chunk, nchunks):
    if emit_y:
        ld_ref, q_ref, k_ref, v_ref, s0_ref, y_ref, sfin_ref, st_ref = refs
    else:
        ld_ref, k_ref, v_ref, s0_ref, sfin_ref, st_ref = refs
        q_ref = y_ref = None
    direction, hh, t = pl.program_id(0), pl.program_id(2), pl.program_id(3)
    ld = ld_ref[direction, hh]

    @pl.when(t == 0)
    def _():
        st_ref[...] = s0_ref[...]

    ri = lax.broadcasted_iota(jnp.int32, (chunk, chunk), 0)
    ci = lax.broadcasted_iota(jnp.int32, (chunk, chunk), 1)
    dist = jnp.where(direction == 0, ri - ci, ci - ri).astype(F32)
    intra = jnp.where(dist >= 0, jnp.exp(jnp.maximum(dist, 0.0) * ld), 0.0)
    pos = lax.broadcasted_iota(jnp.int32, (chunk, 1), 0)
    rank = jnp.where(direction == 0, pos, chunk - 1 - pos).astype(F32)
    q_decay = jnp.exp((rank + 1.0) * ld)
    k_decay = jnp.exp((chunk - 1.0 - rank) * ld)
    chunk_decay = jnp.exp(jnp.full((1, 1), float(chunk), F32) * ld)

    for c in range(nchunks):
        cc = jnp.where(direction == 0, c, nchunks - 1 - c)
        off = pl.multiple_of(cc * chunk, chunk)
        k = k_ref[pl.ds(off, chunk), :]
        v = v_ref[pl.ds(off, chunk), :]
        state = st_ref[...]
        if emit_y:
            q = q_ref[pl.ds(off, chunk), :]
            sc = _dot_nt(q, k) * intra
            inner = _dot(sc.astype(BF16), v)
            cross = _dot(q, state.astype(BF16)) * q_decay
            y_ref[pl.ds(off, chunk), :] = inner + cross
        kd = (k.astype(F32) * k_decay).astype(BF16)
        st_ref[...] = state * chunk_decay + _dot_tn(kd, v)

    @pl.when(t == pl.num_programs(3) - 1)
    def _():
        sfin_ref[...] = st_ref[...]


def _ret_scan(log_decay, q, k, v, init_state, emit_y, row0, l):
    b = k.shape[0]
    dk, dv, nh = RET_QK_DIM, RET_V_DIM, RET_HEADS
    chunk = 256 if l % 256 == 0 else 128
    tt = 1024 if l % 1024 == 0 else chunk
    nchunks = tt // chunk
    nt = l // tt
    assert row0 % tt == 0

    def tok(d, t):
        return jnp.where(d == 0, t, nt - 1 - t)

    in_specs = [pl.BlockSpec(memory_space=pltpu.SMEM)]
    args = [log_decay]
    if emit_y:
        in_specs.append(pl.BlockSpec((None, tt, dk), lambda d, bb, h, t: (bb, tok(d, t), h)))
        args.append(q)
    in_specs += [pl.BlockSpec((None, tt, dk), lambda d, bb, h, t: (bb, row0 // tt + tok(d, t), h)),
                 pl.BlockSpec((None, tt, dv), lambda d, bb, h, t: (bb, row0 // tt + tok(d, t), h)),
                 pl.BlockSpec((None, None, None, dk, dv), lambda d, bb, h, t: (d, bb, h, 0, 0))]
    args += [k, v, init_state]
    out_shape, out_specs = [], []
    if emit_y:
        out_shape.append(jax.ShapeDtypeStruct((b, l, 2 * nh * dv), F32))
        out_specs.append(pl.BlockSpec((None, tt, dv), lambda d, bb, h, t: (bb, tok(d, t), d * nh + h)))
    out_shape.append(jax.ShapeDtypeStruct((2, b, nh, dk, dv), F32))
    out_specs.append(pl.BlockSpec((None, None, None, dk, dv), lambda d, bb, h, t: (d, bb, h, 0, 0)))
    res = pl.pallas_call(
        functools.partial(_ret_scan_body, emit_y=emit_y, chunk=chunk, nchunks=nchunks),
        grid=(2, b, nh, nt),
        in_specs=in_specs,
        out_specs=out_specs,
        out_shape=out_shape,
        scratch_shapes=[pltpu.VMEM((dk, dv), F32)],
        compiler_params=_compiler_params(("parallel", "parallel", "parallel", "arbitrary")),
        name="ret_scan" if emit_y else "ret_ctx_state",
    )(*args)
    return (res[0], res[1]) if emit_y else (None, res[0])


def _ret_out_body(h_ref, y_ref, g_ref, mod_ref, gn_ref, w_ref, out_ref, *, d):
    nv, dv = RET_HEADS * RET_V_DIM, RET_V_DIM
    y2 = y_ref[...]
    y = y2[:, :nv] + y2[:, nv:]
    gate = _silu(g_ref[...])
    gn = gn_ref[...]
    acc = None
    for hh in range(RET_HEADS):
        yh = y[:, hh * dv:(hh + 1) * dv]
        mu = jnp.mean(yh, axis=-1, keepdims=True)
        yc = yh - mu
        var = jnp.mean(yc * yc, axis=-1, keepdims=True)
        z = yc * lax.rsqrt(var + EPS) * gn[:, hh * dv:(hh + 1) * dv] * gate[:, hh * dv:(hh + 1) * dv]
        part = _dot(z.astype(BF16), w_ref[hh * dv:(hh + 1) * dv, :])
        acc = part if acc is None else acc + part
    out_ref[...] = h_ref[...] + mod_ref[...][:, 2 * d:3 * d] * acc


def _ret_output(h, y, g, mod, out_norm, w_o):
    b, l, d = h.shape
    tm = _tile(l, 256)
    return _row_call(functools.partial(_ret_out_body, d=d), name="ret_out", batch=b, length=l, tm=tm,
                     rows=(h, y, g), vecs=(mod,), consts=(out_norm.reshape(1, -1), w_o), outs=[(d, F32)])[0]


def kernel(x, c, ctx, c_ctx, mod_w, mod_b, norm1, norm2, ffn_w13, ffn_w2, mla_w_down, mla_q_lora_norm, mla_kv_lora_norm, mla_w_uq, mla_w_ukv, mla_q_norm, mla_k_norm, mla_w_o, gqa_w_qkv, gqa_q_norm, gqa_k_norm, gqa_w_o, na_w_qkv, na_q_norm, na_k_norm, na_rpb, na_w_o, ret_w_qkvg, ret_log_decay_fwd, ret_log_decay_bwd, ret_out_norm, ret_w_o):
    b, s, d = x.shape
    lc = ctx.shape[1]
    depth = mod_w.shape[0]

    pad_rows = -(b + 1) % 16
    c_rows = jnp.concatenate([c, c_ctx[None, :], jnp.zeros((pad_rows, d), F32)], axis=0)
    mods = _modulation(c_rows, mod_w, mod_b)

    h, hc = x, ctx
    for i in range(depth):
        need_ctx = i < depth - 1
        mod_l = mods[i, :b].reshape(b, 1, 6 * d)
        mod_c = mods[i, b:b + 1].reshape(1, 1, 6 * d)
        n1 = norm1[i].reshape(1, d)
        n2 = norm2[i].reshape(1, d)
        kind, j = i % 4, i // 4
        if kind == 0:
            prep = _mla_prepare(mla_w_down[j], mla_q_lora_norm[j], mla_kv_lora_norm[j], mla_w_uq[j], mla_w_ukv[j],
                                mla_q_norm[j], mla_k_norm[j])
            tabs = _rope_tables_rolled(s, MLA_ROPE, LANES // 2)
            ql, k_all, v_all = _mla_project(h, mod_l, n1, prep, tabs, True, (s + lc, 0, None))
            qc, k_all, v_all = _mla_project(hc, mod_c, n1, prep, None, need_ctx, (s + lc, s, (k_all, v_all)))
            attn = functools.partial(_flash_attention, kv_heads=MLA_HEADS, group=1, dqk=MLA_QK_PAD, dv=MLA_V, tq=512)
            w_o = mla_w_o[j].astype(BF16)
        elif kind == 1:
            w_qkv = gqa_w_qkv[j].astype(BF16)
            tabs = _rope_tables_rolled(s, GQA_HEAD_DIM, GQA_HEAD_DIM // 2)
            ql, k_all, v_all = _gqa_project(h, mod_l, n1, w_qkv, gqa_q_norm[j], gqa_k_norm[j], tabs, True,
                                            (s + lc, 0, None))
            qc, k_all, v_all = _gqa_project(hc, mod_c, n1, w_qkv, gqa_q_norm[j], gqa_k_norm[j], None, need_ctx,
                                            (s + lc, s, (k_all, v_all)))
            group = GQA_HEADS // GQA_KV_HEADS
            attn = functools.partial(_flash_attention, kv_heads=GQA_KV_HEADS, group=group, dqk=GQA_HEAD_DIM,
                                     dv=GQA_HEAD_DIM, tq=512 // group)
            w_o = gqa_w_o[j].astype(BF16)
        elif kind == 2:
            w_qkv = na_w_qkv[j].astype(BF16)
            ql, k_all, v_all = _na_project(h, mod_l, n1, w_qkv, na_q_norm[j], na_k_norm[j], True, (s + lc, 0, None))
            qc, k_all, v_all = _na_project(hc, mod_c, n1, w_qkv, na_q_norm[j], na_k_norm[j], need_ctx,
                                           (s + lc, s, (k_all, v_all)))
            w_o = na_w_o[j].astype(BF16)
        else:
            w_qkvg = ret_w_qkvg[j].astype(BF16)
            tabs = _rope_cos_sin(s, RET_QK_DIM)
            ql, k_all, v_all, gl = _ret_project(h, mod_l, n1, w_qkvg, tabs, True, (s + lc, 0, None))
            qc, k_all, v_all, gc = _ret_project(hc, mod_c, n1, w_qkvg, None, need_ctx, (s + lc, s, (k_all, v_all)))
            w_o = ret_w_o[j].astype(BF16)

        oc = None
        if kind in (0, 1):
            ol = attn(ql, k_all, v_all, kv_len=s + lc, kv_block=0)
            if need_ctx:
                oc = attn(qc, k_all, v_all, kv_len=lc, kv_block=s // lc)
        elif kind == 2:
            bias = _na_bias_table(na_rpb[j], s // GRID_W)
            ol = _na_attention(ql, k_all, v_all, bias, lc)
            if need_ctx:
                def heads_major(t):
                    return t.reshape(b, lc, NA_HEADS, NA_HEAD_DIM).transpose(0, 2, 1, 3).reshape(
                        b * NA_HEADS, lc, NA_HEAD_DIM)
                oc = _flash_attention(heads_major(qc), heads_major(k_all[:, s:]), heads_major(v_all[:, s:]),
                                      kv_len=lc, kv_block=0, kv_heads=1, group=1, dqk=NA_HEAD_DIM, dv=NA_HEAD_DIM,
                                      tq=lc)
                oc = oc.reshape(b, NA_HEADS, lc, NA_HEAD_DIM).transpose(0, 2, 1, 3).reshape(b, lc, -1)
        else:
            log_decay = jnp.stack([ret_log_decay_fwd[j], ret_log_decay_bwd[j]]).astype(F32)
            zero_state = jnp.zeros((2, b, RET_HEADS, RET_QK_DIM, RET_V_DIM), F32)
            _, ctx_state = _ret_scan(log_decay, None, k_all, v_all, zero_state, False, s, lc)
            y, _ = _ret_scan(log_decay, ql, k_all, v_all, ctx_state, True, 0, s)
            h = _ret_output(h, y, gl, mod_l, ret_out_norm[j], w_o)
            ol = None
            if need_ctx:
                raise NotImplementedError("context update after a retention layer is not needed at this depth")

        w13, w2 = ffn_w13[i].astype(BF16), ffn_w2[i].astype(BF16)
        h = _ffn(h, mod_l, n2, w13, w2, ol, w_o if ol is not None else None)
        if need_ctx:
            hc = _ffn(hc, mod_c, n2, w13, w2, oc, w_o)
    return h
```

```python
import functools

import numpy as np
import jax
import jax.numpy as jnp
from jax import lax
from jax.experimental import pallas as pl
from jax.experimental.pallas import tpu as pltpu

F32 = jnp.float32
BF16 = jnp.bfloat16

GRID_W = 64
ROPE_THETA = 10000.0
EPS = 1e-6
MLA_HEADS, MLA_Q_RANK, MLA_KV_RANK, MLA_NOPE, MLA_ROPE, MLA_V = 8, 384, 256, 128, 64, 128
MLA_QK = MLA_NOPE + MLA_ROPE
MLA_QK_PAD = 256
GQA_HEADS, GQA_KV_HEADS, GQA_HEAD_DIM = 8, 2, 128
NA_HEADS, NA_HEAD_DIM, NA_WIN_ROWS, NA_WIN_COLS = 16, 64, 8, 16
RET_HEADS, RET_QK_DIM, RET_V_DIM = 4, 256, 512

LANES = 128
BF16_SUBLANES = 16
LOG2E = 1.4426950408889634
V7X_VMEM_LIMIT_BYTES = 56 * 1024 * 1024
NEG_BIG = -1e30

FLASH_BUFFERS = 3
FLASH_UNROLL_CHUNKS = 12
NA_QROWS = 4
NA_KROWS = NA_QROWS + NA_WIN_ROWS


def _dot(a, b):
    return jnp.dot(a, b, preferred_element_type=F32)


def _dot_nt(a, b):
    return lax.dot_general(a, b, (((1,), (1,)), ((), ())), preferred_element_type=F32)


def _dot_tn(a, b):
    return lax.dot_general(a, b, (((0,), (0,)), ((), ())), preferred_element_type=F32)


def _silu(x):
    return x / (1.0 + jnp.exp(-x))


def _norm_mod(h, gain, scale, shift):
    y = h * lax.rsqrt(jnp.mean(h * h, axis=-1, keepdims=True) + EPS) * gain
    return y * (1.0 + scale) + shift


def _rms(x, denom):
    return x * lax.rsqrt(jnp.sum(x * x, axis=-1, keepdims=True) / denom + EPS)


def _compiler_params(semantics):
    return pltpu.CompilerParams(dimension_semantics=semantics, vmem_limit_bytes=V7X_VMEM_LIMIT_BYTES)


def _row_call(body, *, name, batch, length, tm, rows=(), pos=(), vecs=(), consts=(), outs=(), tails=()):
    n_main = length // tm
    n_tail = tails[0].shape[1] // tm if tails else 0
    n_in = len(rows) + len(pos) + len(vecs) + len(consts)
    n_plain = len(outs) - len(tails)
    if tails:
        assert all(t.shape[1] == n_tail * tm for t in tails)
        inner = body

        def body(*refs):
            out_refs = refs[n_in + len(tails):]
            i = pl.program_id(1)

            @pl.when(i < n_main)
            def _():
                inner(*refs[:n_in], *out_refs)

            @pl.when(i >= n_main)
            def _():
                for t_ref, o_ref in zip(refs[n_in:n_in + len(tails)], out_refs[n_plain:]):
                    o_ref[...] = t_ref[...]

    def main(i):
        return jnp.minimum(i, n_main - 1) if tails else i

    in_specs = []
    for a in rows:
        in_specs.append(pl.BlockSpec((None, tm, a.shape[-1]), lambda b, i: (b, main(i), 0)))
    for a in pos:
        in_specs.append(pl.BlockSpec((tm, a.shape[-1]), lambda b, i: (main(i), 0)))
    for a in vecs:
        if a.shape[0] == 1:
            in_specs.append(pl.BlockSpec((None, 1, a.shape[-1]), lambda b, i: (0, 0, 0)))
        else:
            in_specs.append(pl.BlockSpec((None, 1, a.shape[-1]), lambda b, i: (b, 0, 0)))
    for a in consts:
        nd = a.ndim
        in_specs.append(pl.BlockSpec(a.shape, lambda b, i, nd=nd: (0,) * nd, pipeline_mode=pl.Buffered(1)))
    for a in tails:
        in_specs.append(pl.BlockSpec((None, tm, a.shape[-1]), lambda b, i: (b, jnp.maximum(i - n_main, 0), 0)))
    out_shape, out_specs = [], []
    for k, (f, dt) in enumerate(outs):
        if k < n_plain:
            out_shape.append(jax.ShapeDtypeStruct((batch, length, f), dt))
            out_specs.append(pl.BlockSpec((None, tm, f), lambda b, i: (b, main(i), 0)))
        else:
            out_shape.append(jax.ShapeDtypeStruct((batch, length + n_tail * tm, f), dt))
            out_specs.append(pl.BlockSpec((None, tm, f), lambda b, i: (b, i, 0)))
    return pl.pallas_call(
        body,
        grid=(batch, n_main + n_tail),
        in_specs=in_specs,
        out_specs=out_specs,
        out_shape=out_shape,
        compiler_params=_compiler_params(("parallel", "arbitrary" if tails else "parallel")),
        name=name,
    )(*rows, *pos, *vecs, *consts, *tails)


def _tile(length, pref, also=None):
    t = pref
    while t >= 128:
        if length % t == 0 and (also is None or also % t == 0):
            return t
        t //= 2
    return length


def _mod_body(c_ref, w_ref, b_ref, o_ref):
    s = _silu(c_ref[...]).astype(BF16)
    o_ref[...] = _dot(s, w_ref[...].astype(BF16)) + b_ref[...]


def _modulation(c_rows, mod_w, mod_b):
    depth, d, n = mod_w.shape
    tn = 1536 if n % 1536 == 0 else n
    return pl.pallas_call(
        _mod_body,
        grid=(depth, n // tn),
        in_specs=[pl.BlockSpec(c_rows.shape, lambda l, j: (0, 0)),
                  pl.BlockSpec((None, d, tn), lambda l, j: (l, 0, j)),
                  pl.BlockSpec((None, 1, tn), lambda l, j: (l, 0, j))],
        out_specs=pl.BlockSpec((None, c_rows.shape[0], tn), lambda l, j: (l, 0, j)),
        out_shape=jax.ShapeDtypeStruct((depth, c_rows.shape[0], n), F32),
        compiler_params=_compiler_params(("parallel", "parallel")),
        name="modulation",
    )(c_rows, mod_w, mod_b.reshape(depth, 1, n))


def _rope_cos_sin(length, dim):
    t = jnp.arange(length)
    row = (t // GRID_W).astype(F32)
    col = (t % GRID_W).astype(F32)
    quarter = dim // 4
    inv_freq = ROPE_THETA ** (-jnp.arange(quarter, dtype=F32) / quarter)
    ang = jnp.concatenate([row[:, None] * inv_freq, col[:, None] * inv_freq], axis=-1)
    return jnp.cos(ang), jnp.sin(ang)


def _rope_tables_rolled(length, dim, lane_half):
    cos, sin = _rope_cos_sin(length, dim)
    pad = jnp.zeros((length, lane_half - dim // 2), F32)
    cos_t = jnp.concatenate([cos, pad, cos, pad], axis=-1)
    sin_t = jnp.concatenate([-sin, pad, sin, pad], axis=-1)
    return cos_t, sin_t


def _rope_rolled(x, cos_t, sin_t):
    return x * cos_t + pltpu.roll(x, x.shape[-1] // 2, 1) * sin_t


def _mla_proj_body(*refs, rope, with_q, d):
    it = iter(refs)
    h_ref = next(it)
    cos_ref = sin_ref = None
    if rope:
        cos_ref, sin_ref = next(it), next(it)
    mod_ref, n1_ref, wd_ref, gql_ref, gkl_ref, wuq_ref, wuk_ref, wuv_ref, gq_ref, gk_ref = (next(it) for _ in range(10))
    if with_q:
        q_ref = next(it)
    k_ref, v_ref = next(it), next(it)

    mod = mod_ref[...]
    x = _norm_mod(h_ref[...], n1_ref[...], mod[:, d:2 * d], mod[:, 0:d]).astype(BF16)
    dl = _dot(x, wd_ref[...])
    ckv = (_rms(dl[:, MLA_Q_RANK:MLA_Q_RANK + MLA_KV_RANK], MLA_KV_RANK) * gkl_ref[...]).astype(BF16)
    kr = dl[:, MLA_Q_RANK + MLA_KV_RANK:]
    kn = _dot(ckv, wuk_ref[...])
    v_ref[...] = _dot(ckv, wuv_ref[...]).astype(BF16)
    if rope:
        cos_t, sin_t = cos_ref[...], sin_ref[...]
    gk = gk_ref[...]
    kr_ss = jnp.sum(kr * kr, axis=-1, keepdims=True)
    for hh in range(MLA_HEADS):
        kh = kn[:, hh * MLA_NOPE:(hh + 1) * MLA_NOPE]
        r = lax.rsqrt((jnp.sum(kh * kh, axis=-1, keepdims=True) + kr_ss) / MLA_QK + EPS)
        k_ref[:, hh * MLA_QK_PAD:hh * MLA_QK_PAD + MLA_NOPE] = (kh * r * gk[:, :MLA_NOPE]).astype(BF16)
        krh = kr * r * gk[:, MLA_NOPE:]
        if rope:
            krh = _rope_rolled(krh, cos_t, sin_t)
        k_ref[:, hh * MLA_QK_PAD + MLA_NOPE:(hh + 1) * MLA_QK_PAD] = krh.astype(BF16)
    if with_q:
        cq = (_rms(dl[:, :MLA_Q_RANK], MLA_Q_RANK) * gql_ref[...]).astype(BF16)
        q = _dot(cq, wuq_ref[...])
        gq = gq_ref[...]
        scale = MLA_QK ** -0.5 * LOG2E
        for hh in range(MLA_HEADS):
            qn = q[:, hh * MLA_QK_PAD:hh * MLA_QK_PAD + MLA_NOPE]
            qr = q[:, hh * MLA_QK_PAD + MLA_NOPE:(hh + 1) * MLA_QK_PAD]
            ss = jnp.sum(qn * qn, axis=-1, keepdims=True) + jnp.sum(qr * qr, axis=-1, keepdims=True)
            r = lax.rsqrt(ss / MLA_QK + EPS) * scale
            q_ref[:, hh * MLA_QK_PAD:hh * MLA_QK_PAD + MLA_NOPE] = (qn * r * gq[:, :MLA_NOPE]).astype(BF16)
            qr = qr * r * gq[:, MLA_NOPE:]
            if rope:
                qr = _rope_rolled(qr, cos_t, sin_t)
            q_ref[:, hh * MLA_QK_PAD + MLA_NOPE:(hh + 1) * MLA_QK_PAD] = qr.astype(BF16)


def _pad_rope_cols(r):
    half = MLA_ROPE // 2
    z = jnp.zeros(r.shape[:-1] + (LANES // 2 - half,), r.dtype)
    return jnp.concatenate([r[..., :half], z, r[..., half:], z], axis=-1)


def _mla_prepare(w_down, q_lora_norm, kv_lora_norm, w_uq, w_ukv, q_norm, k_norm):
    d = w_down.shape[0]
    nq = MLA_Q_RANK + MLA_KV_RANK
    wd = jnp.concatenate([w_down[:, :nq], _pad_rope_cols(w_down[:, nq:])], axis=1).astype(BF16)
    wq = w_uq.reshape(MLA_Q_RANK, MLA_HEADS, MLA_QK)
    wq = jnp.concatenate([wq[..., :MLA_NOPE], _pad_rope_cols(wq[..., MLA_NOPE:])], axis=-1)
    wq = wq.reshape(MLA_Q_RANK, MLA_HEADS * MLA_QK_PAD).astype(BF16)
    wkv = w_ukv.reshape(MLA_KV_RANK, MLA_HEADS, MLA_NOPE + MLA_V)
    wuk = wkv[..., :MLA_NOPE].reshape(MLA_KV_RANK, MLA_HEADS * MLA_NOPE).astype(BF16)
    wuv = wkv[..., MLA_NOPE:].reshape(MLA_KV_RANK, MLA_HEADS * MLA_V).astype(BF16)

    def pad_gain(g):
        return jnp.concatenate([g[:MLA_NOPE], _pad_rope_cols(g[MLA_NOPE:])]).reshape(1, MLA_QK_PAD)

    return (wd, q_lora_norm.reshape(1, -1), kv_lora_norm.reshape(1, -1), wq, wuk, wuv,
            pad_gain(q_norm), pad_gain(k_norm))


def _proj_tile(length, kv_tail):
    return _tile(length, 512, kv_tail[0].shape[1] if kv_tail else None)


def _mla_project(h, mod, n1, prep, rope_tabs, with_q, kv_tail=()):
    b, l, d = h.shape
    rope = rope_tabs is not None
    outs = []
    if with_q:
        outs.append((MLA_HEADS * MLA_QK_PAD, BF16))
    outs += [(MLA_HEADS * MLA_QK_PAD, BF16), (MLA_HEADS * MLA_V, BF16)]
    res = _row_call(functools.partial(_mla_proj_body, rope=rope, with_q=with_q, d=d),
                    name="mla_proj", batch=b, length=l, tm=_proj_tile(l, kv_tail), rows=(h,),
                    pos=tuple(rope_tabs) if rope else (), vecs=(mod,), consts=(n1,) + tuple(prep), outs=outs,
                    tails=kv_tail)
    return res if with_q else [None] + list(res)


def _gqa_proj_body(*refs, rope, with_q, d):
    it = iter(refs)
    h_ref = next(it)
    cos_ref = sin_ref = None
    if rope:
        cos_ref, sin_ref = next(it), next(it)
    mod_ref, n1_ref, w_ref, gq_ref, gk_ref = (next(it) for _ in range(5))
    if with_q:
        q_ref = next(it)
    k_ref, v_ref = next(it), next(it)

    dh = GQA_HEAD_DIM
    nq = GQA_HEADS * dh
    nkv = GQA_KV_HEADS * dh
    mod = mod_ref[...]
    x = _norm_mod(h_ref[...], n1_ref[...], mod[:, d:2 * d], mod[:, 0:d]).astype(BF16)
    if rope:
        cos_t, sin_t = cos_ref[...], sin_ref[...]
    kv = _dot(x, w_ref[:, nq:])
    v_ref[...] = kv[:, nkv:].astype(BF16)
    for hh in range(GQA_KV_HEADS):
        kh = _rms(kv[:, hh * dh:(hh + 1) * dh], dh) * gk_ref[...]
        if rope:
            kh = _rope_rolled(kh, cos_t, sin_t)
        k_ref[:, hh * dh:(hh + 1) * dh] = kh.astype(BF16)
    if with_q:
        q = _dot(x, w_ref[:, :nq])
        gq = gq_ref[...] * (dh ** -0.5 * LOG2E)
        for hh in range(GQA_HEADS):
            qh = _rms(q[:, hh * dh:(hh + 1) * dh], dh) * gq
            if rope:
                qh = _rope_rolled(qh, cos_t, sin_t)
            q_ref[:, hh * dh:(hh + 1) * dh] = qh.astype(BF16)


def _gqa_project(h, mod, n1, w_qkv, q_norm, k_norm, rope_tabs, with_q, kv_tail=()):
    b, l, d = h.shape
    rope = rope_tabs is not None
    outs = []
    if with_q:
        outs.append((GQA_HEADS * GQA_HEAD_DIM, BF16))
    outs += [(GQA_KV_HEADS * GQA_HEAD_DIM, BF16)] * 2
    res = _row_call(functools.partial(_gqa_proj_body, rope=rope, with_q=with_q, d=d),
                    name="gqa_proj", batch=b, length=l, tm=_proj_tile(l, kv_tail), rows=(h,),
                    pos=tuple(rope_tabs) if rope else (), vecs=(mod,),
                    consts=(n1, w_qkv, q_norm.reshape(1, -1), k_norm.reshape(1, -1)), outs=outs, tails=kv_tail)
    return res if with_q else [None] + list(res)


def _flash_body(q_ref, k_ref, vt_ref, o_ref, qs_ref, *scratch, group, dqk, dv, tq, tk, n, nbuf):
    s_bufs, c_bufs, p_bufs, a_bufs = (scratch[k * nbuf:(k + 1) * nbuf] for k in range(4))
    m_ref, acc_ref = scratch[4 * nbuf:]
    for g in range(group):
        qs_ref[g * tq:(g + 1) * tq, :] = q_ref[:, g * dqk:(g + 1) * dqk]
    m_ref[...] = jnp.full(m_ref.shape, NEG_BIG, F32)
    acc_ref[...] = jnp.zeros(acc_ref.shape, F32)

    def scores(j, slot):
        rows = pl.ds(pl.multiple_of(j * tk, tk), tk)
        s = _dot_nt(k_ref[rows, :], qs_ref[...])
        s_bufs[slot][...] = s
        c_bufs[slot][...] = jnp.max(s, axis=0, keepdims=True)

    def softmax(slot):
        m_prev = m_ref[...]
        m_new = jnp.maximum(m_prev, c_bufs[slot][...])
        a_bufs[slot][...] = jnp.exp2(m_prev - m_new)
        m_ref[...] = m_new
        p_bufs[slot][...] = jnp.exp2(s_bufs[slot][...] - m_new).astype(BF16)

    def weighted_values(j, slot):
        acc_ref[...] = a_bufs[slot][...] * acc_ref[...] + _dot(vt_ref[j], p_bufs[slot][...])

    def steady(j, r):
        scores(j + 1, (r + 1) % nbuf)
        softmax(r)
        weighted_values(j - 1, (r - 1) % nbuf)

    scores(0, 0)
    if n > 1:
        scores(1, 1 % nbuf)
        softmax(0)
        n_rounds = 0 if n <= FLASH_UNROLL_CHUNKS else (n - 2) // nbuf

        def one_round(t, carry):
            for u in range(nbuf):
                steady(1 + nbuf * t + u, (1 + u) % nbuf)
            return carry

        if n_rounds > 0:
            lax.fori_loop(0, n_rounds, one_round, 0)
        for j in range(1 + nbuf * n_rounds, n - 1):
            steady(j, j % nbuf)
        softmax((n - 1) % nbuf)
        weighted_values(n - 2, (n - 2) % nbuf)
    else:
        softmax(0)
    weighted_values(n - 1, (n - 1) % nbuf)

    out = (acc_ref[0:dv, :] * (1.0 / acc_ref[dv:dv + 1, :])).T
    for g in range(group):
        o_ref[:, g * dv:(g + 1) * dv] = out[g * tq:(g + 1) * tq, :].astype(o_ref.dtype)


def _with_ones_row(vt):
    extra = jnp.zeros(vt.shape[:-2] + (BF16_SUBLANES, vt.shape[-1]), vt.dtype).at[..., 0, :].set(1.0)
    return jnp.concatenate([vt, extra], axis=-2)


def _key_chunk(length):
    for tk in (768, 512, 256, 128):
        if length % tk == 0:
            return tk
    return length


def _flash_attention(q, k, v, *, kv_heads, group, dqk, dv, tq):
    b, lq, _ = q.shape
    kv_len = k.shape[1]
    tq = _tile(lq, tq)
    tk = _key_chunk(kv_len)
    n = kv_len // tk
    rows = group * tq
    vt = _with_ones_row(v.reshape(b, n, tk, kv_heads, dv).transpose(0, 3, 1, 4, 2))
    dve = dv + BF16_SUBLANES
    nbuf = min(FLASH_BUFFERS, n)
    stage_bufs = ([pltpu.VMEM((tk, rows), F32)] * nbuf
                  + [pltpu.VMEM((1, rows), F32)] * nbuf
                  + [pltpu.VMEM((tk, rows), BF16)] * nbuf
                  + [pltpu.VMEM((1, rows), F32)] * nbuf)
    return pl.pallas_call(
        functools.partial(_flash_body, group=group, dqk=dqk, dv=dv, tq=tq, tk=tk, n=n, nbuf=nbuf),
        grid=(b, kv_heads, lq // tq),
        in_specs=[pl.BlockSpec((None, tq, group * dqk), lambda bb, h, i: (bb, i, h)),
                  pl.BlockSpec((None, kv_len, dqk), lambda bb, h, i: (bb, 0, h)),
                  pl.BlockSpec((None, None, n, dve, tk), lambda bb, h, i: (bb, h, 0, 0, 0))],
        out_specs=pl.BlockSpec((None, tq, group * dv), lambda bb, h, i: (bb, i, h)),
        out_shape=jax.ShapeDtypeStruct((b, lq, kv_heads * group * dv), BF16),
        scratch_shapes=[pltpu.VMEM((rows, dqk), BF16)] + stage_bufs + [
            pltpu.VMEM((1, rows), F32),
            pltpu.VMEM((dve, rows), F32)],
        compiler_params=_compiler_params(("parallel", "parallel", "parallel")),
        name="flash_attention",
    )(q, k, vt)


def _ffn_body(*refs, d, hidden, chunks, with_mixer_out):
    if with_mixer_out:
        h_ref, o_ref, mod_ref, wo_ref, n2_ref, w13_ref, w2_ref, out_ref = refs
    else:
        h_ref, mod_ref, n2_ref, w13_ref, w2_ref, out_ref = refs
    h = h_ref[...]
    mod = mod_ref[...]
    if with_mixer_out:
        h = h + mod[:, 2 * d:3 * d] * _dot(o_ref[...], wo_ref[...])
    x = _norm_mod(h, n2_ref[...], mod[:, 4 * d:5 * d], mod[:, 3 * d:4 * d]).astype(BF16)
    acc = None
    for c0, cw in chunks:
        a1 = _dot(x, w13_ref[:, c0:c0 + cw])
        a3 = _dot(x, w13_ref[:, hidden + c0:hidden + c0 + cw])
        part = _dot((_silu(a1) * a3).astype(BF16), w2_ref[c0:c0 + cw, :])
        acc = part if acc is None else acc + part
    out_ref[...] = h + mod[:, 5 * d:6 * d] * acc


def _hidden_chunks(hidden, width):
    chunks, c0 = [], 0
    while c0 < hidden:
        cw = min(width, hidden - c0)
        chunks.append((c0, cw))
        c0 += cw
    return tuple(chunks)


def _ffn(h, mod, n2, w13, w2, mixer_out=None, w_o=None):
    b, l, d = h.shape
    hidden = w2.shape[0]
    tm = _tile(l, 512)
    fused = mixer_out is not None
    return _row_call(functools.partial(_ffn_body, d=d, hidden=hidden, chunks=_hidden_chunks(hidden, 1024),
                                       with_mixer_out=fused),
                     name="swiglu", batch=b, length=l, tm=tm, rows=(h, mixer_out) if fused else (h,), vecs=(mod,),
                     consts=((w_o,) if fused else ()) + (n2, w13, w2), outs=[(d, F32)])[0]


def _na_proj_body(*refs, with_q, d):
    it = iter(refs)
    h_ref, mod_ref, n1_ref, w_ref, gq_ref, gk_ref = (next(it) for _ in range(6))
    if with_q:
        q_ref = next(it)
    k_ref, v_ref = next(it), next(it)

    nh = NA_HEADS * NA_HEAD_DIM
    mod = mod_ref[...]
    x = _norm_mod(h_ref[...], n1_ref[...], mod[:, d:2 * d], mod[:, 0:d]).astype(BF16)
    tm = x.shape[0]
    left = lax.broadcasted_iota(jnp.int32, (tm, LANES), 1) < NA_HEAD_DIM

    def pair_norm(t, gain2):
        sq = t * t
        sl = jnp.sum(jnp.where(left, sq, 0.0), axis=-1, keepdims=True)
        sr = jnp.sum(jnp.where(left, 0.0, sq), axis=-1, keepdims=True)
        r = jnp.where(left, lax.rsqrt(sl / NA_HEAD_DIM + EPS), lax.rsqrt(sr / NA_HEAD_DIM + EPS))
        return t * r * gain2

    v_ref[...] = _dot(x, w_ref[:, 2 * nh:]).astype(BF16)
    k = _dot(x, w_ref[:, nh:2 * nh])
    gk = gk_ref[...]
    for hp in range(nh // LANES):
        k_ref[:, hp * LANES:(hp + 1) * LANES] = pair_norm(k[:, hp * LANES:(hp + 1) * LANES], gk).astype(BF16)
    if with_q:
        q = _dot(x, w_ref[:, :nh])
        gq = gq_ref[...] * (NA_HEAD_DIM ** -0.5 * LOG2E)
        for hp in range(nh // LANES):
            q_ref[:, hp * LANES:(hp + 1) * LANES] = pair_norm(q[:, hp * LANES:(hp + 1) * LANES], gq).astype(BF16)


def _na_project(h, mod, n1, w_qkv, q_norm, k_norm, with_q, kv_tail=()):
    b, l, d = h.shape
    nh = NA_HEADS * NA_HEAD_DIM
    outs = [(nh, BF16)] * (3 if with_q else 2)
    gq2 = jnp.concatenate([q_norm, q_norm]).reshape(1, LANES)
    gk2 = jnp.concatenate([k_norm, k_norm]).reshape(1, LANES)
    res = _row_call(functools.partial(_na_proj_body, with_q=with_q, d=d), name="na_proj", batch=b, length=l,
                    tm=_proj_tile(l, kv_tail), rows=(h,), vecs=(mod,), consts=(n1, w_qkv, gq2, gk2), outs=outs,
                    tails=kv_tail)
    return res if with_q else [None] + list(res)


def _na_bias_plan(rows):
    wr = NA_WIN_ROWS
    nsteps = rows // NA_QROWS
    plan = []
    for j in (0, 1, nsteps - 1):
        base = int(np.clip(j * NA_QROWS - wr // 2, 0, rows - NA_KROWS))
        per_a = []
        for a in range(NA_QROWS):
            r = j * NA_QROWS + a
            r0 = int(np.clip(r - wr // 2, 0, rows - wr))
            ok = tuple(bool(r0 <= base + wp < r0 + wr) for wp in range(NA_KROWS))
            per_a.append((base - r + (wr - 1), ok))
        plan.append(tuple(per_a))
    return tuple(plan)


def _na_bias_body(e_ref, t_ref, *, plan):
    w = GRID_W
    left = lax.broadcasted_iota(jnp.int32, (w, LANES), 1) < w
    outside = jnp.full((w, LANES), NEG_BIG, F32)
    for pat, per_a in enumerate(plan):
        for wp in range(NA_KROWS):
            for blk in range(2 * NA_QROWS // 2):
                hd, a = divmod(2 * blk, NA_QROWS)
                halves = []
                for aa in (a, a + 1):
                    start, ok = per_a[aa]
                    halves.append(e_ref[hd, start + wp] if ok[wp] else outside)
                t_ref[pat, wp * w:(wp + 1) * w, blk * LANES:(blk + 1) * LANES] = jnp.where(left, halves[0], halves[1])


def _na_bias_table(rpb, rows):
    wr, wc, w = NA_WIN_ROWS, NA_WIN_COLS, GRID_W
    nh = rpb.shape[0]
    col = np.arange(w)
    qcol = np.arange(LANES) % w
    col_start = np.clip(qcol - wc // 2, 0, w - wc)
    col_ok = (col[:, None] >= col_start[None, :]) & (col[:, None] < col_start[None, :] + wc)
    col_rel = col[:, None] - qcol[None, :] + (wc - 1)
    onehot = (np.arange(2 * wc - 1)[:, None, None] == col_rel[None]).astype(np.float32)
    e = jnp.einsum('hrk,kcl->hrcl', rpb.astype(F32) * LOG2E, onehot, precision=lax.Precision.HIGHEST)
    e = jnp.where(col_ok[None, None], e, NEG_BIG)
    nq2, nk = 2 * NA_QROWS * w, NA_KROWS * w
    return pl.pallas_call(
        functools.partial(_na_bias_body, plan=_na_bias_plan(rows)),
        grid=(nh // 2,),
        in_specs=[pl.BlockSpec((2, 2 * wr - 1, w, LANES), lambda hp: (hp, 0, 0, 0))],
        out_specs=pl.BlockSpec((None, 3, nk, nq2), lambda hp: (hp, 0, 0, 0)),
        out_shape=jax.ShapeDtypeStruct((nh // 2, 3, nk, nq2), F32),
        compiler_params=_compiler_params(("parallel",)),
        name="na_bias_table",
    )(e)


def _na_attn_body(q_ref, k_ref, kc_ref, vt_ref, bias_ref, o_ref, *scratch, rows, steps, nbuf):
    s_bufs, c_bufs, p_bufs = (scratch[k * nbuf:(k + 1) * nbuf] for k in range(3))
    nq = NA_QROWS * GRID_W
    nkw = NA_KROWS * GRID_W
    lc = kc_ref.shape[0]
    blk = vt_ref.shape[-1]
    nsteps = rows // NA_QROWS
    rb = pl.program_id(2)
    left = lax.broadcasted_iota(jnp.int32, (nq, LANES), 1) < NA_HEAD_DIM
    head_masks = (left.astype(F32).astype(BF16), jnp.logical_not(left).astype(F32).astype(BF16))
    top_half = lax.broadcasted_iota(jnp.int32, (LANES, nq), 0) < NA_HEAD_DIM

    def window_base(it):
        j = rb * steps + it
        return j, jnp.clip(j * NA_QROWS - NA_WIN_ROWS // 2, 0, rows - NA_KROWS)

    def scores(it, slot):
        j, base = window_base(it)
        pat = jnp.where(j == 0, 0, jnp.where(j == nsteps - 1, 2, 1))
        koff = pl.multiple_of(base * GRID_W, NA_QROWS * GRID_W)
        q = q_ref[it * nq:(it + 1) * nq, :]
        q2 = jnp.concatenate([q * head_masks[0], q * head_masks[1]], axis=0)
        s_loc = _dot_nt(k_ref[pl.ds(koff, nkw), :], q2) + bias_ref[pat]
        s_ctx = _dot_nt(kc_ref[...], q2)
        s_bufs[slot][0:nkw, :] = s_loc
        s_bufs[slot][nkw:, :] = s_ctx
        c_bufs[slot][...] = jnp.maximum(jnp.max(s_loc, axis=0, keepdims=True), jnp.max(s_ctx, axis=0, keepdims=True))

    def softmax(slot):
        p_bufs[slot][...] = jnp.exp2(s_bufs[slot][...] - c_bufs[slot][...]).astype(BF16)

    def values(it, slot):
        _, base = window_base(it)
        blk0 = base * GRID_W // blk
        p_ref = p_bufs[slot]
        acc = None
        for t in range(nkw // blk):
            part = _dot(vt_ref[blk0 + t], p_ref[t * blk:(t + 1) * blk, :])
            acc = part if acc is None else acc + part
        for t in range(lc // blk):
            acc = acc + _dot(vt_ref[rows * GRID_W // blk + t], p_ref[nkw + t * blk:nkw + (t + 1) * blk, :])
        o = acc[0:LANES, :] * (1.0 / acc[LANES:LANES + 1, :])
        pair = jnp.where(top_half, o[:, :nq], o[:, nq:])
        o_ref[it * nq:(it + 1) * nq, :] = pair.T.astype(o_ref.dtype)

    scores(0, 0)
    for it in range(steps):
        if it + 1 < steps:
            scores(it + 1, (it + 1) % nbuf)
        softmax(it % nbuf)
        if it >= 1:
            values(it - 1, (it - 1) % nbuf)
    values(steps - 1, (steps - 1) % nbuf)


def _na_attention(q, k, v, bias, lc):
    b, s, nh = q.shape
    rows = s // GRID_W
    nsteps = rows // NA_QROWS
    steps = 8 if nsteps % 8 == 0 else nsteps
    tq = steps * NA_QROWS * GRID_W
    nq = NA_QROWS * GRID_W
    nk = NA_KROWS * GRID_W + lc
    blk = NA_QROWS * GRID_W
    nblk = (s + lc) // blk
    vt = _with_ones_row(v.reshape(b, nblk, blk, nh // LANES, LANES).transpose(0, 3, 1, 4, 2))
    nbuf = min(FLASH_BUFFERS, steps)
    stage_bufs = ([pltpu.VMEM((nk, 2 * nq), F32)] * nbuf + [pltpu.VMEM((1, 2 * nq), F32)] * nbuf
                  + [pltpu.VMEM((nk, 2 * nq), BF16)] * nbuf)
    return pl.pallas_call(
        functools.partial(_na_attn_body, rows=rows, steps=steps, nbuf=nbuf),
        grid=(nh // LANES, b, s // tq),
        in_specs=[pl.BlockSpec((None, tq, LANES), lambda hp, bb, i: (bb, i, hp)),
                  pl.BlockSpec((None, s, LANES), lambda hp, bb, i: (bb, 0, hp)),
                  pl.BlockSpec((None, lc, LANES), lambda hp, bb, i: (bb, s // lc, hp)),
                  pl.BlockSpec((None, None, nblk, LANES + BF16_SUBLANES, blk), lambda hp, bb, i: (bb, hp, 0, 0, 0)),
                  pl.BlockSpec((None, 3, NA_KROWS * GRID_W, 2 * nq), lambda hp, bb, i: (hp, 0, 0, 0))],
        out_specs=pl.BlockSpec((None, tq, LANES), lambda hp, bb, i: (bb, i, hp)),
        out_shape=jax.ShapeDtypeStruct((b, s, nh), BF16),
        scratch_shapes=stage_bufs,
        compiler_params=_compiler_params(("parallel", "parallel", "parallel")),
        name="na_attention",
    )(q, k, k, vt, bias)


def _ret_proj_body(*refs, rope, with_qg, d):
    it = iter(refs)
    h_ref = next(it)
    cos_ref = sin_ref = None
    if rope:
        cos_ref, sin_ref = next(it), next(it)
    mod_ref, n1_ref, w_ref = (next(it) for _ in range(3))
    if with_qg:
        q_ref, g_ref = next(it), next(it)
    k_ref, v_ref = next(it), next(it)

    nk, nv, dk = RET_HEADS * RET_QK_DIM, RET_HEADS * RET_V_DIM, RET_QK_DIM
    mod = mod_ref[...]
    x = _norm_mod(h_ref[...], n1_ref[...], mod[:, d:2 * d], mod[:, 0:d]).astype(BF16)
    if rope:
        cos, sin = cos_ref[...], sin_ref[...]

    def rotate_store(t, ref, scale):
        for hh in range(RET_HEADS):
            x1 = t[:, hh * dk:hh * dk + dk // 2] * scale
            x2 = t[:, hh * dk + dk // 2:(hh + 1) * dk] * scale
            if rope:
                x1, x2 = x1 * cos - x2 * sin, x1 * sin + x2 * cos
            ref[:, hh * dk:hh * dk + dk // 2] = x1.astype(BF16)
            ref[:, hh * dk + dk // 2:(hh + 1) * dk] = x2.astype(BF16)

    rotate_store(_dot(x, w_ref[:, nk:2 * nk]), k_ref, dk ** -0.5)
    v_ref[...] = _dot(x, w_ref[:, 2 * nk:2 * nk + nv]).astype(BF16)
    if with_qg:
        rotate_store(_dot(x, w_ref[:, :nk]), q_ref, 1.0)
        g_ref[...] = _dot(x, w_ref[:, 2 * nk + nv:])


def _ret_project(h, mod, n1, w_qkvg, rope_tabs, with_qg, kv_tail=()):
    b, l, d = h.shape
    nk, nv = RET_HEADS * RET_QK_DIM, RET_HEADS * RET_V_DIM
    rope = rope_tabs is not None
    outs = ([(nk, BF16), (nv, F32)] if with_qg else []) + [(nk, BF16), (nv, BF16)]
    res = _row_call(functools.partial(_ret_proj_body, rope=rope, with_qg=with_qg, d=d), name="ret_proj", batch=b,
                    length=l, tm=_proj_tile(l, kv_tail), rows=(h,), pos=tuple(rope_tabs) if rope else (),
                    vecs=(mod,), consts=(n1, w_qkvg), outs=outs, tails=kv_tail)
    return res if with_qg else [None, None] + list(res)


def _ret_scan_body(*refs, emit_y, chunk, nchunks):
    if emit_y:
        ld_ref, q_ref, k_ref, v_ref, s0_ref, y_ref, sfin_ref, st_ref = refs
    else:
        ld_ref, k_ref, v_ref, s0_ref, sfin_ref, st_ref = refs
        q_ref = y_ref = None
    direction, hh, t = pl.program_id(0), pl.program_id(2), pl.program_id(3)
    ld = ld_ref[direction, hh]

    @pl.when(t == 0)
    def _():
        st_ref[...] = s0_ref[...]

    ri = lax.broadcasted_iota(jnp.int32, (chunk, chunk), 0)
    ci = lax.broadcasted_iota(jnp.int32, (chunk, chunk), 1)
    dist = jnp.where(direction == 0, ri - ci, ci - ri).astype(F32)
    intra = jnp.where(dist >= 0, jnp.exp(jnp.maximum(dist, 0.0) * ld), 0.0)
    pos = lax.broadcasted_iota(jnp.int32, (chunk, 1), 0)
    rank = jnp.where(direction == 0, pos, chunk - 1 - pos).astype(F32)
    q_decay = jnp.exp((rank + 1.0) * ld)
    k_decay = jnp.exp((chunk - 1.0 - rank) * ld)
    chunk_decay = jnp.exp(jnp.full((1, 1), float(chunk), F32) * ld)

    for c in range(nchunks):
        cc = jnp.where(direction == 0, c, nchunks - 1 - c)
        off = pl.multiple_of(cc * chunk, chunk)
        k = k_ref[pl.ds(off, chunk), :]
        v = v_ref[pl.ds(off, chunk), :]
        state = st_ref[...]
        if emit_y:
            q = q_ref[pl.ds(off, chunk), :]
            sc = _dot_nt(q, k) * intra
            inner = _dot(sc.astype(BF16), v)
            cross = _dot(q, state.astype(BF16)) * q_decay
            y_ref[pl.ds(off, chunk), :] = inner + cross
        kd = (k.astype(F32) * k_decay).astype(BF16)
        st_ref[...] = state * chunk_decay + _dot_tn(kd, v)

    @pl.when(t == pl.num_programs(3) - 1)
    def _():
        sfin_ref[...] = st_ref[...]


def _ret_scan(log_decay, q, k, v, init_state, emit_y, row0, l):
    b = k.shape[0]
    dk, dv, nh = RET_QK_DIM, RET_V_DIM, RET_HEADS
    chunk = 256 if l % 256 == 0 else 128
    tt = 1024 if l % 1024 == 0 else chunk
    nchunks = tt // chunk
    nt = l // tt
    assert row0 % tt == 0

    def tok(d, t):
        return jnp.where(d == 0, t, nt - 1 - t)

    in_specs = [pl.BlockSpec(memory_space=pltpu.SMEM)]
    args = [log_decay]
    if emit_y:
        in_specs.append(pl.BlockSpec((None, tt, dk), lambda d, bb, h, t: (bb, tok(d, t), h)))
        args.append(q)
    in_specs += [pl.BlockSpec((None, tt, dk), lambda d, bb, h, t: (bb, row0 // tt + tok(d, t), h)),
                 pl.BlockSpec((None, tt, dv), lambda d, bb, h, t: (bb, row0 // tt + tok(d, t), h)),
                 pl.BlockSpec((None, None, None, dk, dv), lambda d, bb, h, t: (d, bb, h, 0, 0))]
    args += [k, v, init_state]
    out_shape, out_specs = [], []
    if emit_y:
        out_shape.append(jax.ShapeDtypeStruct((b, l, 2 * nh * dv), F32))
        out_specs.append(pl.BlockSpec((None, tt, dv), lambda d, bb, h, t: (bb, tok(d, t), d * nh + h)))
    out_shape.append(jax.ShapeDtypeStruct((2, b, nh, dk, dv), F32))
    out_specs.append(pl.BlockSpec((None, None, None, dk, dv), lambda d, bb, h, t: (d, bb, h, 0, 0)))
    res = pl.pallas_call(
        functools.partial(_ret_scan_body, emit_y=emit_y, chunk=chunk, nchunks=nchunks),
        grid=(2, b, nh, nt),
        in_specs=in_specs,
        out_specs=out_specs,
        out_shape=out_shape,
        scratch_shapes=[pltpu.VMEM((dk, dv), F32)],
        compiler_params=_compiler_params(("parallel", "parallel", "parallel", "arbitrary")),
        name="ret_scan" if emit_y else "ret_ctx_state",
    )(*args)
    return (res[0], res[1]) if emit_y else (None, res[0])


def _ret_out_body(h_ref, y_ref, g_ref, mod_ref, gn_ref, w_ref, out_ref, *, d):
    nv, dv = RET_HEADS * RET_V_DIM, RET_V_DIM
    y2 = y_ref[...]
    y = y2[:, :nv] + y2[:, nv:]
    gate = _silu(g_ref[...])
    gn = gn_ref[...]
    acc = None
    for hh in range(RET_HEADS):
        yh = y[:, hh * dv:(hh + 1) * dv]
        mu = jnp.mean(yh, axis=-1, keepdims=True)
        yc = yh - mu
        var = jnp.mean(yc * yc, axis=-1, keepdims=True)
        z = yc * lax.rsqrt(var + EPS) * gn[:, hh * dv:(hh + 1) * dv] * gate[:, hh * dv:(hh + 1) * dv]
        part = _dot(z.astype(BF16), w_ref[hh * dv:(hh + 1) * dv, :])
        acc = part if acc is None else acc + part
    out_ref[...] = h_ref[...] + mod_ref[...][:, 2 * d:3 * d] * acc


def _ret_output(h, y, g, mod, out_norm, w_o):
    b, l, d = h.shape
    tm = _tile(l, 256)
    return _row_call(functools.partial(_ret_out_body, d=d), name="ret_out", batch=b, length=l, tm=tm,
                     rows=(h, y, g), vecs=(mod,), consts=(out_norm.reshape(1, -1), w_o), outs=[(d, F32)])[0]


def kernel(x, c, ctx, c_ctx, mod_w, mod_b, norm1, norm2, ffn_w13, ffn_w2, mla_w_down, mla_q_lora_norm, mla_kv_lora_norm, mla_w_uq, mla_w_ukv, mla_q_norm, mla_k_norm, mla_w_o, gqa_w_qkv, gqa_q_norm, gqa_k_norm, gqa_w_o, na_w_qkv, na_q_norm, na_k_norm, na_rpb, na_w_o, ret_w_qkvg, ret_log_decay_fwd, ret_log_decay_bwd, ret_out_norm, ret_w_o):
    b, s, d = x.shape
    lc = ctx.shape[1]
    depth = mod_w.shape[0]

    pad_rows = -(b + 1) % 16
    c_rows = jnp.concatenate([c, c_ctx[None, :], jnp.zeros((pad_rows, d), F32)], axis=0)
    mods = _modulation(c_rows, mod_w, mod_b)

    h, hc = x, ctx
    for i in range(depth):
        need_ctx = i < depth - 1
        mod_l = mods[i, :b].reshape(b, 1, 6 * d)
        mod_c = mods[i, b:b + 1].reshape(1, 1, 6 * d)
        n1 = norm1[i].reshape(1, d)
        n2 = norm2[i].reshape(1, d)
        kind, j = i % 4, i // 4
        if kind == 0:
            prep = _mla_prepare(mla_w_down[j], mla_q_lora_norm[j], mla_kv_lora_norm[j], mla_w_uq[j], mla_w_ukv[j],
                                mla_q_norm[j], mla_k_norm[j])
            tabs = _rope_tables_rolled(s, MLA_ROPE, LANES // 2)
            qc, kc, vc = _mla_project(hc, mod_c, n1, prep, None, need_ctx)
            ql, k_all, v_all = _mla_project(h, mod_l, n1, prep, tabs, True, (kc, vc))
            attn = functools.partial(_flash_attention, kv_heads=MLA_HEADS, group=1, dqk=MLA_QK_PAD, dv=MLA_V, tq=512)
            w_o = mla_w_o[j].astype(BF16)
        elif kind == 1:
            w_qkv = gqa_w_qkv[j].astype(BF16)
            tabs = _rope_tables_rolled(s, GQA_HEAD_DIM, GQA_HEAD_DIM // 2)
            qc, kc, vc = _gqa_project(hc, mod_c, n1, w_qkv, gqa_q_norm[j], gqa_k_norm[j], None, need_ctx)
            ql, k_all, v_all = _gqa_project(h, mod_l, n1, w_qkv, gqa_q_norm[j], gqa_k_norm[j], tabs, True, (kc, vc))
            group = GQA_HEADS // GQA_KV_HEADS
            attn = functools.partial(_flash_attention, kv_heads=GQA_KV_HEADS, group=group, dqk=GQA_HEAD_DIM,
                                     dv=GQA_HEAD_DIM, tq=512 // group)
            w_o = gqa_w_o[j].astype(BF16)
        elif kind == 2:
            w_qkv = na_w_qkv[j].astype(BF16)
            qc, kc, vc = _na_project(hc, mod_c, n1, w_qkv, na_q_norm[j], na_k_norm[j], need_ctx)
            ql, k_all, v_all = _na_project(h, mod_l, n1, w_qkv, na_q_norm[j], na_k_norm[j], True, (kc, vc))
            w_o = na_w_o[j].astype(BF16)
        else:
            w_qkvg = ret_w_qkvg[j].astype(BF16)
            tabs = _rope_cos_sin(s, RET_QK_DIM)
            qc, gc, kc, vc = _ret_project(hc, mod_c, n1, w_qkvg, None, need_ctx)
            ql, gl, k_all, v_all = _ret_project(h, mod_l, n1, w_qkvg, tabs, True, (kc, vc))
            w_o = ret_w_o[j].astype(BF16)

        oc = None
        if kind in (0, 1):
            ol = attn(ql, k_all, v_all)
            if need_ctx:
                oc = attn(qc, kc, vc)
        elif kind == 2:
            bias = _na_bias_table(na_rpb[j], s // GRID_W)
            ol = _na_attention(ql, k_all, v_all, bias, lc)
            if need_ctx:
                def heads_major(t):
                    return t.reshape(b, lc, NA_HEADS, NA_HEAD_DIM).transpose(0, 2, 1, 3).reshape(
                        b * NA_HEADS, lc, NA_HEAD_DIM)
                oc = _flash_attention(heads_major(qc), heads_major(kc), heads_major(vc), kv_heads=1, group=1,
                                      dqk=NA_HEAD_DIM, dv=NA_HEAD_DIM, tq=lc)
                oc = oc.reshape(b, NA_HEADS, lc, NA_HEAD_DIM).transpose(0, 2, 1, 3).reshape(b, lc, -1)
        else:
            log_decay = jnp.stack([ret_log_decay_fwd[j], ret_log_decay_bwd[j]]).astype(F32)
            zero_state = jnp.zeros((2, b, RET_HEADS, RET_QK_DIM, RET_V_DIM), F32)
            _, ctx_state = _ret_scan(log_decay, None, kc, vc, zero_state, False, 0, lc)
            y, _ = _ret_scan(log_decay, ql, k_all, v_all, ctx_state, True, 0, s)
            h = _ret_output(h, y, gl, mod_l, ret_out_norm[j], w_o)
            ol = None
            if need_ctx:
                raise NotImplementedError("context update after a retention layer is not needed at this depth")

        w13, w2 = ffn_w13[i].astype(BF16), ffn_w2[i].astype(BF16)
        h = _ffn(h, mod_l, n2, w13, w2, ol, w_o if ol is not None else None)
        if need_ctx:
            hc = _ffn(hc, mod_c, n2, w13, w2, oc, w_o)
    return h
```

```python
import functools

import numpy as np
import jax
import jax.numpy as jnp
from jax import lax
from jax.experimental import pallas as pl
from jax.experimental.pallas import tpu as pltpu

F32 = jnp.float32
BF16 = jnp.bfloat16

GRID_W = 64
ROPE_THETA = 10000.0
EPS = 1e-6
MLA_HEADS, MLA_Q_RANK, MLA_KV_RANK, MLA_NOPE, MLA_ROPE, MLA_V = 8, 384, 256, 128, 64, 128
MLA_QK = MLA_NOPE + MLA_ROPE
MLA_QK_PAD = 256
GQA_HEADS, GQA_KV_HEADS, GQA_HEAD_DIM = 8, 2, 128
NA_HEADS, NA_HEAD_DIM, NA_WIN_ROWS, NA_WIN_COLS = 16, 64, 8, 16
RET_HEADS, RET_QK_DIM, RET_V_DIM = 4, 256, 512

LANES = 128
BF16_SUBLANES = 16
LOG2E = 1.4426950408889634
V7X_VMEM_LIMIT_BYTES = 56 * 1024 * 1024
NEG_BIG = -1e30

FLASH_BUFFERS = 2
FLASH_QUERY_COLUMNS = 1024
FLASH_UNROLL_CHUNKS = 12
NA_QROWS = 4
NA_KROWS = NA_QROWS + NA_WIN_ROWS


def _dot(a, b):
    return jnp.dot(a, b, preferred_element_type=F32)


def _dot_nt(a, b):
    return lax.dot_general(a, b, (((1,), (1,)), ((), ())), preferred_element_type=F32)


def _dot_tn(a, b):
    return lax.dot_general(a, b, (((0,), (0,)), ((), ())), preferred_element_type=F32)


def _silu(x):
    return x / (1.0 + jnp.exp(-x))


def _norm_mod(h, gain, scale, shift):
    y = h * lax.rsqrt(jnp.mean(h * h, axis=-1, keepdims=True) + EPS) * gain
    return y * (1.0 + scale) + shift


def _rms(x, denom):
    return x * lax.rsqrt(jnp.sum(x * x, axis=-1, keepdims=True) / denom + EPS)


def _compiler_params(semantics):
    return pltpu.CompilerParams(dimension_semantics=semantics, vmem_limit_bytes=V7X_VMEM_LIMIT_BYTES)


def _row_call(body, *, name, batch, length, tm, rows=(), pos=(), vecs=(), consts=(), outs=(), tails=()):
    n_main = length // tm
    n_tail = tails[0].shape[1] // tm if tails else 0
    n_in = len(rows) + len(pos) + len(vecs) + len(consts)
    n_plain = len(outs) - len(tails)
    if tails:
        assert all(t.shape[1] == n_tail * tm for t in tails)
        inner = body

        def body(*refs):
            out_refs = refs[n_in + len(tails):]
            i = pl.program_id(1)

            @pl.when(i < n_main)
            def _():
                inner(*refs[:n_in], *out_refs)

            @pl.when(i >= n_main)
            def _():
                for t_ref, o_ref in zip(refs[n_in:n_in + len(tails)], out_refs[n_plain:]):
                    o_ref[...] = t_ref[...]

    def main(i):
        return jnp.minimum(i, n_main - 1) if tails else i

    in_specs = []
    for a in rows:
        in_specs.append(pl.BlockSpec((None, tm, a.shape[-1]), lambda b, i: (b, main(i), 0)))
    for a in pos:
        in_specs.append(pl.BlockSpec((tm, a.shape[-1]), lambda b, i: (main(i), 0)))
    for a in vecs:
        if a.shape[0] == 1:
            in_specs.append(pl.BlockSpec((None, 1, a.shape[-1]), lambda b, i: (0, 0, 0)))
        else:
            in_specs.append(pl.BlockSpec((None, 1, a.shape[-1]), lambda b, i: (b, 0, 0)))
    for a in consts:
        nd = a.ndim
        in_specs.append(pl.BlockSpec(a.shape, lambda b, i, nd=nd: (0,) * nd, pipeline_mode=pl.Buffered(1)))
    for a in tails:
        in_specs.append(pl.BlockSpec((None, tm, a.shape[-1]), lambda b, i: (b, jnp.maximum(i - n_main, 0), 0)))
    out_shape, out_specs = [], []
    for k, (f, dt) in enumerate(outs):
        if k < n_plain:
            out_shape.append(jax.ShapeDtypeStruct((batch, length, f), dt))
            out_specs.append(pl.BlockSpec((None, tm, f), lambda b, i: (b, main(i), 0)))
        else:
            out_shape.append(jax.ShapeDtypeStruct((batch, length + n_tail * tm, f), dt))
            out_specs.append(pl.BlockSpec((None, tm, f), lambda b, i: (b, i, 0)))
    return pl.pallas_call(
        body,
        grid=(batch, n_main + n_tail),
        in_specs=in_specs,
        out_specs=out_specs,
        out_shape=out_shape,
        compiler_params=_compiler_params(("parallel", "arbitrary" if tails else "parallel")),
        name=name,
    )(*rows, *pos, *vecs, *consts, *tails)


def _tile(length, pref, also=None):
    t = pref
    while t >= 128:
        if length % t == 0 and (also is None or also % t == 0):
            return t
        t //= 2
    return length


def _mod_body(c_ref, w_ref, b_ref, o_ref):
    s = _silu(c_ref[...]).astype(BF16)
    o_ref[...] = _dot(s, w_ref[...].astype(BF16)) + b_ref[...]


def _modulation(c_rows, mod_w, mod_b):
    depth, d, n = mod_w.shape
    tn = 1536 if n % 1536 == 0 else n
    return pl.pallas_call(
        _mod_body,
        grid=(depth, n // tn),
        in_specs=[pl.BlockSpec(c_rows.shape, lambda l, j: (0, 0)),
                  pl.BlockSpec((None, d, tn), lambda l, j: (l, 0, j)),
                  pl.BlockSpec((None, 1, tn), lambda l, j: (l, 0, j))],
        out_specs=pl.BlockSpec((None, c_rows.shape[0], tn), lambda l, j: (l, 0, j)),
        out_shape=jax.ShapeDtypeStruct((depth, c_rows.shape[0], n), F32),
        compiler_params=_compiler_params(("parallel", "parallel")),
        name="modulation",
    )(c_rows, mod_w, mod_b.reshape(depth, 1, n))


def _rope_cos_sin(length, dim):
    t = jnp.arange(length)
    row = (t // GRID_W).astype(F32)
    col = (t % GRID_W).astype(F32)
    quarter = dim // 4
    inv_freq = ROPE_THETA ** (-jnp.arange(quarter, dtype=F32) / quarter)
    ang = jnp.concatenate([row[:, None] * inv_freq, col[:, None] * inv_freq], axis=-1)
    return jnp.cos(ang), jnp.sin(ang)


def _rope_tables_rolled(length, dim, lane_half):
    cos, sin = _rope_cos_sin(length, dim)
    pad = jnp.zeros((length, lane_half - dim // 2), F32)
    cos_t = jnp.concatenate([cos, pad, cos, pad], axis=-1)
    sin_t = jnp.concatenate([-sin, pad, sin, pad], axis=-1)
    return cos_t, sin_t


def _rope_rolled(x, cos_t, sin_t):
    return x * cos_t + pltpu.roll(x, x.shape[-1] // 2, 1) * sin_t


def _mla_proj_body(*refs, rope, with_q, d):
    it = iter(refs)
    h_ref = next(it)
    cos_ref = sin_ref = None
    if rope:
        cos_ref, sin_ref = next(it), next(it)
    mod_ref, n1_ref, wd_ref, gql_ref, gkl_ref, wuq_ref, wuk_ref, wuv_ref, gq_ref, gk_ref = (next(it) for _ in range(10))
    if with_q:
        q_ref = next(it)
    k_ref, v_ref = next(it), next(it)

    mod = mod_ref[...]
    x = _norm_mod(h_ref[...], n1_ref[...], mod[:, d:2 * d], mod[:, 0:d]).astype(BF16)
    dl = _dot(x, wd_ref[...])
    ckv = (_rms(dl[:, MLA_Q_RANK:MLA_Q_RANK + MLA_KV_RANK], MLA_KV_RANK) * gkl_ref[...]).astype(BF16)
    kr = dl[:, MLA_Q_RANK + MLA_KV_RANK:]
    kn = _dot(ckv, wuk_ref[...])
    v_ref[...] = _dot(ckv, wuv_ref[...]).astype(BF16)
    if rope:
        cos_t, sin_t = cos_ref[...], sin_ref[...]
    gk = gk_ref[...]
    kr_sq = kr * kr
    kr_g = kr * gk[:, MLA_NOPE:]
    if rope:
        kr_g = _rope_rolled(kr_g, cos_t, sin_t)
    for hh in range(MLA_HEADS):
        kh = kn[:, hh * MLA_NOPE:(hh + 1) * MLA_NOPE]
        r = lax.rsqrt(jnp.sum(kh * kh + kr_sq, axis=-1, keepdims=True) / MLA_QK + EPS)
        k_ref[:, hh * MLA_QK_PAD:hh * MLA_QK_PAD + MLA_NOPE] = (kh * r * gk[:, :MLA_NOPE]).astype(BF16)
        k_ref[:, hh * MLA_QK_PAD + MLA_NOPE:(hh + 1) * MLA_QK_PAD] = (kr_g * r).astype(BF16)
    if with_q:
        cq = (_rms(dl[:, :MLA_Q_RANK], MLA_Q_RANK) * gql_ref[...]).astype(BF16)
        q = _dot(cq, wuq_ref[...])
        gq = gq_ref[...]
        scale = MLA_QK ** -0.5 * LOG2E
        for hh in range(MLA_HEADS):
            qn = q[:, hh * MLA_QK_PAD:hh * MLA_QK_PAD + MLA_NOPE]
            qr = q[:, hh * MLA_QK_PAD + MLA_NOPE:(hh + 1) * MLA_QK_PAD]
            ss = jnp.sum(qn * qn + qr * qr, axis=-1, keepdims=True)
            r = lax.rsqrt(ss / MLA_QK + EPS) * scale
            q_ref[:, hh * MLA_QK_PAD:hh * MLA_QK_PAD + MLA_NOPE] = (qn * r * gq[:, :MLA_NOPE]).astype(BF16)
            qr = qr * r * gq[:, MLA_NOPE:]
            if rope:
                qr = _rope_rolled(qr, cos_t, sin_t)
            q_ref[:, hh * MLA_QK_PAD + MLA_NOPE:(hh + 1) * MLA_QK_PAD] = qr.astype(BF16)


def _pad_rope_cols(r):
    half = MLA_ROPE // 2
    z = jnp.zeros(r.shape[:-1] + (LANES // 2 - half,), r.dtype)
    return jnp.concatenate([r[..., :half], z, r[..., half:], z], axis=-1)


def _mla_prepare(w_down, q_lora_norm, kv_lora_norm, w_uq, w_ukv, q_norm, k_norm):
    d = w_down.shape[0]
    nq = MLA_Q_RANK + MLA_KV_RANK
    wd = jnp.concatenate([w_down[:, :nq], _pad_rope_cols(w_down[:, nq:])], axis=1).astype(BF16)
    wq = w_uq.reshape(MLA_Q_RANK, MLA_HEADS, MLA_QK)
    wq = jnp.concatenate([wq[..., :MLA_NOPE], _pad_rope_cols(wq[..., MLA_NOPE:])], axis=-1)
    wq = wq.reshape(MLA_Q_RANK, MLA_HEADS * MLA_QK_PAD).astype(BF16)
    wkv = w_ukv.reshape(MLA_KV_RANK, MLA_HEADS, MLA_NOPE + MLA_V)
    wuk = wkv[..., :MLA_NOPE].reshape(MLA_KV_RANK, MLA_HEADS * MLA_NOPE).astype(BF16)
    wuv = wkv[..., MLA_NOPE:].reshape(MLA_KV_RANK, MLA_HEADS * MLA_V).astype(BF16)

    def pad_gain(g):
        return jnp.concatenate([g[:MLA_NOPE], _pad_rope_cols(g[MLA_NOPE:])]).reshape(1, MLA_QK_PAD)

    return (wd, q_lora_norm.reshape(1, -1), kv_lora_norm.reshape(1, -1), wq, wuk, wuv,
            pad_gain(q_norm), pad_gain(k_norm))


def _proj_tile(length, kv_tail):
    return _tile(length, 512, kv_tail[0].shape[1] if kv_tail else None)


def _mla_project(h, mod, n1, prep, rope_tabs, with_q, kv_tail=()):
    b, l, d = h.shape
    rope = rope_tabs is not None
    outs = []
    if with_q:
        outs.append((MLA_HEADS * MLA_QK_PAD, BF16))
    outs += [(MLA_HEADS * MLA_QK_PAD, BF16), (MLA_HEADS * MLA_V, BF16)]
    res = _row_call(functools.partial(_mla_proj_body, rope=rope, with_q=with_q, d=d),
                    name="mla_proj", batch=b, length=l, tm=_proj_tile(l, kv_tail), rows=(h,),
                    pos=tuple(rope_tabs) if rope else (), vecs=(mod,), consts=(n1,) + tuple(prep), outs=outs,
                    tails=kv_tail)
    return res if with_q else [None] + list(res)


def _gqa_proj_body(*refs, rope, with_q, d):
    it = iter(refs)
    h_ref = next(it)
    cos_ref = sin_ref = None
    if rope:
        cos_ref, sin_ref = next(it), next(it)
    mod_ref, n1_ref, w_ref, gq_ref, gk_ref = (next(it) for _ in range(5))
    if with_q:
        q_ref = next(it)
    k_ref, v_ref = next(it), next(it)

    dh = GQA_HEAD_DIM
    nq = GQA_HEADS * dh
    nkv = GQA_KV_HEADS * dh
    mod = mod_ref[...]
    x = _norm_mod(h_ref[...], n1_ref[...], mod[:, d:2 * d], mod[:, 0:d]).astype(BF16)
    if rope:
        cos_t, sin_t = cos_ref[...], sin_ref[...]
    kv = _dot(x, w_ref[:, nq:])
    v_ref[...] = kv[:, nkv:].astype(BF16)
    for hh in range(GQA_KV_HEADS):
        kh = _rms(kv[:, hh * dh:(hh + 1) * dh], dh) * gk_ref[...]
        if rope:
            kh = _rope_rolled(kh, cos_t, sin_t)
        k_ref[:, hh * dh:(hh + 1) * dh] = kh.astype(BF16)
    if with_q:
        q = _dot(x, w_ref[:, :nq])
        gq = gq_ref[...] * (dh ** -0.5 * LOG2E)
        for hh in range(GQA_HEADS):
            qh = _rms(q[:, hh * dh:(hh + 1) * dh], dh) * gq
            if rope:
                qh = _rope_rolled(qh, cos_t, sin_t)
            q_ref[:, hh * dh:(hh + 1) * dh] = qh.astype(BF16)


def _gqa_project(h, mod, n1, w_qkv, q_norm, k_norm, rope_tabs, with_q, kv_tail=()):
    b, l, d = h.shape
    rope = rope_tabs is not None
    outs = []
    if with_q:
        outs.append((GQA_HEADS * GQA_HEAD_DIM, BF16))
    outs += [(GQA_KV_HEADS * GQA_HEAD_DIM, BF16)] * 2
    res = _row_call(functools.partial(_gqa_proj_body, rope=rope, with_q=with_q, d=d),
                    name="gqa_proj", batch=b, length=l, tm=_proj_tile(l, kv_tail), rows=(h,),
                    pos=tuple(rope_tabs) if rope else (), vecs=(mod,),
                    consts=(n1, w_qkv, q_norm.reshape(1, -1), k_norm.reshape(1, -1)), outs=outs, tails=kv_tail)
    return res if with_q else [None] + list(res)


def _flash_body(q_ref, k_ref, vt_ref, o_ref, qs_ref, *scratch, group, dqk, dv, tq, tk, n, nbuf):
    s_bufs, c_bufs, p_bufs, a_bufs = (scratch[k * nbuf:(k + 1) * nbuf] for k in range(4))
    m_ref, acc_ref = scratch[4 * nbuf:]
    for g in range(group):
        qs_ref[g * tq:(g + 1) * tq, :] = q_ref[:, g * dqk:(g + 1) * dqk]
    m_ref[...] = jnp.full(m_ref.shape, NEG_BIG, F32)
    acc_ref[...] = jnp.zeros(acc_ref.shape, F32)

    def scores(j, slot):
        rows = pl.ds(pl.multiple_of(j * tk, tk), tk)
        s = _dot_nt(k_ref[rows, :], qs_ref[...])
        s_bufs[slot][...] = s
        c_bufs[slot][...] = jnp.max(s, axis=0, keepdims=True)

    def softmax(slot):
        m_prev = m_ref[...]
        m_new = jnp.maximum(m_prev, c_bufs[slot][...])
        a_bufs[slot][...] = jnp.exp2(m_prev - m_new)
        m_ref[...] = m_new
        p_bufs[slot][...] = jnp.exp2(s_bufs[slot][...] - m_new).astype(BF16)

    def weighted_values(j, slot):
        acc_ref[...] = a_bufs[slot][...] * acc_ref[...] + _dot(vt_ref[j], p_bufs[slot][...])

    def steady(j, r):
        scores(j + 1, (r + 1) % nbuf)
        softmax(r)
        weighted_values(j - 1, (r - 1) % nbuf)

    scores(0, 0)
    if n > 1:
        scores(1, 1 % nbuf)
        softmax(0)
        n_rounds = 0 if n <= FLASH_UNROLL_CHUNKS else (n - 2) // nbuf

        def one_round(t, carry):
            for u in range(nbuf):
                steady(1 + nbuf * t + u, (1 + u) % nbuf)
            return carry

        if n_rounds > 0:
            lax.fori_loop(0, n_rounds, one_round, 0)
        for j in range(1 + nbuf * n_rounds, n - 1):
            steady(j, j % nbuf)
        softmax((n - 1) % nbuf)
        weighted_values(n - 2, (n - 2) % nbuf)
    else:
        softmax(0)
    weighted_values(n - 1, (n - 1) % nbuf)

    out = (acc_ref[0:dv, :] * (1.0 / acc_ref[dv:dv + 1, :])).T
    for g in range(group):
        o_ref[:, g * dv:(g + 1) * dv] = out[g * tq:(g + 1) * tq, :].astype(o_ref.dtype)


def _values_t_body(v_ref, vt_ref, *, n, tk, dv):
    extra = (lax.broadcasted_iota(jnp.int32, (BF16_SUBLANES, tk), 0) == 0).astype(F32).astype(vt_ref.dtype)
    for c in range(n):
        vt_ref[c, 0:dv, :] = v_ref[c * tk:(c + 1) * tk, :].astype(F32).T.astype(vt_ref.dtype)
        vt_ref[c, dv:, :] = extra


def _transposed_values(v, heads, dv, tk):
    b, l, _ = v.shape
    n = l // tk
    return pl.pallas_call(
        functools.partial(_values_t_body, n=n, tk=tk, dv=dv),
        grid=(b, heads),
        in_specs=[pl.BlockSpec((None, l, dv), lambda bb, h: (bb, 0, h))],
        out_specs=pl.BlockSpec((None, None, n, dv + BF16_SUBLANES, tk), lambda bb, h: (bb, h, 0, 0, 0)),
        out_shape=jax.ShapeDtypeStruct((b, heads, n, dv + BF16_SUBLANES, tk), v.dtype),
        compiler_params=_compiler_params(("parallel", "parallel")),
        name="values_t",
    )(v)


def _key_chunk(length):
    for tk in (768, 512, 256, 128):
        if length % tk == 0:
            return tk
    return length


def _flash_attention(q, k, v, *, kv_heads, group, dqk, dv, tq):
    b, lq, _ = q.shape
    kv_len = k.shape[1]
    tq = _tile(lq, tq)
    tk = _key_chunk(kv_len)
    n = kv_len // tk
    rows = group * tq
    vt = _transposed_values(v, kv_heads, dv, tk)
    dve = dv + BF16_SUBLANES
    nbuf = min(FLASH_BUFFERS, n)
    stage_bufs = ([pltpu.VMEM((tk, rows), F32)] * nbuf
                  + [pltpu.VMEM((1, rows), F32)] * nbuf
                  + [pltpu.VMEM((tk, rows), BF16)] * nbuf
                  + [pltpu.VMEM((1, rows), F32)] * nbuf)
    return pl.pallas_call(
        functools.partial(_flash_body, group=group, dqk=dqk, dv=dv, tq=tq, tk=tk, n=n, nbuf=nbuf),
        grid=(b, kv_heads, lq // tq),
        in_specs=[pl.BlockSpec((None, tq, group * dqk), lambda bb, h, i: (bb, i, h)),
                  pl.BlockSpec((None, kv_len, dqk), lambda bb, h, i: (bb, 0, h)),
                  pl.BlockSpec((None, None, n, dve, tk), lambda bb, h, i: (bb, h, 0, 0, 0))],
        out_specs=pl.BlockSpec((None, tq, group * dv), lambda bb, h, i: (bb, i, h)),
        out_shape=jax.ShapeDtypeStruct((b, lq, kv_heads * group * dv), BF16),
        scratch_shapes=[pltpu.VMEM((rows, dqk), BF16)] + stage_bufs + [
            pltpu.VMEM((1, rows), F32),
            pltpu.VMEM((dve, rows), F32)],
        compiler_params=_compiler_params(("parallel", "parallel", "parallel")),
        name="flash_attention",
    )(q, k, vt)


def _ffn_body(*refs, d, hidden, chunks, with_mixer_out):
    if with_mixer_out:
        h_ref, o_ref, mod_ref, wo_ref, n2_ref, w13_ref, w2_ref, out_ref = refs
    else:
        h_ref, mod_ref, n2_ref, w13_ref, w2_ref, out_ref = refs
    h = h_ref[...]
    mod = mod_ref[...]
    if with_mixer_out:
        h = h + mod[:, 2 * d:3 * d] * _dot(o_ref[...], wo_ref[...])
    x = _norm_mod(h, n2_ref[...], mod[:, 4 * d:5 * d], mod[:, 3 * d:4 * d]).astype(BF16)
    acc = None
    for c0, cw in chunks:
        a1 = _dot(x, w13_ref[:, c0:c0 + cw])
        a3 = _dot(x, w13_ref[:, hidden + c0:hidden + c0 + cw])
        part = _dot((_silu(a1) * a3).astype(BF16), w2_ref[c0:c0 + cw, :])
        acc = part if acc is None else acc + part
    out_ref[...] = h + mod[:, 5 * d:6 * d] * acc


def _hidden_chunks(hidden, width):
    chunks, c0 = [], 0
    while c0 < hidden:
        cw = min(width, hidden - c0)
        chunks.append((c0, cw))
        c0 += cw
    return tuple(chunks)


def _ffn(h, mod, n2, w13, w2, mixer_out=None, w_o=None):
    b, l, d = h.shape
    hidden = w2.shape[0]
    tm = _tile(l, 512)
    fused = mixer_out is not None
    return _row_call(functools.partial(_ffn_body, d=d, hidden=hidden, chunks=_hidden_chunks(hidden, 1024),
                                       with_mixer_out=fused),
                     name="swiglu", batch=b, length=l, tm=tm, rows=(h, mixer_out) if fused else (h,), vecs=(mod,),
                     consts=((w_o,) if fused else ()) + (n2, w13, w2), outs=[(d, F32)])[0]


def _na_proj_body(*refs, with_q, d):
    it = iter(refs)
    h_ref, mod_ref, n1_ref, w_ref, gq_ref, gk_ref = (next(it) for _ in range(6))
    if with_q:
        q_ref = next(it)
    k_ref, v_ref = next(it), next(it)

    nh = NA_HEADS * NA_HEAD_DIM
    mod = mod_ref[...]
    x = _norm_mod(h_ref[...], n1_ref[...], mod[:, d:2 * d], mod[:, 0:d]).astype(BF16)
    tm = x.shape[0]
    left = lax.broadcasted_iota(jnp.int32, (tm, LANES), 1) < NA_HEAD_DIM

    def pair_norm(t, gain2):
        sq = t * t
        sl = jnp.sum(jnp.where(left, sq, 0.0), axis=-1, keepdims=True)
        sr = jnp.sum(jnp.where(left, 0.0, sq), axis=-1, keepdims=True)
        r = jnp.where(left, lax.rsqrt(sl / NA_HEAD_DIM + EPS), lax.rsqrt(sr / NA_HEAD_DIM + EPS))
        return t * r * gain2

    v_ref[...] = _dot(x, w_ref[:, 2 * nh:]).astype(BF16)
    k = _dot(x, w_ref[:, nh:2 * nh])
    gk = gk_ref[...]
    for hp in range(nh // LANES):
        k_ref[:, hp * LANES:(hp + 1) * LANES] = pair_norm(k[:, hp * LANES:(hp + 1) * LANES], gk).astype(BF16)
    if with_q:
        q = _dot(x, w_ref[:, :nh])
        gq = gq_ref[...] * (NA_HEAD_DIM ** -0.5 * LOG2E)
        for hp in range(nh // LANES):
            q_ref[:, hp * LANES:(hp + 1) * LANES] = pair_norm(q[:, hp * LANES:(hp + 1) * LANES], gq).astype(BF16)


def _na_project(h, mod, n1, w_qkv, q_norm, k_norm, with_q, kv_tail=()):
    b, l, d = h.shape
    nh = NA_HEADS * NA_HEAD_DIM
    outs = [(nh, BF16)] * (3 if with_q else 2)
    gq2 = jnp.concatenate([q_norm, q_norm]).reshape(1, LANES)
    gk2 = jnp.concatenate([k_norm, k_norm]).reshape(1, LANES)
    res = _row_call(functools.partial(_na_proj_body, with_q=with_q, d=d), name="na_proj", batch=b, length=l,
                    tm=_proj_tile(l, kv_tail), rows=(h,), vecs=(mod,), consts=(n1, w_qkv, gq2, gk2), outs=outs,
                    tails=kv_tail)
    return res if with_q else [None] + list(res)


def _na_bias_plan(rows):
    wr = NA_WIN_ROWS
    nsteps = rows // NA_QROWS
    plan = []
    for j in (0, 1, nsteps - 1):
        base = int(np.clip(j * NA_QROWS - wr // 2, 0, rows - NA_KROWS))
        per_a = []
        for a in range(NA_QROWS):
            r = j * NA_QROWS + a
            r0 = int(np.clip(r - wr // 2, 0, rows - wr))
            ok = tuple(bool(r0 <= base + wp < r0 + wr) for wp in range(NA_KROWS))
            per_a.append((base - r + (wr - 1), ok))
        plan.append(tuple(per_a))
    return tuple(plan)


def _na_bias_body(e_ref, t_ref, *, plan):
    w = GRID_W
    left = lax.broadcasted_iota(jnp.int32, (w, LANES), 1) < w
    outside = jnp.full((w, LANES), NEG_BIG, F32)
    for pat, per_a in enumerate(plan):
        for wp in range(NA_KROWS):
            for blk in range(2 * NA_QROWS // 2):
                hd, a = divmod(2 * blk, NA_QROWS)
                halves = []
                for aa in (a, a + 1):
                    start, ok = per_a[aa]
                    halves.append(e_ref[hd, start + wp] if ok[wp] else outside)
                t_ref[pat, wp * w:(wp + 1) * w, blk * LANES:(blk + 1) * LANES] = jnp.where(left, halves[0], halves[1])


def _na_bias_table(rpb, rows):
    wr, wc, w = NA_WIN_ROWS, NA_WIN_COLS, GRID_W
    nh = rpb.shape[0]
    col = np.arange(w)
    qcol = np.arange(LANES) % w
    col_start = np.clip(qcol - wc // 2, 0, w - wc)
    col_ok = (col[:, None] >= col_start[None, :]) & (col[:, None] < col_start[None, :] + wc)
    col_rel = col[:, None] - qcol[None, :] + (wc - 1)
    onehot = (np.arange(2 * wc - 1)[:, None, None] == col_rel[None]).astype(np.float32)
    e = jnp.einsum('hrk,kcl->hrcl', rpb.astype(F32) * LOG2E, onehot, precision=lax.Precision.HIGHEST)
    e = jnp.where(col_ok[None, None], e, NEG_BIG)
    nq2, nk = 2 * NA_QROWS * w, NA_KROWS * w
    return pl.pallas_call(
        functools.partial(_na_bias_body, plan=_na_bias_plan(rows)),
        grid=(nh // 2,),
        in_specs=[pl.BlockSpec((2, 2 * wr - 1, w, LANES), lambda hp: (hp, 0, 0, 0))],
        out_specs=pl.BlockSpec((None, 3, nk, nq2), lambda hp: (hp, 0, 0, 0)),
        out_shape=jax.ShapeDtypeStruct((nh // 2, 3, nk, nq2), F32),
        compiler_params=_compiler_params(("parallel",)),
        name="na_bias_table",
    )(e)


def _na_attn_body(q_ref, k_ref, kc_ref, vt_ref, bias_ref, o_ref, *scratch, rows, steps, nbuf):
    s_bufs, c_bufs, p_bufs = (scratch[k * nbuf:(k + 1) * nbuf] for k in range(3))
    nq = NA_QROWS * GRID_W
    nkw = NA_KROWS * GRID_W
    lc = kc_ref.shape[0]
    blk = vt_ref.shape[-1]
    nsteps = rows // NA_QROWS
    rb = pl.program_id(2)
    left = lax.broadcasted_iota(jnp.int32, (nq, LANES), 1) < NA_HEAD_DIM
    head_masks = (left.astype(F32).astype(BF16), jnp.logical_not(left).astype(F32).astype(BF16))
    top_half = lax.broadcasted_iota(jnp.int32, (LANES, nq), 0) < NA_HEAD_DIM

    def window_base(it):
        j = rb * steps + it
        return j, jnp.clip(j * NA_QROWS - NA_WIN_ROWS // 2, 0, rows - NA_KROWS)

    def scores(it, slot):
        j, base = window_base(it)
        pat = jnp.where(j == 0, 0, jnp.where(j == nsteps - 1, 2, 1))
        koff = pl.multiple_of(base * GRID_W, NA_QROWS * GRID_W)
        q = q_ref[it * nq:(it + 1) * nq, :]
        q2 = jnp.concatenate([q * head_masks[0], q * head_masks[1]], axis=0)
        s_loc = _dot_nt(k_ref[pl.ds(koff, nkw), :], q2) + bias_ref[pat]
        s_ctx = _dot_nt(kc_ref[...], q2)
        s_bufs[slot][0:nkw, :] = s_loc
        s_bufs[slot][nkw:, :] = s_ctx
        c_bufs[slot][...] = jnp.maximum(jnp.max(s_loc, axis=0, keepdims=True), jnp.max(s_ctx, axis=0, keepdims=True))

    def softmax(slot):
        p_bufs[slot][...] = jnp.exp2(s_bufs[slot][...] - c_bufs[slot][...]).astype(BF16)

    def values(it, slot):
        _, base = window_base(it)
        blk0 = base * GRID_W // blk
        p_ref = p_bufs[slot]
        acc = None
        for t in range(nkw // blk):
            part = _dot(vt_ref[blk0 + t], p_ref[t * blk:(t + 1) * blk, :])
            acc = part if acc is None else acc + part
        for t in range(lc // blk):
            acc = acc + _dot(vt_ref[rows * GRID_W // blk + t], p_ref[nkw + t * blk:nkw + (t + 1) * blk, :])
        o = acc[0:LANES, :] * (1.0 / acc[LANES:LANES + 1, :])
        pair = jnp.where(top_half, o[:, :nq], o[:, nq:])
        o_ref[it * nq:(it + 1) * nq, :] = pair.T.astype(o_ref.dtype)

    scores(0, 0)
    for it in range(steps):
        if it + 1 < steps:
            scores(it + 1, (it + 1) % nbuf)
        softmax(it % nbuf)
        if it >= 1:
            values(it - 1, (it - 1) % nbuf)
    values(steps - 1, (steps - 1) % nbuf)


def _na_attention(q, k, v, bias, lc):
    b, s, nh = q.shape
    rows = s // GRID_W
    nsteps = rows // NA_QROWS
    steps = 8 if nsteps % 8 == 0 else nsteps
    tq = steps * NA_QROWS * GRID_W
    nq = NA_QROWS * GRID_W
    nk = NA_KROWS * GRID_W + lc
    blk = NA_QROWS * GRID_W
    nblk = (s + lc) // blk
    vt = _transposed_values(v, nh // LANES, LANES, blk)
    nbuf = min(FLASH_BUFFERS, steps)
    stage_bufs = ([pltpu.VMEM((nk, 2 * nq), F32)] * nbuf + [pltpu.VMEM((1, 2 * nq), F32)] * nbuf
                  + [pltpu.VMEM((nk, 2 * nq), BF16)] * nbuf)
    return pl.pallas_call(
        functools.partial(_na_attn_body, rows=rows, steps=steps, nbuf=nbuf),
        grid=(nh // LANES, b, s // tq),
        in_specs=[pl.BlockSpec((None, tq, LANES), lambda hp, bb, i: (bb, i, hp)),
                  pl.BlockSpec((None, s, LANES), lambda hp, bb, i: (bb, 0, hp)),
                  pl.BlockSpec((None, lc, LANES), lambda hp, bb, i: (bb, s // lc, hp)),
                  pl.BlockSpec((None, None, nblk, LANES + BF16_SUBLANES, blk), lambda hp, bb, i: (bb, hp, 0, 0, 0)),
                  pl.BlockSpec((None, 3, NA_KROWS * GRID_W, 2 * nq), lambda hp, bb, i: (hp, 0, 0, 0))],
        out_specs=pl.BlockSpec((None, tq, LANES), lambda hp, bb, i: (bb, i, hp)),
        out_shape=jax.ShapeDtypeStruct((b, s, nh), BF16),
        scratch_shapes=stage_bufs,
        compiler_params=_compiler_params(("parallel", "parallel", "parallel")),
        name="na_attention",
    )(q, k, k, vt, bias)


def _ret_proj_body(*refs, rope, with_qg, d):
    it = iter(refs)
    h_ref = next(it)
    cos_ref = sin_ref = None
    if rope:
        cos_ref, sin_ref = next(it), next(it)
    mod_ref, n1_ref, w_ref = (next(it) for _ in range(3))
    if with_qg:
        q_ref, g_ref = next(it), next(it)
    k_ref, v_ref = next(it), next(it)

    nk, nv, dk = RET_HEADS * RET_QK_DIM, RET_HEADS * RET_V_DIM, RET_QK_DIM
    mod = mod_ref[...]
    x = _norm_mod(h_ref[...], n1_ref[...], mod[:, d:2 * d], mod[:, 0:d]).astype(BF16)
    if rope:
        cos, sin = cos_ref[...], sin_ref[...]

    def rotate_store(t, ref, scale):
        for hh in range(RET_HEADS):
            x1 = t[:, hh * dk:hh * dk + dk // 2] * scale
            x2 = t[:, hh * dk + dk // 2:(hh + 1) * dk] * scale
            if rope:
                x1, x2 = x1 * cos - x2 * sin, x1 * sin + x2 * cos
            ref[:, hh * dk:hh * dk + dk // 2] = x1.astype(BF16)
            ref[:, hh * dk + dk // 2:(hh + 1) * dk] = x2.astype(BF16)

    rotate_store(_dot(x, w_ref[:, nk:2 * nk]), k_ref, dk ** -0.5)
    v_ref[...] = _dot(x, w_ref[:, 2 * nk:2 * nk + nv]).astype(BF16)
    if with_qg:
        rotate_store(_dot(x, w_ref[:, :nk]), q_ref, 1.0)
        g_ref[...] = _dot(x, w_ref[:, 2 * nk + nv:])


def _ret_project(h, mod, n1, w_qkvg, rope_tabs, with_qg, kv_tail=()):
    b, l, d = h.shape
    nk, nv = RET_HEADS * RET_QK_DIM, RET_HEADS * RET_V_DIM
    rope = rope_tabs is not None
    outs = ([(nk, BF16), (nv, F32)] if with_qg else []) + [(nk, BF16), (nv, BF16)]
    res = _row_call(functools.partial(_ret_proj_body, rope=rope, with_qg=with_qg, d=d), name="ret_proj", batch=b,
                    length=l, tm=_proj_tile(l, kv_tail), rows=(h,), pos=tuple(rope_tabs) if rope else (),
                    vecs=(mod,), consts=(n1, w_qkvg), outs=outs, tails=kv_tail)
    return res if with_qg else [None, None] + list(res)


def _ret_scan_body(*refs, emit_y, chunk, nchunks):
    if emit_y:
        ld_ref, q_ref, k_ref, v_ref, s0_ref, y_ref, sfin_ref, st_ref = refs
    else:
        ld_ref, k_ref, v_ref, s0_ref, sfin_ref, st_ref = refs
        q_ref = y_ref = None
    direction, hh, t = pl.program_id(0), pl.program_id(2), pl.program_id(3)
    ld = ld_ref[direction, hh]

    @pl.when(t == 0)
    def _():
        st_ref[...] = s0_ref[...]

    ri = lax.broadcasted_iota(jnp.int32, (chunk, chunk), 0)
    ci = lax.broadcasted_iota(jnp.int32, (chunk, chunk), 1)
    dist = jnp.where(direction == 0, ri - ci, ci - ri).astype(F32)
    intra = jnp.where(dist >= 0, jnp.exp(jnp.maximum(dist, 0.0) * ld), 0.0)
    pos = lax.broadcasted_iota(jnp.int32, (chunk, 1), 0)
    rank = jnp.where(direction == 0, pos, chunk - 1 - pos).astype(F32)
    q_decay = jnp.exp((rank + 1.0) * ld)
    k_decay = jnp.exp((chunk - 1.0 - rank) * ld)
    chunk_decay = jnp.exp(jnp.full((1, 1), float(chunk), F32) * ld)

    for c in range(nchunks):
        cc = jnp.where(direction == 0, c, nchunks - 1 - c)
        off = pl.multiple_of(cc * chunk, chunk)
        k = k_ref[pl.ds(off, chunk), :]
        v = v_ref[pl.ds(off, chunk), :]
        state = st_ref[...]
        if emit_y:
            q = q_ref[pl.ds(off, chunk), :]
            sc = _dot_nt(q, k) * intra
            inner = _dot(sc.astype(BF16), v)
            cross = _dot(q, state.astype(BF16)) * q_decay
            y_ref[pl.ds(off, chunk), :] = inner + cross
        kd = (k.astype(F32) * k_decay).astype(BF16)
        st_ref[...] = state * chunk_decay + _dot_tn(kd, v)

    @pl.when(t == pl.num_programs(3) - 1)
    def _():
        sfin_ref[...] = st_ref[...]


def _ret_scan(log_decay, q, k, v, init_state, emit_y, row0, l):
    b = k.shape[0]
    dk, dv, nh = RET_QK_DIM, RET_V_DIM, RET_HEADS
    chunk = 256 if l % 256 == 0 else 128
    tt = 1024 if l % 1024 == 0 else chunk
    nchunks = tt // chunk
    nt = l // tt
    assert row0 % tt == 0

    def tok(d, t):
        return jnp.where(d == 0, t, nt - 1 - t)

    in_specs = [pl.BlockSpec(memory_space=pltpu.SMEM)]
    args = [log_decay]
    if emit_y:
        in_specs.append(pl.BlockSpec((None, tt, dk), lambda d, bb, h, t: (bb, tok(d, t), h)))
        args.append(q)
    in_specs += [pl.BlockSpec((None, tt, dk), lambda d, bb, h, t: (bb, row0 // tt + tok(d, t), h)),
                 pl.BlockSpec((None, tt, dv), lambda d, bb, h, t: (bb, row0 // tt + tok(d, t), h)),
                 pl.BlockSpec((None, None, None, dk, dv), lambda d, bb, h, t: (d, bb, h, 0, 0))]
    args += [k, v, init_state]
    out_shape, out_specs = [], []
    if emit_y:
        out_shape.append(jax.ShapeDtypeStruct((b, l, 2 * nh * dv), F32))
        out_specs.append(pl.BlockSpec((None, tt, dv), lambda d, bb, h, t: (bb, tok(d, t), d * nh + h)))
    out_shape.append(jax.ShapeDtypeStruct((2, b, nh, dk, dv), F32))
    out_specs.append(pl.BlockSpec((None, None, None, dk, dv), lambda d, bb, h, t: (d, bb, h, 0, 0)))
    res = pl.pallas_call(
        functools.partial(_ret_scan_body, emit_y=emit_y, chunk=chunk, nchunks=nchunks),
        grid=(2, b, nh, nt),
        in_specs=in_specs,
        out_specs=out_specs,
        out_shape=out_shape,
        scratch_shapes=[pltpu.VMEM((dk, dv), F32)],
        compiler_params=_compiler_params(("parallel", "parallel", "parallel", "arbitrary")),
        name="ret_scan" if emit_y else "ret_ctx_state",
    )(*args)
    return (res[0], res[1]) if emit_y else (None, res[0])


def _ret_out_body(h_ref, y_ref, g_ref, mod_ref, gn_ref, w_ref, out_ref, *, d):
    nv, dv = RET_HEADS * RET_V_DIM, RET_V_DIM
    y2 = y_ref[...]
    y = y2[:, :nv] + y2[:, nv:]
    gate = _silu(g_ref[...])
    gn = gn_ref[...]
    acc = None
    for hh in range(RET_HEADS):
        yh = y[:, hh * dv:(hh + 1) * dv]
        mu = jnp.mean(yh, axis=-1, keepdims=True)
        yc = yh - mu
        var = jnp.mean(yc * yc, axis=-1, keepdims=True)
        z = yc * lax.rsqrt(var + EPS) * gn[:, hh * dv:(hh + 1) * dv] * gate[:, hh * dv:(hh + 1) * dv]
        part = _dot(z.astype(BF16), w_ref[hh * dv:(hh + 1) * dv, :])
        acc = part if acc is None else acc + part
    out_ref[...] = h_ref[...] + mod_ref[...][:, 2 * d:3 * d] * acc


def _ret_output(h, y, g, mod, out_norm, w_o):
    b, l, d = h.shape
    tm = _tile(l, 256)
    return _row_call(functools.partial(_ret_out_body, d=d), name="ret_out", batch=b, length=l, tm=tm,
                     rows=(h, y, g), vecs=(mod,), consts=(out_norm.reshape(1, -1), w_o), outs=[(d, F32)])[0]


def kernel(x, c, ctx, c_ctx, mod_w, mod_b, norm1, norm2, ffn_w13, ffn_w2, mla_w_down, mla_q_lora_norm, mla_kv_lora_norm, mla_w_uq, mla_w_ukv, mla_q_norm, mla_k_norm, mla_w_o, gqa_w_qkv, gqa_q_norm, gqa_k_norm, gqa_w_o, na_w_qkv, na_q_norm, na_k_norm, na_rpb, na_w_o, ret_w_qkvg, ret_log_decay_fwd, ret_log_decay_bwd, ret_out_norm, ret_w_o):
    b, s, d = x.shape
    lc = ctx.shape[1]
    depth = mod_w.shape[0]

    pad_rows = -(b + 1) % 16
    c_rows = jnp.concatenate([c, c_ctx[None, :], jnp.zeros((pad_rows, d), F32)], axis=0)
    mods = _modulation(c_rows, mod_w, mod_b)

    h, hc = x, ctx
    for i in range(depth):
        need_ctx = i < depth - 1
        mod_l = mods[i, :b].reshape(b, 1, 6 * d)
        mod_c = mods[i, b:b + 1].reshape(1, 1, 6 * d)
        n1 = norm1[i].reshape(1, d)
        n2 = norm2[i].reshape(1, d)
        kind, j = i % 4, i // 4
        if kind == 0:
            prep = _mla_prepare(mla_w_down[j], mla_q_lora_norm[j], mla_kv_lora_norm[j], mla_w_uq[j], mla_w_ukv[j],
                                mla_q_norm[j], mla_k_norm[j])
            tabs = _rope_tables_rolled(s, MLA_ROPE, LANES // 2)
            qc, kc, vc = _mla_project(hc, mod_c, n1, prep, None, need_ctx)
            ql, k_all, v_all = _mla_project(h, mod_l, n1, prep, tabs, True, (kc, vc))
            attn = functools.partial(_flash_attention, kv_heads=MLA_HEADS, group=1, dqk=MLA_QK_PAD, dv=MLA_V,
                                     tq=FLASH_QUERY_COLUMNS)
            w_o = mla_w_o[j].astype(BF16)
        elif kind == 1:
            w_qkv = gqa_w_qkv[j].astype(BF16)
            tabs = _rope_tables_rolled(s, GQA_HEAD_DIM, GQA_HEAD_DIM // 2)
            qc, kc, vc = _gqa_project(hc, mod_c, n1, w_qkv, gqa_q_norm[j], gqa_k_norm[j], None, need_ctx)
            ql, k_all, v_all = _gqa_project(h, mod_l, n1, w_qkv, gqa_q_norm[j], gqa_k_norm[j], tabs, True, (kc, vc))
            group = GQA_HEADS // GQA_KV_HEADS
            attn = functools.partial(_flash_attention, kv_heads=GQA_KV_HEADS, group=group, dqk=GQA_HEAD_DIM,
                                     dv=GQA_HEAD_DIM, tq=FLASH_QUERY_COLUMNS // group)
            w_o = gqa_w_o[j].astype(BF16)
        elif kind == 2:
            w_qkv = na_w_qkv[j].astype(BF16)
            qc, kc, vc = _na_project(hc, mod_c, n1, w_qkv, na_q_norm[j], na_k_norm[j], need_ctx)
            ql, k_all, v_all = _na_project(h, mod_l, n1, w_qkv, na_q_norm[j], na_k_norm[j], True, (kc, vc))
            w_o = na_w_o[j].astype(BF16)
        else:
            w_qkvg = ret_w_qkvg[j].astype(BF16)
            tabs = _rope_cos_sin(s, RET_QK_DIM)
            qc, gc, kc, vc = _ret_project(hc, mod_c, n1, w_qkvg, None, need_ctx)
            ql, gl, k_all, v_all = _ret_project(h, mod_l, n1, w_qkvg, tabs, True, (kc, vc))
            w_o = ret_w_o[j].astype(BF16)

        oc = None
        if kind in (0, 1):
            ol = attn(ql, k_all, v_all)
            if need_ctx:
                oc = attn(qc, kc, vc)
        elif kind == 2:
            bias = _na_bias_table(na_rpb[j], s // GRID_W)
            ol = _na_attention(ql, k_all, v_all, bias, lc)
            if need_ctx:
                def heads_major(t):
                    return t.reshape(b, lc, NA_HEADS, NA_HEAD_DIM).transpose(0, 2, 1, 3).reshape(
                        b * NA_HEADS, lc, NA_HEAD_DIM)
                oc = _flash_attention(heads_major(qc), heads_major(kc), heads_major(vc), kv_heads=1, group=1,
                                      dqk=NA_HEAD_DIM, dv=NA_HEAD_DIM, tq=lc)
                oc = oc.reshape(b, NA_HEADS, lc, NA_HEAD_DIM).transpose(0, 2, 1, 3).reshape(b, lc, -1)
        else:
            log_decay = jnp.stack([ret_log_decay_fwd[j], ret_log_decay_bwd[j]]).astype(F32)
            zero_state = jnp.zeros((2, b, RET_HEADS, RET_QK_DIM, RET_V_DIM), F32)
            _, ctx_state = _ret_scan(log_decay, None, kc, vc, zero_state, False, 0, lc)
            y, _ = _ret_scan(log_decay, ql, k_all, v_all, ctx_state, True, 0, s)
            h = _ret_output(h, y, gl, mod_l, ret_out_norm[j], w_o)
            ol = None
            if need_ctx:
                raise NotImplementedError("context update after a retention layer is not needed at this depth")

        w13, w2 = ffn_w13[i].astype(BF16), ffn_w2[i].astype(BF16)
        h = _ffn(h, mod_l, n2, w13, w2, ol, w_o if ol is not None else None)
        if need_ctx:
            hc = _ffn(hc, mod_c, n2, w13, w2, oc, w_o)
    return h
```

```python
import functools

import numpy as np
import jax
import jax.numpy as jnp
from jax import lax
from jax.experimental import pallas as pl
from jax.experimental.pallas import tpu as pltpu

F32 = jnp.float32
BF16 = jnp.bfloat16

GRID_W = 64
ROPE_THETA = 10000.0
EPS = 1e-6
MLA_HEADS, MLA_Q_RANK, MLA_KV_RANK, MLA_NOPE, MLA_ROPE, MLA_V = 8, 384, 256, 128, 64, 128
MLA_QK = MLA_NOPE + MLA_ROPE
MLA_QK_PAD = 256
GQA_HEADS, GQA_KV_HEADS, GQA_HEAD_DIM = 8, 2, 128
NA_HEADS, NA_HEAD_DIM, NA_WIN_ROWS, NA_WIN_COLS = 16, 64, 8, 16
RET_HEADS, RET_QK_DIM, RET_V_DIM = 4, 256, 512

LANES = 128
BF16_SUBLANES = 16
LOG2E = 1.4426950408889634
V7X_VMEM_LIMIT_BYTES = 56 * 1024 * 1024
NEG_BIG = -1e30

FLASH_BUFFERS = 2
FLASH_QUERY_COLUMNS = 1024
FLASH_UNROLL_CHUNKS = 12
NA_QROWS = 4
NA_KROWS = NA_QROWS + NA_WIN_ROWS


def _dot(a, b):
    return jnp.dot(a, b, preferred_element_type=F32)


def _dot_nt(a, b):
    return lax.dot_general(a, b, (((1,), (1,)), ((), ())), preferred_element_type=F32)


def _dot_tn(a, b):
    return lax.dot_general(a, b, (((0,), (0,)), ((), ())), preferred_element_type=F32)


def _silu(x):
    return x / (1.0 + jnp.exp(-x))


def _norm_mod(h, gain, scale, shift):
    y = h * lax.rsqrt(jnp.mean(h * h, axis=-1, keepdims=True) + EPS) * gain
    return y * (1.0 + scale) + shift


def _rms(x, denom):
    return x * lax.rsqrt(jnp.sum(x * x, axis=-1, keepdims=True) / denom + EPS)


def _compiler_params(semantics):
    return pltpu.CompilerParams(dimension_semantics=semantics, vmem_limit_bytes=V7X_VMEM_LIMIT_BYTES)


def _row_call(body, *, name, batch, length, tm, rows=(), pos=(), vecs=(), consts=(), outs=(), tails=()):
    n_main = length // tm
    n_tail = tails[0].shape[1] // tm if tails else 0
    n_in = len(rows) + len(pos) + len(vecs) + len(consts)
    n_plain = len(outs) - len(tails)
    if tails:
        assert all(t.shape[1] == n_tail * tm for t in tails)
        inner = body

        def body(*refs):
            out_refs = refs[n_in + len(tails):]
            i = pl.program_id(1)

            @pl.when(i < n_main)
            def _():
                inner(*refs[:n_in], *out_refs)

            @pl.when(i >= n_main)
            def _():
                for t_ref, o_ref in zip(refs[n_in:n_in + len(tails)], out_refs[n_plain:]):
                    o_ref[...] = t_ref[...]

    def main(i):
        return jnp.minimum(i, n_main - 1) if tails else i

    in_specs = []
    for a in rows:
        in_specs.append(pl.BlockSpec((None, tm, a.shape[-1]), lambda b, i: (b, main(i), 0)))
    for a in pos:
        in_specs.append(pl.BlockSpec((tm, a.shape[-1]), lambda b, i: (main(i), 0)))
    for a in vecs:
        if a.shape[0] == 1:
            in_specs.append(pl.BlockSpec((None, 1, a.shape[-1]), lambda b, i: (0, 0, 0)))
        else:
            in_specs.append(pl.BlockSpec((None, 1, a.shape[-1]), lambda b, i: (b, 0, 0)))
    for a in consts:
        nd = a.ndim
        in_specs.append(pl.BlockSpec(a.shape, lambda b, i, nd=nd: (0,) * nd, pipeline_mode=pl.Buffered(1)))
    for a in tails:
        in_specs.append(pl.BlockSpec((None, tm, a.shape[-1]), lambda b, i: (b, jnp.maximum(i - n_main, 0), 0)))
    out_shape, out_specs = [], []
    for k, (f, dt) in enumerate(outs):
        if k < n_plain:
            out_shape.append(jax.ShapeDtypeStruct((batch, length, f), dt))
            out_specs.append(pl.BlockSpec((None, tm, f), lambda b, i: (b, main(i), 0)))
        else:
            out_shape.append(jax.ShapeDtypeStruct((batch, length + n_tail * tm, f), dt))
            out_specs.append(pl.BlockSpec((None, tm, f), lambda b, i: (b, i, 0)))
    return pl.pallas_call(
        body,
        grid=(batch, n_main + n_tail),
        in_specs=in_specs,
        out_specs=out_specs,
        out_shape=out_shape,
        compiler_params=_compiler_params(("parallel", "arbitrary" if tails else "parallel")),
        name=name,
    )(*rows, *pos, *vecs, *consts, *tails)


def _tile(length, pref, also=None):
    t = pref
    while t >= 128:
        if length % t == 0 and (also is None or also % t == 0):
            return t
        t //= 2
    return length


def _mod_body(c_ref, w_ref, b_ref, o_ref):
    s = _silu(c_ref[...]).astype(BF16)
    o_ref[...] = _dot(s, w_ref[...].astype(BF16)) + b_ref[...]


def _modulation(c_rows, mod_w, mod_b):
    depth, d, n = mod_w.shape
    tn = 1536 if n % 1536 == 0 else n
    return pl.pallas_call(
        _mod_body,
        grid=(depth, n // tn),
        in_specs=[pl.BlockSpec(c_rows.shape, lambda l, j: (0, 0)),
                  pl.BlockSpec((None, d, tn), lambda l, j: (l, 0, j)),
                  pl.BlockSpec((None, 1, tn), lambda l, j: (l, 0, j))],
        out_specs=pl.BlockSpec((None, c_rows.shape[0], tn), lambda l, j: (l, 0, j)),
        out_shape=jax.ShapeDtypeStruct((depth, c_rows.shape[0], n), F32),
        compiler_params=_compiler_params(("parallel", "parallel")),
        name="modulation",
    )(c_rows, mod_w, mod_b.reshape(depth, 1, n))


def _rope_cos_sin(length, dim):
    t = jnp.arange(length)
    row = (t // GRID_W).astype(F32)
    col = (t % GRID_W).astype(F32)
    quarter = dim // 4
    inv_freq = ROPE_THETA ** (-jnp.arange(quarter, dtype=F32) / quarter)
    ang = jnp.concatenate([row[:, None] * inv_freq, col[:, None] * inv_freq], axis=-1)
    return jnp.cos(ang), jnp.sin(ang)


def _rope_tables_rolled(length, dim, lane_half):
    cos, sin = _rope_cos_sin(length, dim)
    pad = jnp.zeros((length, lane_half - dim // 2), F32)
    cos_t = jnp.concatenate([cos, pad, cos, pad], axis=-1)
    sin_t = jnp.concatenate([-sin, pad, sin, pad], axis=-1)
    return cos_t, sin_t


def _rope_rolled(x, cos_t, sin_t):
    return x * cos_t + pltpu.roll(x, x.shape[-1] // 2, 1) * sin_t


def _mla_proj_body(*refs, rope, with_q, d):
    it = iter(refs)
    h_ref = next(it)
    cos_ref = sin_ref = None
    if rope:
        cos_ref, sin_ref = next(it), next(it)
    mod_ref, n1_ref, wd_ref, gql_ref, gkl_ref, wuq_ref, wuk_ref, wuv_ref, gq_ref, gk_ref = (next(it) for _ in range(10))
    if with_q:
        q_ref = next(it)
    k_ref, v_ref = next(it), next(it)

    mod = mod_ref[...]
    x = _norm_mod(h_ref[...], n1_ref[...], mod[:, d:2 * d], mod[:, 0:d]).astype(BF16)
    dl = _dot(x, wd_ref[...])
    ckv = (_rms(dl[:, MLA_Q_RANK:MLA_Q_RANK + MLA_KV_RANK], MLA_KV_RANK) * gkl_ref[...]).astype(BF16)
    kr = dl[:, MLA_Q_RANK + MLA_KV_RANK:]
    kn = _dot(ckv, wuk_ref[...])
    v_ref[...] = _dot(ckv, wuv_ref[...]).astype(BF16)
    if rope:
        cos_t, sin_t = cos_ref[...], sin_ref[...]
    gk = gk_ref[...]
    kr_sq = kr * kr
    kr_g = kr * gk[:, MLA_NOPE:]
    if rope:
        kr_g = _rope_rolled(kr_g, cos_t, sin_t)
    for hh in range(MLA_HEADS):
        kh = kn[:, hh * MLA_NOPE:(hh + 1) * MLA_NOPE]
        r = lax.rsqrt(jnp.sum(kh * kh + kr_sq, axis=-1, keepdims=True) / MLA_QK + EPS)
        k_ref[:, hh * MLA_QK_PAD:hh * MLA_QK_PAD + MLA_NOPE] = (kh * r * gk[:, :MLA_NOPE]).astype(BF16)
        k_ref[:, hh * MLA_QK_PAD + MLA_NOPE:(hh + 1) * MLA_QK_PAD] = (kr_g * r).astype(BF16)
    if with_q:
        cq = (_rms(dl[:, :MLA_Q_RANK], MLA_Q_RANK) * gql_ref[...]).astype(BF16)
        q = _dot(cq, wuq_ref[...])
        gq = gq_ref[...]
        scale = MLA_QK ** -0.5 * LOG2E
        for hh in range(MLA_HEADS):
            qn = q[:, hh * MLA_QK_PAD:hh * MLA_QK_PAD + MLA_NOPE]
            qr = q[:, hh * MLA_QK_PAD + MLA_NOPE:(hh + 1) * MLA_QK_PAD]
            ss = jnp.sum(qn * qn + qr * qr, axis=-1, keepdims=True)
            r = lax.rsqrt(ss / MLA_QK + EPS) * scale
            q_ref[:, hh * MLA_QK_PAD:hh * MLA_QK_PAD + MLA_NOPE] = (qn * r * gq[:, :MLA_NOPE]).astype(BF16)
            qr = qr * r * gq[:, MLA_NOPE:]
            if rope:
                qr = _rope_rolled(qr, cos_t, sin_t)
            q_ref[:, hh * MLA_QK_PAD + MLA_NOPE:(hh + 1) * MLA_QK_PAD] = qr.astype(BF16)


def _pad_rope_cols(r):
    half = MLA_ROPE // 2
    z = jnp.zeros(r.shape[:-1] + (LANES // 2 - half,), r.dtype)
    return jnp.concatenate([r[..., :half], z, r[..., half:], z], axis=-1)


def _mla_prepare(w_down, q_lora_norm, kv_lora_norm, w_uq, w_ukv, q_norm, k_norm):
    d = w_down.shape[0]
    nq = MLA_Q_RANK + MLA_KV_RANK
    wd = jnp.concatenate([w_down[:, :nq], _pad_rope_cols(w_down[:, nq:])], axis=1).astype(BF16)
    wq = w_uq.reshape(MLA_Q_RANK, MLA_HEADS, MLA_QK)
    wq = jnp.concatenate([wq[..., :MLA_NOPE], _pad_rope_cols(wq[..., MLA_NOPE:])], axis=-1)
    wq = wq.reshape(MLA_Q_RANK, MLA_HEADS * MLA_QK_PAD).astype(BF16)
    wkv = w_ukv.reshape(MLA_KV_RANK, MLA_HEADS, MLA_NOPE + MLA_V)
    wuk = wkv[..., :MLA_NOPE].reshape(MLA_KV_RANK, MLA_HEADS * MLA_NOPE).astype(BF16)
    wuv = wkv[..., MLA_NOPE:].reshape(MLA_KV_RANK, MLA_HEADS * MLA_V).astype(BF16)

    def pad_gain(g):
        return jnp.concatenate([g[:MLA_NOPE], _pad_rope_cols(g[MLA_NOPE:])]).reshape(1, MLA_QK_PAD)

    return (wd, q_lora_norm.reshape(1, -1), kv_lora_norm.reshape(1, -1), wq, wuk, wuv,
            pad_gain(q_norm), pad_gain(k_norm))


def _proj_tile(length, kv_tail):
    return _tile(length, 512, kv_tail[0].shape[1] if kv_tail else None)


def _mla_project(h, mod, n1, prep, rope_tabs, with_q, kv_tail=()):
    b, l, d = h.shape
    rope = rope_tabs is not None
    outs = []
    if with_q:
        outs.append((MLA_HEADS * MLA_QK_PAD, BF16))
    outs += [(MLA_HEADS * MLA_QK_PAD, BF16), (MLA_HEADS * MLA_V, BF16)]
    res = _row_call(functools.partial(_mla_proj_body, rope=rope, with_q=with_q, d=d),
                    name="mla_proj", batch=b, length=l, tm=_proj_tile(l, kv_tail), rows=(h,),
                    pos=tuple(rope_tabs) if rope else (), vecs=(mod,), consts=(n1,) + tuple(prep), outs=outs,
                    tails=kv_tail)
    return res if with_q else [None] + list(res)


def _gqa_proj_body(*refs, rope, with_q, d):
    it = iter(refs)
    h_ref = next(it)
    cos_ref = sin_ref = None
    if rope:
        cos_ref, sin_ref = next(it), next(it)
    mod_ref, n1_ref, w_ref, gq_ref, gk_ref = (next(it) for _ in range(5))
    if with_q:
        q_ref = next(it)
    k_ref, v_ref = next(it), next(it)

    dh = GQA_HEAD_DIM
    nq = GQA_HEADS * dh
    nkv = GQA_KV_HEADS * dh
    mod = mod_ref[...]
    x = _norm_mod(h_ref[...], n1_ref[...], mod[:, d:2 * d], mod[:, 0:d]).astype(BF16)
    if rope:
        cos_t, sin_t = cos_ref[...], sin_ref[...]
    kv = _dot(x, w_ref[:, nq:])
    v_ref[...] = kv[:, nkv:].astype(BF16)
    for hh in range(GQA_KV_HEADS):
        kh = _rms(kv[:, hh * dh:(hh + 1) * dh], dh) * gk_ref[...]
        if rope:
            kh = _rope_rolled(kh, cos_t, sin_t)
        k_ref[:, hh * dh:(hh + 1) * dh] = kh.astype(BF16)
    if with_q:
        q = _dot(x, w_ref[:, :nq])
        gq = gq_ref[...] * (dh ** -0.5 * LOG2E)
        for hh in range(GQA_HEADS):
            qh = _rms(q[:, hh * dh:(hh + 1) * dh], dh) * gq
            if rope:
                qh = _rope_rolled(qh, cos_t, sin_t)
            q_ref[:, hh * dh:(hh + 1) * dh] = qh.astype(BF16)


def _gqa_project(h, mod, n1, w_qkv, q_norm, k_norm, rope_tabs, with_q, kv_tail=()):
    b, l, d = h.shape
    rope = rope_tabs is not None
    outs = []
    if with_q:
        outs.append((GQA_HEADS * GQA_HEAD_DIM, BF16))
    outs += [(GQA_KV_HEADS * GQA_HEAD_DIM, BF16)] * 2
    res = _row_call(functools.partial(_gqa_proj_body, rope=rope, with_q=with_q, d=d),
                    name="gqa_proj", batch=b, length=l, tm=_proj_tile(l, kv_tail), rows=(h,),
                    pos=tuple(rope_tabs) if rope else (), vecs=(mod,),
                    consts=(n1, w_qkv, q_norm.reshape(1, -1), k_norm.reshape(1, -1)), outs=outs, tails=kv_tail)
    return res if with_q else [None] + list(res)


def _flash_body(q_ref, k_ref, vt_ref, o_ref, qs_ref, *scratch, group, dqk, dv, tq, tk, n, nbuf):
    s_bufs, c_bufs, p_bufs, a_bufs = (scratch[k * nbuf:(k + 1) * nbuf] for k in range(4))
    m_ref, acc_ref = scratch[4 * nbuf:]
    for g in range(group):
        qs_ref[g * tq:(g + 1) * tq, :] = q_ref[:, g * dqk:(g + 1) * dqk]
    m_ref[...] = jnp.full(m_ref.shape, NEG_BIG, F32)
    acc_ref[...] = jnp.zeros(acc_ref.shape, F32)

    def scores(j, slot):
        rows = pl.ds(pl.multiple_of(j * tk, tk), tk)
        s = _dot_nt(k_ref[rows, :], qs_ref[...])
        s_bufs[slot][...] = s
        c_bufs[slot][...] = jnp.max(s, axis=0, keepdims=True)

    def softmax(slot):
        m_prev = m_ref[...]
        m_new = jnp.maximum(m_prev, c_bufs[slot][...])
        a_bufs[slot][...] = jnp.exp2(m_prev - m_new)
        m_ref[...] = m_new
        p_bufs[slot][...] = jnp.exp2(s_bufs[slot][...] - m_new).astype(BF16)

    def weighted_values(j, slot):
        acc_ref[...] = a_bufs[slot][...] * acc_ref[...] + _dot(vt_ref[j], p_bufs[slot][...])

    def steady(j, r):
        scores(j + 1, (r + 1) % nbuf)
        softmax(r)
        weighted_values(j - 1, (r - 1) % nbuf)

    scores(0, 0)
    if n > 1:
        scores(1, 1 % nbuf)
        softmax(0)
        n_rounds = 0 if n <= FLASH_UNROLL_CHUNKS else (n - 2) // nbuf

        def one_round(t, carry):
            for u in range(nbuf):
                steady(1 + nbuf * t + u, (1 + u) % nbuf)
            return carry

        if n_rounds > 0:
            lax.fori_loop(0, n_rounds, one_round, 0)
        for j in range(1 + nbuf * n_rounds, n - 1):
            steady(j, j % nbuf)
        softmax((n - 1) % nbuf)
        weighted_values(n - 2, (n - 2) % nbuf)
    else:
        softmax(0)
    weighted_values(n - 1, (n - 1) % nbuf)

    out = (acc_ref[0:dv, :] * (1.0 / acc_ref[dv:dv + 1, :])).T
    for g in range(group):
        o_ref[:, g * dv:(g + 1) * dv] = out[g * tq:(g + 1) * tq, :].astype(o_ref.dtype)


def _values_t_body(v_ref, vt_ref, *, n, tk, dv):
    extra = (lax.broadcasted_iota(jnp.int32, (BF16_SUBLANES, tk), 0) == 0).astype(F32).astype(vt_ref.dtype)
    for c in range(n):
        vt_ref[c, 0:dv, :] = v_ref[c * tk:(c + 1) * tk, :].astype(F32).T.astype(vt_ref.dtype)
        vt_ref[c, dv:, :] = extra


def _transposed_values(v, heads, dv, tk):
    b, l, _ = v.shape
    n = l // tk
    return pl.pallas_call(
        functools.partial(_values_t_body, n=n, tk=tk, dv=dv),
        grid=(b, heads),
        in_specs=[pl.BlockSpec((None, l, dv), lambda bb, h: (bb, 0, h))],
        out_specs=pl.BlockSpec((None, None, n, dv + BF16_SUBLANES, tk), lambda bb, h: (bb, h, 0, 0, 0)),
        out_shape=jax.ShapeDtypeStruct((b, heads, n, dv + BF16_SUBLANES, tk), v.dtype),
        compiler_params=_compiler_params(("parallel", "parallel")),
        name="values_t",
    )(v)


def _key_chunk(length):
    for tk in (768, 512, 256, 128):
        if length % tk == 0:
            return tk
    return length


def _flash_attention(q, k, v, *, kv_heads, group, dqk, dv, tq):
    b, lq, _ = q.shape
    kv_len = k.shape[1]
    tq = _tile(lq, tq)
    tk = _key_chunk(kv_len)
    n = kv_len // tk
    rows = group * tq
    vt = _transposed_values(v, kv_heads, dv, tk)
    dve = dv + BF16_SUBLANES
    nbuf = min(FLASH_BUFFERS, n)
    stage_bufs = ([pltpu.VMEM((tk, rows), F32)] * nbuf
                  + [pltpu.VMEM((1, rows), F32)] * nbuf
                  + [pltpu.VMEM((tk, rows), BF16)] * nbuf
                  + [pltpu.VMEM((1, rows), F32)] * nbuf)
    return pl.pallas_call(
        functools.partial(_flash_body, group=group, dqk=dqk, dv=dv, tq=tq, tk=tk, n=n, nbuf=nbuf),
        grid=(b, kv_heads, lq // tq),
        in_specs=[pl.BlockSpec((None, tq, group * dqk), lambda bb, h, i: (bb, i, h)),
                  pl.BlockSpec((None, kv_len, dqk), lambda bb, h, i: (bb, 0, h)),
                  pl.BlockSpec((None, None, n, dve, tk), lambda bb, h, i: (bb, h, 0, 0, 0))],
        out_specs=pl.BlockSpec((None, tq, group * dv), lambda bb, h, i: (bb, i, h)),
        out_shape=jax.ShapeDtypeStruct((b, lq, kv_heads * group * dv), BF16),
        scratch_shapes=[pltpu.VMEM((rows, dqk), BF16)] + stage_bufs + [
            pltpu.VMEM((1, rows), F32),
            pltpu.VMEM((dve, rows), F32)],
        compiler_params=_compiler_params(("parallel", "parallel", "parallel")),
        name="flash_attention",
    )(q, k, vt)


def _ffn_body(*refs, d, hidden, chunks, with_mixer_out):
    if with_mixer_out:
        h_ref, o_ref, mod_ref, wo_ref, n2_ref, w13_ref, w2_ref, out_ref = refs
    else:
        h_ref, mod_ref, n2_ref, w13_ref, w2_ref, out_ref = refs
    h = h_ref[...]
    mod = mod_ref[...]
    if with_mixer_out:
        h = h + mod[:, 2 * d:3 * d] * _dot(o_ref[...], wo_ref[...])
    x = _norm_mod(h, n2_ref[...], mod[:, 4 * d:5 * d], mod[:, 3 * d:4 * d]).astype(BF16)
    acc = None
    for c0, cw in chunks:
        a1 = _dot(x, w13_ref[:, c0:c0 + cw])
        a3 = _dot(x, w13_ref[:, hidden + c0:hidden + c0 + cw])
        part = _dot((_silu(a1) * a3).astype(BF16), w2_ref[c0:c0 + cw, :])
        acc = part if acc is None else acc + part
    out_ref[...] = h + mod[:, 5 * d:6 * d] * acc


def _hidden_chunks(hidden, width):
    chunks, c0 = [], 0
    while c0 < hidden:
        cw = min(width, hidden - c0)
        chunks.append((c0, cw))
        c0 += cw
    return tuple(chunks)


def _ffn(h, mod, n2, w13, w2, mixer_out=None, w_o=None):
    b, l, d = h.shape
    hidden = w2.shape[0]
    tm = _tile(l, 512)
    fused = mixer_out is not None
    return _row_call(functools.partial(_ffn_body, d=d, hidden=hidden, chunks=_hidden_chunks(hidden, 1024),
                                       with_mixer_out=fused),
                     name="swiglu", batch=b, length=l, tm=tm, rows=(h, mixer_out) if fused else (h,), vecs=(mod,),
                     consts=((w_o,) if fused else ()) + (n2, w13, w2), outs=[(d, F32)])[0]


def _na_proj_body(*refs, with_q, d):
    it = iter(refs)
    h_ref, mod_ref, n1_ref, w_ref, gq_ref, gk_ref = (next(it) for _ in range(6))
    if with_q:
        q_ref = next(it)
    k_ref, v_ref = next(it), next(it)

    nh = NA_HEADS * NA_HEAD_DIM
    mod = mod_ref[...]
    x = _norm_mod(h_ref[...], n1_ref[...], mod[:, d:2 * d], mod[:, 0:d]).astype(BF16)
    tm = x.shape[0]
    left = lax.broadcasted_iota(jnp.int32, (tm, LANES), 1) < NA_HEAD_DIM

    def pair_norm(t, gain2):
        sq = t * t
        sl = jnp.sum(jnp.where(left, sq, 0.0), axis=-1, keepdims=True)
        sr = jnp.sum(jnp.where(left, 0.0, sq), axis=-1, keepdims=True)
        r = jnp.where(left, lax.rsqrt(sl / NA_HEAD_DIM + EPS), lax.rsqrt(sr / NA_HEAD_DIM + EPS))
        return t * r * gain2

    v_ref[...] = _dot(x, w_ref[:, 2 * nh:]).astype(BF16)
    k = _dot(x, w_ref[:, nh:2 * nh])
    gk = gk_ref[...]
    for hp in range(nh // LANES):
        k_ref[:, hp * LANES:(hp + 1) * LANES] = pair_norm(k[:, hp * LANES:(hp + 1) * LANES], gk).astype(BF16)
    if with_q:
        q = _dot(x, w_ref[:, :nh])
        gq = gq_ref[...] * (NA_HEAD_DIM ** -0.5 * LOG2E)
        for hp in range(nh // LANES):
            q_ref[:, hp * LANES:(hp + 1) * LANES] = pair_norm(q[:, hp * LANES:(hp + 1) * LANES], gq).astype(BF16)


def _na_project(h, mod, n1, w_qkv, q_norm, k_norm, with_q, kv_tail=()):
    b, l, d = h.shape
    nh = NA_HEADS * NA_HEAD_DIM
    outs = [(nh, BF16)] * (3 if with_q else 2)
    gq2 = jnp.concatenate([q_norm, q_norm]).reshape(1, LANES)
    gk2 = jnp.concatenate([k_norm, k_norm]).reshape(1, LANES)
    res = _row_call(functools.partial(_na_proj_body, with_q=with_q, d=d), name="na_proj", batch=b, length=l,
                    tm=_proj_tile(l, kv_tail), rows=(h,), vecs=(mod,), consts=(n1, w_qkv, gq2, gk2), outs=outs,
                    tails=kv_tail)
    return res if with_q else [None] + list(res)


def _na_bias_plan(rows):
    wr = NA_WIN_ROWS
    nsteps = rows // NA_QROWS
    plan = []
    for j in (0, 1, nsteps - 1):
        base = int(np.clip(j * NA_QROWS - wr // 2, 0, rows - NA_KROWS))
        per_a = []
        for a in range(NA_QROWS):
            r = j * NA_QROWS + a
            r0 = int(np.clip(r - wr // 2, 0, rows - wr))
            ok = tuple(bool(r0 <= base + wp < r0 + wr) for wp in range(NA_KROWS))
            per_a.append((base - r + (wr - 1), ok))
        plan.append(tuple(per_a))
    return tuple(plan)


def _na_bias_body(e_ref, t_ref, *, plan):
    w = GRID_W
    left = lax.broadcasted_iota(jnp.int32, (w, LANES), 1) < w
    outside = jnp.full((w, LANES), NEG_BIG, F32)
    for pat, per_a in enumerate(plan):
        for wp in range(NA_KROWS):
            for blk in range(2 * NA_QROWS // 2):
                hd, a = divmod(2 * blk, NA_QROWS)
                halves = []
                for aa in (a, a + 1):
                    start, ok = per_a[aa]
                    halves.append(e_ref[hd, start + wp] if ok[wp] else outside)
                t_ref[pat, wp * w:(wp + 1) * w, blk * LANES:(blk + 1) * LANES] = jnp.where(left, halves[0], halves[1])


def _na_bias_table(rpb, rows):
    wr, wc, w = NA_WIN_ROWS, NA_WIN_COLS, GRID_W
    nh = rpb.shape[0]
    col = np.arange(w)
    qcol = np.arange(LANES) % w
    col_start = np.clip(qcol - wc // 2, 0, w - wc)
    col_ok = (col[:, None] >= col_start[None, :]) & (col[:, None] < col_start[None, :] + wc)
    col_rel = col[:, None] - qcol[None, :] + (wc - 1)
    onehot = (np.arange(2 * wc - 1)[:, None, None] == col_rel[None]).astype(np.float32)
    e = jnp.einsum('hrk,kcl->hrcl', rpb.astype(F32) * LOG2E, onehot, precision=lax.Precision.HIGHEST)
    e = jnp.where(col_ok[None, None], e, NEG_BIG)
    nq2, nk = 2 * NA_QROWS * w, NA_KROWS * w
    return pl.pallas_call(
        functools.partial(_na_bias_body, plan=_na_bias_plan(rows)),
        grid=(nh // 2,),
        in_specs=[pl.BlockSpec((2, 2 * wr - 1, w, LANES), lambda hp: (hp, 0, 0, 0))],
        out_specs=pl.BlockSpec((None, 3, nk, nq2), lambda hp: (hp, 0, 0, 0)),
        out_shape=jax.ShapeDtypeStruct((nh // 2, 3, nk, nq2), F32),
        compiler_params=_compiler_params(("parallel",)),
        name="na_bias_table",
    )(e)


def _na_attn_body(q_ref, k_ref, kc_ref, vt_ref, bias_ref, o_ref, *scratch, rows, steps, nbuf):
    s_bufs, c_bufs, p_bufs = (scratch[k * nbuf:(k + 1) * nbuf] for k in range(3))
    nq = NA_QROWS * GRID_W
    nkw = NA_KROWS * GRID_W
    lc = kc_ref.shape[0]
    blk = vt_ref.shape[-1]
    nsteps = rows // NA_QROWS
    rb = pl.program_id(2)
    left = lax.broadcasted_iota(jnp.int32, (nq, LANES), 1) < NA_HEAD_DIM
    head_masks = (left.astype(F32).astype(BF16), jnp.logical_not(left).astype(F32).astype(BF16))
    top_half = lax.broadcasted_iota(jnp.int32, (LANES, nq), 0) < NA_HEAD_DIM

    def window_base(it):
        j = rb * steps + it
        return j, jnp.clip(j * NA_QROWS - NA_WIN_ROWS // 2, 0, rows - NA_KROWS)

    def scores(it, slot):
        j, base = window_base(it)
        pat = jnp.where(j == 0, 0, jnp.where(j == nsteps - 1, 2, 1))
        koff = pl.multiple_of(base * GRID_W, NA_QROWS * GRID_W)
        q = q_ref[it * nq:(it + 1) * nq, :]
        q2 = jnp.concatenate([q * head_masks[0], q * head_masks[1]], axis=0)
        s_loc = _dot_nt(k_ref[pl.ds(koff, nkw), :], q2) + bias_ref[pat]
        s_ctx = _dot_nt(kc_ref[...], q2)
        s_bufs[slot][0:nkw, :] = s_loc
        s_bufs[slot][nkw:, :] = s_ctx
        c_bufs[slot][...] = jnp.maximum(jnp.max(s_loc, axis=0, keepdims=True), jnp.max(s_ctx, axis=0, keepdims=True))

    def softmax(slot):
        p_bufs[slot][...] = jnp.exp2(s_bufs[slot][...] - c_bufs[slot][...]).astype(BF16)

    def values(it, slot):
        _, base = window_base(it)
        blk0 = base * GRID_W // blk
        p_ref = p_bufs[slot]
        acc = None
        for t in range(nkw // blk):
            part = _dot(vt_ref[blk0 + t], p_ref[t * blk:(t + 1) * blk, :])
            acc = part if acc is None else acc + part
        for t in range(lc // blk):
            acc = acc + _dot(vt_ref[rows * GRID_W // blk + t], p_ref[nkw + t * blk:nkw + (t + 1) * blk, :])
        o = acc[0:LANES, :] * (1.0 / acc[LANES:LANES + 1, :])
        pair = jnp.where(top_half, o[:, :nq], o[:, nq:])
        o_ref[it * nq:(it + 1) * nq, :] = pair.T.astype(o_ref.dtype)

    scores(0, 0)
    for it in range(steps):
        if it + 1 < steps:
            scores(it + 1, (it + 1) % nbuf)
        softmax(it % nbuf)
        if it >= 1:
            values(it - 1, (it - 1) % nbuf)
    values(steps - 1, (steps - 1) % nbuf)


def _na_attention(q, k, v, bias, lc):
    b, s, nh = q.shape
    rows = s // GRID_W
    nsteps = rows // NA_QROWS
    steps = 8 if nsteps % 8 == 0 else nsteps
    tq = steps * NA_QROWS * GRID_W
    nq = NA_QROWS * GRID_W
    nk = NA_KROWS * GRID_W + lc
    blk = NA_QROWS * GRID_W
    nblk = (s + lc) // blk
    vt = _transposed_values(v, nh // LANES, LANES, blk)
    nbuf = min(FLASH_BUFFERS, steps)
    stage_bufs = ([pltpu.VMEM((nk, 2 * nq), F32)] * nbuf + [pltpu.VMEM((1, 2 * nq), F32)] * nbuf
                  + [pltpu.VMEM((nk, 2 * nq), BF16)] * nbuf)
    return pl.pallas_call(
        functools.partial(_na_attn_body, rows=rows, steps=steps, nbuf=nbuf),
        grid=(nh // LANES, b, s // tq),
        in_specs=[pl.BlockSpec((None, tq, LANES), lambda hp, bb, i: (bb, i, hp)),
                  pl.BlockSpec((None, s, LANES), lambda hp, bb, i: (bb, 0, hp)),
                  pl.BlockSpec((None, lc, LANES), lambda hp, bb, i: (bb, s // lc, hp)),
                  pl.BlockSpec((None, None, nblk, LANES + BF16_SUBLANES, blk), lambda hp, bb, i: (bb, hp, 0, 0, 0)),
                  pl.BlockSpec((None, 3, NA_KROWS * GRID_W, 2 * nq), lambda hp, bb, i: (hp, 0, 0, 0))],
        out_specs=pl.BlockSpec((None, tq, LANES), lambda hp, bb, i: (bb, i, hp)),
        out_shape=jax.ShapeDtypeStruct((b, s, nh), BF16),
        scratch_shapes=stage_bufs,
        compiler_params=_compiler_params(("parallel", "parallel", "parallel")),
        name="na_attention",
    )(q, k, k, vt, bias)


def _ret_proj_body(*refs, rope, with_qg, d):
    it = iter(refs)
    h_ref = next(it)
    cos_ref = sin_ref = None
    if rope:
        cos_ref, sin_ref = next(it), next(it)
    mod_ref, n1_ref, w_ref = (next(it) for _ in range(3))
    if with_qg:
        q_ref, g_ref = next(it), next(it)
    k_ref, v_ref = next(it), next(it)

    nk, nv, dk = RET_HEADS * RET_QK_DIM, RET_HEADS * RET_V_DIM, RET_QK_DIM
    mod = mod_ref[...]
    x = _norm_mod(h_ref[...], n1_ref[...], mod[:, d:2 * d], mod[:, 0:d]).astype(BF16)
    if rope:
        cos, sin = cos_ref[...], sin_ref[...]

    def rotate_store(t, ref, scale):
        for hh in range(RET_HEADS):
            x1 = t[:, hh * dk:hh * dk + dk // 2] * scale
            x2 = t[:, hh * dk + dk // 2:(hh + 1) * dk] * scale
            if rope:
                x1, x2 = x1 * cos - x2 * sin, x1 * sin + x2 * cos
            ref[:, hh * dk:hh * dk + dk // 2] = x1.astype(BF16)
            ref[:, hh * dk + dk // 2:(hh + 1) * dk] = x2.astype(BF16)

    rotate_store(_dot(x, w_ref[:, nk:2 * nk]), k_ref, dk ** -0.5)
    v_ref[...] = _dot(x, w_ref[:, 2 * nk:2 * nk + nv]).astype(BF16)
    if with_qg:
        rotate_store(_dot(x, w_ref[:, :nk]), q_ref, 1.0)
        g_ref[...] = _dot(x, w_ref[:, 2 * nk + nv:]).astype(BF16)


def _ret_project(h, mod, n1, w_qkvg, rope_tabs, with_qg, kv_tail=()):
    b, l, d = h.shape
    nk, nv = RET_HEADS * RET_QK_DIM, RET_HEADS * RET_V_DIM
    rope = rope_tabs is not None
    outs = ([(nk, BF16), (nv, BF16)] if with_qg else []) + [(nk, BF16), (nv, BF16)]
    res = _row_call(functools.partial(_ret_proj_body, rope=rope, with_qg=with_qg, d=d), name="ret_proj", batch=b,
                    length=l, tm=_proj_tile(l, kv_tail), rows=(h,), pos=tuple(rope_tabs) if rope else (),
                    vecs=(mod,), consts=(n1, w_qkvg), outs=outs, tails=kv_tail)
    return res if with_qg else [None, None] + list(res)


def _ret_scan_body(*refs, mode, chunk, nchunks, first_dir):
    it = iter(refs)
    ld_ref = next(it)
    q_ref = next(it) if mode != "state" else None
    k_ref, v_ref, s0_ref = next(it), next(it), next(it)
    if mode == "gated":
        yo_ref, g_ref, gn_ref = next(it), next(it), next(it)
    out_ref = next(it) if mode != "state" else None
    sfin_ref, st_ref = next(it), next(it)
    direction = pl.program_id(0) + first_dir
    hh, t = pl.program_id(2), pl.program_id(3)
    ld = ld_ref[direction, hh]

    @pl.when(t == 0)
    def _():
        st_ref[...] = s0_ref[...]

    ri = lax.broadcasted_iota(jnp.int32, (chunk, chunk), 0)
    ci = lax.broadcasted_iota(jnp.int32, (chunk, chunk), 1)
    dist = jnp.where(direction == 0, ri - ci, ci - ri).astype(F32)
    intra = jnp.where(dist >= 0, jnp.exp(jnp.maximum(dist, 0.0) * ld), 0.0)
    pos = lax.broadcasted_iota(jnp.int32, (chunk, 1), 0)
    rank = jnp.where(direction == 0, pos, chunk - 1 - pos).astype(F32)
    q_decay = jnp.exp((rank + 1.0) * ld)
    k_decay = jnp.exp((chunk - 1.0 - rank) * ld)
    chunk_decay = jnp.exp(jnp.full((1, 1), float(chunk), F32) * ld)

    for c in range(nchunks):
        cc = jnp.where(direction == 0, c, nchunks - 1 - c)
        rows = pl.ds(pl.multiple_of(cc * chunk, chunk), chunk)
        k = k_ref[rows, :]
        v = v_ref[rows, :]
        state = st_ref[...]
        if mode != "state":
            q = q_ref[rows, :]
            sc = _dot_nt(q, k) * intra
            y = _dot(sc.astype(BF16), v) + _dot(q, state.astype(BF16)) * q_decay
            if mode == "gated":
                y = y + yo_ref[rows, :]
                yc = y - jnp.mean(y, axis=-1, keepdims=True)
                var = jnp.mean(yc * yc, axis=-1, keepdims=True)
                y = yc * lax.rsqrt(var + EPS) * gn_ref[...] * _silu(g_ref[rows, :].astype(F32))
            out_ref[rows, :] = y.astype(out_ref.dtype)
        kd = (k.astype(F32) * k_decay).astype(BF16)
        st_ref[...] = state * chunk_decay + _dot_tn(kd, v)

    @pl.when(t == pl.num_programs(3) - 1)
    def _():
        sfin_ref[...] = st_ref[...]


def _ret_scan(log_decay, q, k, v, init_state, mode, dirs, length, gated=None):
    b = k.shape[0]
    dk, dv, nh = RET_QK_DIM, RET_V_DIM, RET_HEADS
    chunk = 256 if length % 256 == 0 else 128
    tt = 1024 if length % 1024 == 0 else chunk
    nt = length // tt
    first = dirs[0]

    def tok(d, t):
        return jnp.where(d + first == 0, t, nt - 1 - t)

    def token_block(width):
        return pl.BlockSpec((None, tt, width), lambda d, bb, h, t: (bb, tok(d, t), h))

    in_specs = [pl.BlockSpec(memory_space=pltpu.SMEM)]
    args = [log_decay]
    if mode != "state":
        in_specs.append(token_block(dk))
        args.append(q)
    in_specs += [token_block(dk), token_block(dv),
                 pl.BlockSpec((None, None, None, dk, dv), lambda d, bb, h, t: (d + first, bb, h, 0, 0))]
    args += [k, v, init_state]
    if mode == "gated":
        in_specs += [token_block(dv), token_block(dv), pl.BlockSpec((1, dv), lambda d, bb, h, t: (0, h))]
        args += list(gated)
    out_shape, out_specs = [], []
    if mode != "state":
        out_shape.append(jax.ShapeDtypeStruct((b, length, nh * dv), F32 if mode == "y" else BF16))
        out_specs.append(token_block(dv))
    out_shape.append(jax.ShapeDtypeStruct((len(dirs), b, nh, dk, dv), F32))
    out_specs.append(pl.BlockSpec((None, None, None, dk, dv), lambda d, bb, h, t: (d, bb, h, 0, 0)))
    res = pl.pallas_call(
        functools.partial(_ret_scan_body, mode=mode, chunk=chunk, nchunks=tt // chunk, first_dir=first),
        grid=(len(dirs), b, nh, nt),
        in_specs=in_specs,
        out_specs=out_specs,
        out_shape=out_shape,
        scratch_shapes=[pltpu.VMEM((dk, dv), F32)],
        compiler_params=_compiler_params(("parallel", "parallel", "parallel", "arbitrary")),
        name="ret_" + mode,
    )(*args)
    return res


def kernel(x, c, ctx, c_ctx, mod_w, mod_b, norm1, norm2, ffn_w13, ffn_w2, mla_w_down, mla_q_lora_norm, mla_kv_lora_norm, mla_w_uq, mla_w_ukv, mla_q_norm, mla_k_norm, mla_w_o, gqa_w_qkv, gqa_q_norm, gqa_k_norm, gqa_w_o, na_w_qkv, na_q_norm, na_k_norm, na_rpb, na_w_o, ret_w_qkvg, ret_log_decay_fwd, ret_log_decay_bwd, ret_out_norm, ret_w_o):
    b, s, d = x.shape
    lc = ctx.shape[1]
    depth = mod_w.shape[0]

    pad_rows = -(b + 1) % 16
    c_rows = jnp.concatenate([c, c_ctx[None, :], jnp.zeros((pad_rows, d), F32)], axis=0)
    mods = _modulation(c_rows, mod_w, mod_b)

    h, hc = x, ctx
    for i in range(depth):
        need_ctx = i < depth - 1
        mod_l = mods[i, :b].reshape(b, 1, 6 * d)
        mod_c = mods[i, b:b + 1].reshape(1, 1, 6 * d)
        n1 = norm1[i].reshape(1, d)
        n2 = norm2[i].reshape(1, d)
        kind, j = i % 4, i // 4
        if kind == 0:
            prep = _mla_prepare(mla_w_down[j], mla_q_lora_norm[j], mla_kv_lora_norm[j], mla_w_uq[j], mla_w_ukv[j],
                                mla_q_norm[j], mla_k_norm[j])
            tabs = _rope_tables_rolled(s, MLA_ROPE, LANES // 2)
            qc, kc, vc = _mla_project(hc, mod_c, n1, prep, None, need_ctx)
            ql, k_all, v_all = _mla_project(h, mod_l, n1, prep, tabs, True, (kc, vc))
            attn = functools.partial(_flash_attention, kv_heads=MLA_HEADS, group=1, dqk=MLA_QK_PAD, dv=MLA_V,
                                     tq=FLASH_QUERY_COLUMNS)
            w_o = mla_w_o[j].astype(BF16)
        elif kind == 1:
            w_qkv = gqa_w_qkv[j].astype(BF16)
            tabs = _rope_tables_rolled(s, GQA_HEAD_DIM, GQA_HEAD_DIM // 2)
            qc, kc, vc = _gqa_project(hc, mod_c, n1, w_qkv, gqa_q_norm[j], gqa_k_norm[j], None, need_ctx)
            ql, k_all, v_all = _gqa_project(h, mod_l, n1, w_qkv, gqa_q_norm[j], gqa_k_norm[j], tabs, True, (kc, vc))
            group = GQA_HEADS // GQA_KV_HEADS
            attn = functools.partial(_flash_attention, kv_heads=GQA_KV_HEADS, group=group, dqk=GQA_HEAD_DIM,
                                     dv=GQA_HEAD_DIM, tq=FLASH_QUERY_COLUMNS // group)
            w_o = gqa_w_o[j].astype(BF16)
        elif kind == 2:
            w_qkv = na_w_qkv[j].astype(BF16)
            qc, kc, vc = _na_project(hc, mod_c, n1, w_qkv, na_q_norm[j], na_k_norm[j], need_ctx)
            ql, k_all, v_all = _na_project(h, mod_l, n1, w_qkv, na_q_norm[j], na_k_norm[j], True, (kc, vc))
            w_o = na_w_o[j].astype(BF16)
        else:
            w_qkvg = ret_w_qkvg[j].astype(BF16)
            tabs = _rope_cos_sin(s, RET_QK_DIM)
            qc, gc, kc, vc = _ret_project(hc, mod_c, n1, w_qkvg, None, need_ctx)
            ql, gl, k_all, v_all = _ret_project(h, mod_l, n1, w_qkvg, tabs, True, (kc, vc))
            w_o = ret_w_o[j].astype(BF16)

        oc = None
        if kind in (0, 1):
            ol = attn(ql, k_all, v_all)
            if need_ctx:
                oc = attn(qc, kc, vc)
        elif kind == 2:
            bias = _na_bias_table(na_rpb[j], s // GRID_W)
            ol = _na_attention(ql, k_all, v_all, bias, lc)
            if need_ctx:
                def heads_major(t):
                    return t.reshape(b, lc, NA_HEADS, NA_HEAD_DIM).transpose(0, 2, 1, 3).reshape(
                        b * NA_HEADS, lc, NA_HEAD_DIM)
                oc = _flash_attention(heads_major(qc), heads_major(kc), heads_major(vc), kv_heads=1, group=1,
                                      dqk=NA_HEAD_DIM, dv=NA_HEAD_DIM, tq=lc)
                oc = oc.reshape(b, NA_HEADS, lc, NA_HEAD_DIM).transpose(0, 2, 1, 3).reshape(b, lc, -1)
        else:
            log_decay = jnp.stack([ret_log_decay_fwd[j], ret_log_decay_bwd[j]]).astype(F32)
            zero_state = jnp.zeros((2, b, RET_HEADS, RET_QK_DIM, RET_V_DIM), F32)
            ctx_state, = _ret_scan(log_decay, None, kc, vc, zero_state, "state", (0, 1), lc)
            y_fwd, _ = _ret_scan(log_decay, ql, k_all, v_all, ctx_state, "y", (0,), s)
            ol, _ = _ret_scan(log_decay, ql, k_all, v_all, ctx_state, "gated", (1,), s,
                              gated=(y_fwd, gl, ret_out_norm[j].reshape(1, -1)))
            if need_ctx:
                raise NotImplementedError("context update after a retention layer is not needed at this depth")

        w13, w2 = ffn_w13[i].astype(BF16), ffn_w2[i].astype(BF16)
        h = _ffn(h, mod_l, n2, w13, w2, ol, w_o)
        if need_ctx:
            hc = _ffn(hc, mod_c, n2, w13, w2, oc, w_o)
    return h
```

```python
import functools

import numpy as np
import jax
import jax.numpy as jnp
from jax import lax
from jax.experimental import pallas as pl
from jax.experimental.pallas import tpu as pltpu

F32 = jnp.float32
BF16 = jnp.bfloat16

GRID_W = 64
ROPE_THETA = 10000.0
EPS = 1e-6
MLA_HEADS, MLA_Q_RANK, MLA_KV_RANK, MLA_NOPE, MLA_ROPE, MLA_V = 8, 384, 256, 128, 64, 128
MLA_QK = MLA_NOPE + MLA_ROPE
MLA_QK_PAD = 256
GQA_HEADS, GQA_KV_HEADS, GQA_HEAD_DIM = 8, 2, 128
NA_HEADS, NA_HEAD_DIM, NA_WIN_ROWS, NA_WIN_COLS = 16, 64, 8, 16
RET_HEADS, RET_QK_DIM, RET_V_DIM = 4, 256, 512

LANES = 128
BF16_SUBLANES = 16
LOG2E = 1.4426950408889634
V7X_VMEM_LIMIT_BYTES = 56 * 1024 * 1024
NEG_BIG = -1e30

FLASH_BUFFERS = 2
FLASH_QUERY_COLUMNS = 1024
FLASH_TILES_PER_STEP = 2
FLASH_UNROLL_CHUNKS = 12
NA_QROWS = 4
NA_KROWS = NA_QROWS + NA_WIN_ROWS


def _dot(a, b):
    return jnp.dot(a, b, preferred_element_type=F32)


def _dot_nt(a, b):
    return lax.dot_general(a, b, (((1,), (1,)), ((), ())), preferred_element_type=F32)


def _dot_tn(a, b):
    return lax.dot_general(a, b, (((0,), (0,)), ((), ())), preferred_element_type=F32)


def _silu(x):
    return x / (1.0 + jnp.exp(-x))


def _norm_mod(h, gain, scale, shift):
    y = h * lax.rsqrt(jnp.mean(h * h, axis=-1, keepdims=True) + EPS) * gain
    return y * (1.0 + scale) + shift


def _rms(x, denom):
    return x * lax.rsqrt(jnp.sum(x * x, axis=-1, keepdims=True) / denom + EPS)


def _compiler_params(semantics):
    return pltpu.CompilerParams(dimension_semantics=semantics, vmem_limit_bytes=V7X_VMEM_LIMIT_BYTES)


def _row_call(body, *, name, batch, length, tm, rows=(), pos=(), vecs=(), consts=(), outs=(), tails=()):
    n_main = length // tm
    n_tail = tails[0].shape[1] // tm if tails else 0
    n_in = len(rows) + len(pos) + len(vecs) + len(consts)
    n_plain = len(outs) - len(tails)
    if tails:
        assert all(t.shape[1] == n_tail * tm for t in tails)
        inner = body

        def body(*refs):
            out_refs = refs[n_in + len(tails):]
            i = pl.program_id(1)

            @pl.when(i < n_main)
            def _():
                inner(*refs[:n_in], *out_refs)

            @pl.when(i >= n_main)
            def _():
                for t_ref, o_ref in zip(refs[n_in:n_in + len(tails)], out_refs[n_plain:]):
                    o_ref[...] = t_ref[...]

    def main(i):
        return jnp.minimum(i, n_main - 1) if tails else i

    in_specs = []
    for a in rows:
        in_specs.append(pl.BlockSpec((None, tm, a.shape[-1]), lambda b, i: (b, main(i), 0)))
    for a in pos:
        in_specs.append(pl.BlockSpec((tm, a.shape[-1]), lambda b, i: (main(i), 0)))
    for a in vecs:
        if a.shape[0] == 1:
            in_specs.append(pl.BlockSpec((None, 1, a.shape[-1]), lambda b, i: (0, 0, 0)))
        else:
            in_specs.append(pl.BlockSpec((None, 1, a.shape[-1]), lambda b, i: (b, 0, 0)))
    for a in consts:
        nd = a.ndim
        in_specs.append(pl.BlockSpec(a.shape, lambda b, i, nd=nd: (0,) * nd, pipeline_mode=pl.Buffered(1)))
    for a in tails:
        in_specs.append(pl.BlockSpec((None, tm, a.shape[-1]), lambda b, i: (b, jnp.maximum(i - n_main, 0), 0)))
    out_shape, out_specs = [], []
    for k, (f, dt) in enumerate(outs):
        if k < n_plain:
            out_shape.append(jax.ShapeDtypeStruct((batch, length, f), dt))
            out_specs.append(pl.BlockSpec((None, tm, f), lambda b, i: (b, main(i), 0)))
        else:
            out_shape.append(jax.ShapeDtypeStruct((batch, length + n_tail * tm, f), dt))
            out_specs.append(pl.BlockSpec((None, tm, f), lambda b, i: (b, i, 0)))
    return pl.pallas_call(
        body,
        grid=(batch, n_main + n_tail),
        in_specs=in_specs,
        out_specs=out_specs,
        out_shape=out_shape,
        compiler_params=_compiler_params(("parallel", "arbitrary" if tails else "parallel")),
        name=name,
    )(*rows, *pos, *vecs, *consts, *tails)


def _tile(length, pref, also=None):
    t = pref
    while t >= 128:
        if length % t == 0 and (also is None or also % t == 0):
            return t
        t //= 2
    return length


def _mod_body(c_ref, w_ref, b_ref, o_ref):
    s = _silu(c_ref[...]).astype(BF16)
    o_ref[...] = _dot(s, w_ref[...].astype(BF16)) + b_ref[...]


def _modulation(c_rows, mod_w, mod_b):
    depth, d, n = mod_w.shape
    tn = 1536 if n % 1536 == 0 else n
    return pl.pallas_call(
        _mod_body,
        grid=(depth, n // tn),
        in_specs=[pl.BlockSpec(c_rows.shape, lambda l, j: (0, 0)),
                  pl.BlockSpec((None, d, tn), lambda l, j: (l, 0, j)),
                  pl.BlockSpec((None, 1, tn), lambda l, j: (l, 0, j))],
        out_specs=pl.BlockSpec((None, c_rows.shape[0], tn), lambda l, j: (l, 0, j)),
        out_shape=jax.ShapeDtypeStruct((depth, c_rows.shape[0], n), F32),
        compiler_params=_compiler_params(("parallel", "parallel")),
        name="modulation",
    )(c_rows, mod_w, mod_b.reshape(depth, 1, n))


def _rope_cos_sin(length, dim):
    t = jnp.arange(length)
    row = (t // GRID_W).astype(F32)
    col = (t % GRID_W).astype(F32)
    quarter = dim // 4
    inv_freq = ROPE_THETA ** (-jnp.arange(quarter, dtype=F32) / quarter)
    ang = jnp.concatenate([row[:, None] * inv_freq, col[:, None] * inv_freq], axis=-1)
    return jnp.cos(ang), jnp.sin(ang)


def _rope_tables_rolled(length, dim, lane_half):
    cos, sin = _rope_cos_sin(length, dim)
    pad = jnp.zeros((length, lane_half - dim // 2), F32)
    cos_t = jnp.concatenate([cos, pad, cos, pad], axis=-1)
    sin_t = jnp.concatenate([-sin, pad, sin, pad], axis=-1)
    return cos_t, sin_t


def _rope_rolled(x, cos_t, sin_t):
    return x * cos_t + pltpu.roll(x, x.shape[-1] // 2, 1) * sin_t


def _mla_proj_body(*refs, rope, with_q, d):
    it = iter(refs)
    h_ref = next(it)
    cos_ref = sin_ref = None
    if rope:
        cos_ref, sin_ref = next(it), next(it)
    mod_ref, n1_ref, wd_ref, gql_ref, gkl_ref, wuq_ref, wuk_ref, wuv_ref, gq_ref, gk_ref = (next(it) for _ in range(10))
    if with_q:
        q_ref = next(it)
    k_ref, v_ref = next(it), next(it)

    mod = mod_ref[...]
    x = _norm_mod(h_ref[...], n1_ref[...], mod[:, d:2 * d], mod[:, 0:d]).astype(BF16)
    dl = _dot(x, wd_ref[...])
    ckv = (_rms(dl[:, MLA_Q_RANK:MLA_Q_RANK + MLA_KV_RANK], MLA_KV_RANK) * gkl_ref[...]).astype(BF16)
    kr = dl[:, MLA_Q_RANK + MLA_KV_RANK:]
    kn = _dot(ckv, wuk_ref[...])
    v_ref[...] = _dot(ckv, wuv_ref[...]).astype(BF16)
    if rope:
        cos_t, sin_t = cos_ref[...], sin_ref[...]
    gk = gk_ref[...]
    kr_sq = kr * kr
    kr_g = kr * gk[:, MLA_NOPE:]
    if rope:
        kr_g = _rope_rolled(kr_g, cos_t, sin_t)
    for hh in range(MLA_HEADS):
        kh = kn[:, hh * MLA_NOPE:(hh + 1) * MLA_NOPE]
        r = lax.rsqrt(jnp.sum(kh * kh + kr_sq, axis=-1, keepdims=True) / MLA_QK + EPS)
        k_ref[:, hh * MLA_QK_PAD:hh * MLA_QK_PAD + MLA_NOPE] = (kh * r * gk[:, :MLA_NOPE]).astype(BF16)
        k_ref[:, hh * MLA_QK_PAD + MLA_NOPE:(hh + 1) * MLA_QK_PAD] = (kr_g * r).astype(BF16)
    if with_q:
        cq = (_rms(dl[:, :MLA_Q_RANK], MLA_Q_RANK) * gql_ref[...]).astype(BF16)
        q = _dot(cq, wuq_ref[...])
        gq = gq_ref[...]
        scale = MLA_QK ** -0.5 * LOG2E
        for hh in range(MLA_HEADS):
            qn = q[:, hh * MLA_QK_PAD:hh * MLA_QK_PAD + MLA_NOPE]
            qr = q[:, hh * MLA_QK_PAD + MLA_NOPE:(hh + 1) * MLA_QK_PAD]
            ss = jnp.sum(qn * qn + qr * qr, axis=-1, keepdims=True)
            r = lax.rsqrt(ss / MLA_QK + EPS) * scale
            q_ref[:, hh * MLA_QK_PAD:hh * MLA_QK_PAD + MLA_NOPE] = (qn * r * gq[:, :MLA_NOPE]).astype(BF16)
            qr = qr * r * gq[:, MLA_NOPE:]
            if rope:
                qr = _rope_rolled(qr, cos_t, sin_t)
            q_ref[:, hh * MLA_QK_PAD + MLA_NOPE:(hh + 1) * MLA_QK_PAD] = qr.astype(BF16)


def _pad_rope_cols(r):
    half = MLA_ROPE // 2
    z = jnp.zeros(r.shape[:-1] + (LANES // 2 - half,), r.dtype)
    return jnp.concatenate([r[..., :half], z, r[..., half:], z], axis=-1)


def _mla_prepare(w_down, q_lora_norm, kv_lora_norm, w_uq, w_ukv, q_norm, k_norm):
    d = w_down.shape[0]
    nq = MLA_Q_RANK + MLA_KV_RANK
    wd = jnp.concatenate([w_down[:, :nq], _pad_rope_cols(w_down[:, nq:])], axis=1).astype(BF16)
    wq = w_uq.reshape(MLA_Q_RANK, MLA_HEADS, MLA_QK)
    wq = jnp.concatenate([wq[..., :MLA_NOPE], _pad_rope_cols(wq[..., MLA_NOPE:])], axis=-1)
    wq = wq.reshape(MLA_Q_RANK, MLA_HEADS * MLA_QK_PAD).astype(BF16)
    wkv = w_ukv.reshape(MLA_KV_RANK, MLA_HEADS, MLA_NOPE + MLA_V)
    wuk = wkv[..., :MLA_NOPE].reshape(MLA_KV_RANK, MLA_HEADS * MLA_NOPE).astype(BF16)
    wuv = wkv[..., MLA_NOPE:].reshape(MLA_KV_RANK, MLA_HEADS * MLA_V).astype(BF16)

    def pad_gain(g):
        return jnp.concatenate([g[:MLA_NOPE], _pad_rope_cols(g[MLA_NOPE:])]).reshape(1, MLA_QK_PAD)

    return (wd, q_lora_norm.reshape(1, -1), kv_lora_norm.reshape(1, -1), wq, wuk, wuv,
            pad_gain(q_norm), pad_gain(k_norm))


def _proj_tile(length, kv_tail):
    return _tile(length, 512, kv_tail[0].shape[1] if kv_tail else None)


def _mla_project(h, mod, n1, prep, rope_tabs, with_q, kv_tail=()):
    b, l, d = h.shape
    rope = rope_tabs is not None
    outs = []
    if with_q:
        outs.append((MLA_HEADS * MLA_QK_PAD, BF16))
    outs += [(MLA_HEADS * MLA_QK_PAD, BF16), (MLA_HEADS * MLA_V, BF16)]
    res = _row_call(functools.partial(_mla_proj_body, rope=rope, with_q=with_q, d=d),
                    name="mla_proj", batch=b, length=l, tm=_proj_tile(l, kv_tail), rows=(h,),
                    pos=tuple(rope_tabs) if rope else (), vecs=(mod,), consts=(n1,) + tuple(prep), outs=outs,
                    tails=kv_tail)
    return res if with_q else [None] + list(res)


def _gqa_proj_body(*refs, rope, with_q, d):
    it = iter(refs)
    h_ref = next(it)
    cos_ref = sin_ref = None
    if rope:
        cos_ref, sin_ref = next(it), next(it)
    mod_ref, n1_ref, w_ref, gq_ref, gk_ref = (next(it) for _ in range(5))
    if with_q:
        q_ref = next(it)
    k_ref, v_ref = next(it), next(it)

    dh = GQA_HEAD_DIM
    nq = GQA_HEADS * dh
    nkv = GQA_KV_HEADS * dh
    mod = mod_ref[...]
    x = _norm_mod(h_ref[...], n1_ref[...], mod[:, d:2 * d], mod[:, 0:d]).astype(BF16)
    if rope:
        cos_t, sin_t = cos_ref[...], sin_ref[...]
    kv = _dot(x, w_ref[:, nq:])
    v_ref[...] = kv[:, nkv:].astype(BF16)
    for hh in range(GQA_KV_HEADS):
        kh = _rms(kv[:, hh * dh:(hh + 1) * dh], dh) * gk_ref[...]
        if rope:
            kh = _rope_rolled(kh, cos_t, sin_t)
        k_ref[:, hh * dh:(hh + 1) * dh] = kh.astype(BF16)
    if with_q:
        q = _dot(x, w_ref[:, :nq])
        gq = gq_ref[...] * (dh ** -0.5 * LOG2E)
        for hh in range(GQA_HEADS):
            qh = _rms(q[:, hh * dh:(hh + 1) * dh], dh) * gq
            if rope:
                qh = _rope_rolled(qh, cos_t, sin_t)
            q_ref[:, hh * dh:(hh + 1) * dh] = qh.astype(BF16)


def _gqa_project(h, mod, n1, w_qkv, q_norm, k_norm, rope_tabs, with_q, kv_tail=()):
    b, l, d = h.shape
    rope = rope_tabs is not None
    outs = []
    if with_q:
        outs.append((GQA_HEADS * GQA_HEAD_DIM, BF16))
    outs += [(GQA_KV_HEADS * GQA_HEAD_DIM, BF16)] * 2
    res = _row_call(functools.partial(_gqa_proj_body, rope=rope, with_q=with_q, d=d),
                    name="gqa_proj", batch=b, length=l, tm=_proj_tile(l, kv_tail), rows=(h,),
                    pos=tuple(rope_tabs) if rope else (), vecs=(mod,),
                    consts=(n1, w_qkv, q_norm.reshape(1, -1), k_norm.reshape(1, -1)), outs=outs, tails=kv_tail)
    return res if with_q else [None] + list(res)


def _flash_body(q_ref, k_ref, vt_ref, o_ref, *scratch, group, dqk, dv, tq, tk, n, nbuf, units):
    s_bufs, c_bufs, p_bufs, a_bufs = (scratch[k * nbuf:(k + 1) * nbuf] for k in range(4))
    rest = scratch[4 * nbuf:]
    qs_refs, m_refs, acc_refs = (rest[k * units:(k + 1) * units] for k in range(3))
    for u in range(units):
        for g in range(group):
            qs_refs[u][g * tq:(g + 1) * tq, :] = q_ref[u * tq:(u + 1) * tq, g * dqk:(g + 1) * dqk]
        m_refs[u][...] = jnp.full(m_refs[u].shape, NEG_BIG, F32)
        acc_refs[u][...] = jnp.zeros(acc_refs[u].shape, F32)

    def scores(u, j, slot):
        rows = pl.ds(pl.multiple_of(j * tk, tk), tk)
        s = _dot_nt(k_ref[rows, :], qs_refs[u][...])
        s_bufs[slot][...] = s
        c_bufs[slot][...] = jnp.max(s, axis=0, keepdims=True)

    def softmax(u, slot):
        m_prev = m_refs[u][...]
        m_new = jnp.maximum(m_prev, c_bufs[slot][...])
        a_bufs[slot][...] = jnp.exp2(m_prev - m_new)
        m_refs[u][...] = m_new
        p_bufs[slot][...] = jnp.exp2(s_bufs[slot][...] - m_new).astype(BF16)

    def weighted_values(u, j, slot):
        acc_refs[u][...] = a_bufs[slot][...] * acc_refs[u][...] + _dot(vt_ref[j], p_bufs[slot][...])

    def finish(u):
        out = (acc_refs[u][0:dv, :] * (1.0 / acc_refs[u][dv:dv + 1, :])).T
        for g in range(group):
            o_ref[u * tq:(u + 1) * tq, g * dv:(g + 1) * dv] = out[g * tq:(g + 1) * tq, :].astype(o_ref.dtype)

    if n <= FLASH_UNROLL_CHUNKS:
        items = [(u, j) for u in range(units) for j in range(n)]

        def after_softmax(g):
            u, j = items[g]
            weighted_values(u, j, g % nbuf)
            if j == n - 1:
                finish(u)

        scores(*items[0], 0)
        for g in range(len(items)):
            if g + 1 < len(items):
                scores(*items[g + 1], (g + 1) % nbuf)
            softmax(items[g][0], g % nbuf)
            if g >= 1:
                after_softmax(g - 1)
        after_softmax(len(items) - 1)
    else:
        assert units == 1

        def steady(j, r):
            scores(0, j + 1, (r + 1) % nbuf)
            softmax(0, r)
            weighted_values(0, j - 1, (r - 1) % nbuf)

        scores(0, 0, 0)
        scores(0, 1, 1 % nbuf)
        softmax(0, 0)
        n_rounds = (n - 2) // nbuf

        def one_round(t, carry):
            for w in range(nbuf):
                steady(1 + nbuf * t + w, (1 + w) % nbuf)
            return carry

        lax.fori_loop(0, n_rounds, one_round, 0)
        for j in range(1 + nbuf * n_rounds, n - 1):
            steady(j, j % nbuf)
        softmax(0, (n - 1) % nbuf)
        weighted_values(0, n - 2, (n - 2) % nbuf)
        weighted_values(0, n - 1, (n - 1) % nbuf)
        finish(0)


def _values_t_body(v_ref, vt_ref, *, n, tk, dv):
    extra = (lax.broadcasted_iota(jnp.int32, (BF16_SUBLANES, tk), 0) == 0).astype(F32).astype(vt_ref.dtype)
    for c in range(n):
        vt_ref[c, 0:dv, :] = v_ref[c * tk:(c + 1) * tk, :].astype(F32).T.astype(vt_ref.dtype)
        vt_ref[c, dv:, :] = extra


def _transposed_values(v, heads, dv, tk):
    b, l, _ = v.shape
    n = l // tk
    return pl.pallas_call(
        functools.partial(_values_t_body, n=n, tk=tk, dv=dv),
        grid=(b, heads),
        in_specs=[pl.BlockSpec((None, l, dv), lambda bb, h: (bb, 0, h))],
        out_specs=pl.BlockSpec((None, None, n, dv + BF16_SUBLANES, tk), lambda bb, h: (bb, h, 0, 0, 0)),
        out_shape=jax.ShapeDtypeStruct((b, heads, n, dv + BF16_SUBLANES, tk), v.dtype),
        compiler_params=_compiler_params(("parallel", "parallel")),
        name="values_t",
    )(v)


def _key_chunk(length):
    for tk in (768, 512, 256, 128):
        if length % tk == 0:
            return tk
    return length


def _flash_attention(q, k, v, *, kv_heads, group, dqk, dv, tq):
    b, lq, _ = q.shape
    kv_len = k.shape[1]
    tq = _tile(lq, tq)
    tk = _key_chunk(kv_len)
    n = kv_len // tk
    rows = group * tq
    units = FLASH_TILES_PER_STEP if (n <= FLASH_UNROLL_CHUNKS and lq % (FLASH_TILES_PER_STEP * tq) == 0) else 1
    vt = _transposed_values(v, kv_heads, dv, tk)
    dve = dv + BF16_SUBLANES
    nbuf = min(FLASH_BUFFERS, n)
    stage_bufs = ([pltpu.VMEM((tk, rows), F32)] * nbuf
                  + [pltpu.VMEM((1, rows), F32)] * nbuf
                  + [pltpu.VMEM((tk, rows), BF16)] * nbuf
                  + [pltpu.VMEM((1, rows), F32)] * nbuf)
    return pl.pallas_call(
        functools.partial(_flash_body, group=group, dqk=dqk, dv=dv, tq=tq, tk=tk, n=n, nbuf=nbuf, units=units),
        grid=(b, kv_heads, lq // (units * tq)),
        in_specs=[pl.BlockSpec((None, units * tq, group * dqk), lambda bb, h, i: (bb, i, h)),
                  pl.BlockSpec((None, kv_len, dqk), lambda bb, h, i: (bb, 0, h)),
                  pl.BlockSpec((None, None, n, dve, tk), lambda bb, h, i: (bb, h, 0, 0, 0))],
        out_specs=pl.BlockSpec((None, units * tq, group * dv), lambda bb, h, i: (bb, i, h)),
        out_shape=jax.ShapeDtypeStruct((b, lq, kv_heads * group * dv), BF16),
        scratch_shapes=stage_bufs + ([pltpu.VMEM((rows, dqk), BF16)] * units
                                     + [pltpu.VMEM((1, rows), F32)] * units
                                     + [pltpu.VMEM((dve, rows), F32)] * units),
        compiler_params=_compiler_params(("parallel", "parallel", "parallel")),
        name="flash_attention",
    )(q, k, vt)


def _ffn_body(*refs, d, hidden, chunks, with_mixer_out):
    if with_mixer_out:
        h_ref, o_ref, mod_ref, wo_ref, n2_ref, w13_ref, w2_ref, out_ref = refs
    else:
        h_ref, mod_ref, n2_ref, w13_ref, w2_ref, out_ref = refs
    h = h_ref[...]
    mod = mod_ref[...]
    if with_mixer_out:
        h = h + mod[:, 2 * d:3 * d] * _dot(o_ref[...], wo_ref[...])
    x = _norm_mod(h, n2_ref[...], mod[:, 4 * d:5 * d], mod[:, 3 * d:4 * d]).astype(BF16)
    acc = None
    for c0, cw in chunks:
        a1 = _dot(x, w13_ref[:, c0:c0 + cw])
        a3 = _dot(x, w13_ref[:, hidden + c0:hidden + c0 + cw])
        part = _dot((_silu(a1) * a3).astype(BF16), w2_ref[c0:c0 + cw, :])
        acc = part if acc is None else acc + part
    out_ref[...] = h + mod[:, 5 * d:6 * d] * acc


def _hidden_chunks(hidden, width):
    chunks, c0 = [], 0
    while c0 < hidden:
        cw = min(width, hidden - c0)
        chunks.append((c0, cw))
        c0 += cw
    return tuple(chunks)


def _ffn(h, mod, n2, w13, w2, mixer_out=None, w_o=None):
    b, l, d = h.shape
    hidden = w2.shape[0]
    tm = _tile(l, 512)
    fused = mixer_out is not None
    return _row_call(functools.partial(_ffn_body, d=d, hidden=hidden, chunks=_hidden_chunks(hidden, 1024),
                                       with_mixer_out=fused),
                     name="swiglu", batch=b, length=l, tm=tm, rows=(h, mixer_out) if fused else (h,), vecs=(mod,),
                     consts=((w_o,) if fused else ()) + (n2, w13, w2), outs=[(d, F32)])[0]


def _na_proj_body(*refs, with_q, d):
    it = iter(refs)
    h_ref, mod_ref, n1_ref, w_ref, gq_ref, gk_ref = (next(it) for _ in range(6))
    if with_q:
        q_ref = next(it)
    k_ref, v_ref = next(it), next(it)

    nh = NA_HEADS * NA_HEAD_DIM
    mod = mod_ref[...]
    x = _norm_mod(h_ref[...], n1_ref[...], mod[:, d:2 * d], mod[:, 0:d]).astype(BF16)
    tm = x.shape[0]
    left = lax.broadcasted_iota(jnp.int32, (tm, LANES), 1) < NA_HEAD_DIM

    def pair_norm(t, gain2):
        sq = t * t
        sl = jnp.sum(jnp.where(left, sq, 0.0), axis=-1, keepdims=True)
        sr = jnp.sum(jnp.where(left, 0.0, sq), axis=-1, keepdims=True)
        r = jnp.where(left, lax.rsqrt(sl / NA_HEAD_DIM + EPS), lax.rsqrt(sr / NA_HEAD_DIM + EPS))
        return t * r * gain2

    v_ref[...] = _dot(x, w_ref[:, 2 * nh:]).astype(BF16)
    k = _dot(x, w_ref[:, nh:2 * nh])
    gk = gk_ref[...]
    for hp in range(nh // LANES):
        k_ref[:, hp * LANES:(hp + 1) * LANES] = pair_norm(k[:, hp * LANES:(hp + 1) * LANES], gk).astype(BF16)
    if with_q:
        q = _dot(x, w_ref[:, :nh])
        gq = gq_ref[...] * (NA_HEAD_DIM ** -0.5 * LOG2E)
        for hp in range(nh // LANES):
            q_ref[:, hp * LANES:(hp + 1) * LANES] = pair_norm(q[:, hp * LANES:(hp + 1) * LANES], gq).astype(BF16)


def _na_project(h, mod, n1, w_qkv, q_norm, k_norm, with_q, kv_tail=()):
    b, l, d = h.shape
    nh = NA_HEADS * NA_HEAD_DIM
    outs = [(nh, BF16)] * (3 if with_q else 2)
    gq2 = jnp.concatenate([q_norm, q_norm]).reshape(1, LANES)
    gk2 = jnp.concatenate([k_norm, k_norm]).reshape(1, LANES)
    res = _row_call(functools.partial(_na_proj_body, with_q=with_q, d=d), name="na_proj", batch=b, length=l,
                    tm=_proj_tile(l, kv_tail), rows=(h,), vecs=(mod,), consts=(n1, w_qkv, gq2, gk2), outs=outs,
                    tails=kv_tail)
    return res if with_q else [None] + list(res)


def _na_bias_plan(rows):
    wr = NA_WIN_ROWS
    nsteps = rows // NA_QROWS
    plan = []
    for j in (0, 1, nsteps - 1):
        base = int(np.clip(j * NA_QROWS - wr // 2, 0, rows - NA_KROWS))
        per_a = []
        for a in range(NA_QROWS):
            r = j * NA_QROWS + a
            r0 = int(np.clip(r - wr // 2, 0, rows - wr))
            ok = tuple(bool(r0 <= base + wp < r0 + wr) for wp in range(NA_KROWS))
            per_a.append((base - r + (wr - 1), ok))
        plan.append(tuple(per_a))
    return tuple(plan)


def _na_bias_body(e_ref, t_ref, *, plan):
    w = GRID_W
    left = lax.broadcasted_iota(jnp.int32, (w, LANES), 1) < w
    outside = jnp.full((w, LANES), NEG_BIG, F32)
    for pat, per_a in enumerate(plan):
        for wp in range(NA_KROWS):
            for blk in range(2 * NA_QROWS // 2):
                hd, a = divmod(2 * blk, NA_QROWS)
                halves = []
                for aa in (a, a + 1):
                    start, ok = per_a[aa]
                    halves.append(e_ref[hd, start + wp] if ok[wp] else outside)
                t_ref[pat, wp * w:(wp + 1) * w, blk * LANES:(blk + 1) * LANES] = jnp.where(left, halves[0], halves[1])


def _na_bias_table(rpb, rows):
    wr, wc, w = NA_WIN_ROWS, NA_WIN_COLS, GRID_W
    nh = rpb.shape[0]
    col = np.arange(w)
    qcol = np.arange(LANES) % w
    col_start = np.clip(qcol - wc // 2, 0, w - wc)
    col_ok = (col[:, None] >= col_start[None, :]) & (col[:, None] < col_start[None, :] + wc)
    col_rel = col[:, None] - qcol[None, :] + (wc - 1)
    onehot = (np.arange(2 * wc - 1)[:, None, None] == col_rel[None]).astype(np.float32)
    e = jnp.einsum('hrk,kcl->hrcl', rpb.astype(F32) * LOG2E, onehot, precision=lax.Precision.HIGHEST)
    e = jnp.where(col_ok[None, None], e, NEG_BIG)
    nq2, nk = 2 * NA_QROWS * w, NA_KROWS * w
    return pl.pallas_call(
        functools.partial(_na_bias_body, plan=_na_bias_plan(rows)),
        grid=(nh // 2,),
        in_specs=[pl.BlockSpec((2, 2 * wr - 1, w, LANES), lambda hp: (hp, 0, 0, 0))],
        out_specs=pl.BlockSpec((None, 3, nk, nq2), lambda hp: (hp, 0, 0, 0)),
        out_shape=jax.ShapeDtypeStruct((nh // 2, 3, nk, nq2), F32),
        compiler_params=_compiler_params(("parallel",)),
        name="na_bias_table",
    )(e)


def _na_attn_body(q_ref, k_ref, kc_ref, vt_ref, bias_ref, o_ref, *scratch, rows, steps, nbuf):
    s_bufs, c_bufs, p_bufs = (scratch[k * nbuf:(k + 1) * nbuf] for k in range(3))
    nq = NA_QROWS * GRID_W
    nkw = NA_KROWS * GRID_W
    lc = kc_ref.shape[0]
    blk = vt_ref.shape[-1]
    nsteps = rows // NA_QROWS
    rb = pl.program_id(2)
    left = lax.broadcasted_iota(jnp.int32, (nq, LANES), 1) < NA_HEAD_DIM
    head_masks = (left.astype(F32).astype(BF16), jnp.logical_not(left).astype(F32).astype(BF16))
    top_half = lax.broadcasted_iota(jnp.int32, (LANES, nq), 0) < NA_HEAD_DIM

    def window_base(it):
        j = rb * steps + it
        return j, jnp.clip(j * NA_QROWS - NA_WIN_ROWS // 2, 0, rows - NA_KROWS)

    def scores(it, slot):
        j, base = window_base(it)
        pat = jnp.where(j == 0, 0, jnp.where(j == nsteps - 1, 2, 1))
        koff = pl.multiple_of(base * GRID_W, NA_QROWS * GRID_W)
        q = q_ref[it * nq:(it + 1) * nq, :]
        q2 = jnp.concatenate([q * head_masks[0], q * head_masks[1]], axis=0)
        s_loc = _dot_nt(k_ref[pl.ds(koff, nkw), :], q2) + bias_ref[pat]
        s_ctx = _dot_nt(kc_ref[...], q2)
        s_bufs[slot][0:nkw, :] = s_loc
        s_bufs[slot][nkw:, :] = s_ctx
        c_bufs[slot][...] = jnp.maximum(jnp.max(s_loc, axis=0, keepdims=True), jnp.max(s_ctx, axis=0, keepdims=True))

    def softmax(slot):
        p_bufs[slot][...] = jnp.exp2(s_bufs[slot][...] - c_bufs[slot][...]).astype(BF16)

    def values(it, slot):
        _, base = window_base(it)
        blk0 = base * GRID_W // blk
        p_ref = p_bufs[slot]
        acc = None
        for t in range(nkw // blk):
            part = _dot(vt_ref[blk0 + t], p_ref[t * blk:(t + 1) * blk, :])
            acc = part if acc is None else acc + part
        for t in range(lc // blk):
            acc = acc + _dot(vt_ref[rows * GRID_W // blk + t], p_ref[nkw + t * blk:nkw + (t + 1) * blk, :])
        o = acc[0:LANES, :] * (1.0 / acc[LANES:LANES + 1, :])
        pair = jnp.where(top_half, o[:, :nq], o[:, nq:])
        o_ref[it * nq:(it + 1) * nq, :] = pair.T.astype(o_ref.dtype)

    scores(0, 0)
    for it in range(steps):
        if it + 1 < steps:
            scores(it + 1, (it + 1) % nbuf)
        softmax(it % nbuf)
        if it >= 1:
            values(it - 1, (it - 1) % nbuf)
    values(steps - 1, (steps - 1) % nbuf)


def _na_attention(q, k, v, bias, lc):
    b, s, nh = q.shape
    rows = s // GRID_W
    nsteps = rows // NA_QROWS
    steps = 8 if nsteps % 8 == 0 else nsteps
    tq = steps * NA_QROWS * GRID_W
    nq = NA_QROWS * GRID_W
    nk = NA_KROWS * GRID_W + lc
    blk = NA_QROWS * GRID_W
    nblk = (s + lc) // blk
    vt = _transposed_values(v, nh // LANES, LANES, blk)
    nbuf = min(FLASH_BUFFERS, steps)
    stage_bufs = ([pltpu.VMEM((nk, 2 * nq), F32)] * nbuf + [pltpu.VMEM((1, 2 * nq), F32)] * nbuf
                  + [pltpu.VMEM((nk, 2 * nq), BF16)] * nbuf)
    return pl.pallas_call(
        functools.partial(_na_attn_body, rows=rows, steps=steps, nbuf=nbuf),
        grid=(nh // LANES, b, s // tq),
        in_specs=[pl.BlockSpec((None, tq, LANES), lambda hp, bb, i: (bb, i, hp)),
                  pl.BlockSpec((None, s, LANES), lambda hp, bb, i: (bb, 0, hp)),
                  pl.BlockSpec((None, lc, LANES), lambda hp, bb, i: (bb, s // lc, hp)),
                  pl.BlockSpec((None, None, nblk, LANES + BF16_SUBLANES, blk), lambda hp, bb, i: (bb, hp, 0, 0, 0)),
                  pl.BlockSpec((None, 3, NA_KROWS * GRID_W, 2 * nq), lambda hp, bb, i: (hp, 0, 0, 0))],
        out_specs=pl.BlockSpec((None, tq, LANES), lambda hp, bb, i: (bb, i, hp)),
        out_shape=jax.ShapeDtypeStruct((b, s, nh), BF16),
        scratch_shapes=stage_bufs,
        compiler_params=_compiler_params(("parallel", "parallel", "parallel")),
        name="na_attention",
    )(q, k, k, vt, bias)


def _ret_proj_body(*refs, rope, with_qg, d):
    it = iter(refs)
    h_ref = next(it)
    cos_ref = sin_ref = None
    if rope:
        cos_ref, sin_ref = next(it), next(it)
    mod_ref, n1_ref, w_ref = (next(it) for _ in range(3))
    if with_qg:
        q_ref, g_ref = next(it), next(it)
    k_ref, v_ref = next(it), next(it)

    nk, nv, dk = RET_HEADS * RET_QK_DIM, RET_HEADS * RET_V_DIM, RET_QK_DIM
    mod = mod_ref[...]
    x = _norm_mod(h_ref[...], n1_ref[...], mod[:, d:2 * d], mod[:, 0:d]).astype(BF16)
    if rope:
        cos, sin = cos_ref[...], sin_ref[...]

    def rotate_store(t, ref, scale):
        for hh in range(RET_HEADS):
            x1 = t[:, hh * dk:hh * dk + dk // 2] * scale
            x2 = t[:, hh * dk + dk // 2:(hh + 1) * dk] * scale
            if rope:
                x1, x2 = x1 * cos - x2 * sin, x1 * sin + x2 * cos
            ref[:, hh * dk:hh * dk + dk // 2] = x1.astype(BF16)
            ref[:, hh * dk + dk // 2:(hh + 1) * dk] = x2.astype(BF16)

    rotate_store(_dot(x, w_ref[:, nk:2 * nk]), k_ref, dk ** -0.5)
    v_ref[...] = _dot(x, w_ref[:, 2 * nk:2 * nk + nv]).astype(BF16)
    if with_qg:
        rotate_store(_dot(x, w_ref[:, :nk]), q_ref, 1.0)
        g_ref[...] = _dot(x, w_ref[:, 2 * nk + nv:]).astype(BF16)


def _ret_project(h, mod, n1, w_qkvg, rope_tabs, with_qg, kv_tail=()):
    b, l, d = h.shape
    nk, nv = RET_HEADS * RET_QK_DIM, RET_HEADS * RET_V_DIM
    rope = rope_tabs is not None
    outs = ([(nk, BF16), (nv, BF16)] if with_qg else []) + [(nk, BF16), (nv, BF16)]
    res = _row_call(functools.partial(_ret_proj_body, rope=rope, with_qg=with_qg, d=d), name="ret_proj", batch=b,
                    length=l, tm=_proj_tile(l, kv_tail), rows=(h,), pos=tuple(rope_tabs) if rope else (),
                    vecs=(mod,), consts=(n1, w_qkvg), outs=outs, tails=kv_tail)
    return res if with_qg else [None, None] + list(res)


def _ret_scan_body(*refs, mode, chunk, nchunks, first_dir):
    it = iter(refs)
    ld_ref = next(it)
    q_ref = next(it) if mode != "state" else None
    k_ref, v_ref, s0_ref = next(it), next(it), next(it)
    if mode == "gated":
        yo_ref, g_ref, gn_ref = next(it), next(it), next(it)
    out_ref = next(it) if mode != "state" else None
    sfin_ref, st_ref = next(it), next(it)
    direction = pl.program_id(0) + first_dir
    hh, t = pl.program_id(2), pl.program_id(3)
    ld = ld_ref[direction, hh]

    @pl.when(t == 0)
    def _():
        st_ref[...] = s0_ref[...]

    ri = lax.broadcasted_iota(jnp.int32, (chunk, chunk), 0)
    ci = lax.broadcasted_iota(jnp.int32, (chunk, chunk), 1)
    dist = jnp.where(direction == 0, ri - ci, ci - ri).astype(F32)
    intra = jnp.where(dist >= 0, jnp.exp(jnp.maximum(dist, 0.0) * ld), 0.0)
    pos = lax.broadcasted_iota(jnp.int32, (chunk, 1), 0)
    rank = jnp.where(direction == 0, pos, chunk - 1 - pos).astype(F32)
    q_decay = jnp.exp((rank + 1.0) * ld)
    k_decay = jnp.exp((chunk - 1.0 - rank) * ld)
    chunk_decay = jnp.exp(jnp.full((1, 1), float(chunk), F32) * ld)

    for c in range(nchunks):
        cc = jnp.where(direction == 0, c, nchunks - 1 - c)
        rows = pl.ds(pl.multiple_of(cc * chunk, chunk), chunk)
        k = k_ref[rows, :]
        v = v_ref[rows, :]
        state = st_ref[...]
        if mode != "state":
            q = q_ref[rows, :]
            sc = _dot_nt(q, k) * intra
            y = _dot(sc.astype(BF16), v) + _dot(q, state.astype(BF16)) * q_decay
            if mode == "gated":
                y = y + yo_ref[rows, :]
                yc = y - jnp.mean(y, axis=-1, keepdims=True)
                var = jnp.mean(yc * yc, axis=-1, keepdims=True)
                y = yc * lax.rsqrt(var + EPS) * gn_ref[...] * _silu(g_ref[rows, :].astype(F32))
            out_ref[rows, :] = y.astype(out_ref.dtype)
        kd = (k.astype(F32) * k_decay).astype(BF16)
        st_ref[...] = state * chunk_decay + _dot_tn(kd, v)

    @pl.when(t == pl.num_programs(3) - 1)
    def _():
        sfin_ref[...] = st_ref[...]


def _ret_scan(log_decay, q, k, v, init_state, mode, dirs, length, gated=None):
    b = k.shape[0]
    dk, dv, nh = RET_QK_DIM, RET_V_DIM, RET_HEADS
    chunk = 256 if length % 256 == 0 else 128
    tt = 1024 if length % 1024 == 0 else chunk
    nt = length // tt
    first = dirs[0]

    def tok(d, t):
        return jnp.where(d + first == 0, t, nt - 1 - t)

    def token_block(width):
        return pl.BlockSpec((None, tt, width), lambda d, bb, h, t: (bb, tok(d, t), h))

    in_specs = [pl.BlockSpec(memory_space=pltpu.SMEM)]
    args = [log_decay]
    if mode != "state":
        in_specs.append(token_block(dk))
        args.append(q)
    in_specs += [token_block(dk), token_block(dv),
                 pl.BlockSpec((None, None, None, dk, dv), lambda d, bb, h, t: (d + first, bb, h, 0, 0))]
    args += [k, v, init_state]
    if mode == "gated":
        in_specs += [token_block(dv), token_block(dv), pl.BlockSpec((1, dv), lambda d, bb, h, t: (0, h))]
        args += list(gated)
    out_shape, out_specs = [], []
    if mode != "state":
        out_shape.append(jax.ShapeDtypeStruct((b, length, nh * dv), F32 if mode == "y" else BF16))
        out_specs.append(token_block(dv))
    out_shape.append(jax.ShapeDtypeStruct((len(dirs), b, nh, dk, dv), F32))
    out_specs.append(pl.BlockSpec((None, None, None, dk, dv), lambda d, bb, h, t: (d, bb, h, 0, 0)))
    res = pl.pallas_call(
        functools.partial(_ret_scan_body, mode=mode, chunk=chunk, nchunks=tt // chunk, first_dir=first),
        grid=(len(dirs), b, nh, nt),
        in_specs=in_specs,
        out_specs=out_specs,
        out_shape=out_shape,
        scratch_shapes=[pltpu.VMEM((dk, dv), F32)],
        compiler_params=_compiler_params(("parallel", "parallel", "parallel", "arbitrary")),
        name="ret_" + mode,
    )(*args)
    return res


def kernel(x, c, ctx, c_ctx, mod_w, mod_b, norm1, norm2, ffn_w13, ffn_w2, mla_w_down, mla_q_lora_norm, mla_kv_lora_norm, mla_w_uq, mla_w_ukv, mla_q_norm, mla_k_norm, mla_w_o, gqa_w_qkv, gqa_q_norm, gqa_k_norm, gqa_w_o, na_w_qkv, na_q_norm, na_k_norm, na_rpb, na_w_o, ret_w_qkvg, ret_log_decay_fwd, ret_log_decay_bwd, ret_out_norm, ret_w_o):
    b, s, d = x.shape
    lc = ctx.shape[1]
    depth = mod_w.shape[0]

    pad_rows = -(b + 1) % 16
    c_rows = jnp.concatenate([c, c_ctx[None, :], jnp.zeros((pad_rows, d), F32)], axis=0)
    mods = _modulation(c_rows, mod_w, mod_b)

    h, hc = x, ctx
    for i in range(depth):
        need_ctx = i < depth - 1
        mod_l = mods[i, :b].reshape(b, 1, 6 * d)
        mod_c = mods[i, b:b + 1].reshape(1, 1, 6 * d)
        n1 = norm1[i].reshape(1, d)
        n2 = norm2[i].reshape(1, d)
        kind, j = i % 4, i // 4
        if kind == 0:
            prep = _mla_prepare(mla_w_down[j], mla_q_lora_norm[j], mla_kv_lora_norm[j], mla_w_uq[j], mla_w_ukv[j],
                                mla_q_norm[j], mla_k_norm[j])
            tabs = _rope_tables_rolled(s, MLA_ROPE, LANES // 2)
            qc, kc, vc = _mla_project(hc, mod_c, n1, prep, None, need_ctx)
            ql, k_all, v_all = _mla_project(h, mod_l, n1, prep, tabs, True, (kc, vc))
            attn = functools.partial(_flash_attention, kv_heads=MLA_HEADS, group=1, dqk=MLA_QK_PAD, dv=MLA_V,
                                     tq=FLASH_QUERY_COLUMNS)
            w_o = mla_w_o[j].astype(BF16)
        elif kind == 1:
            w_qkv = gqa_w_qkv[j].astype(BF16)
            tabs = _rope_tables_rolled(s, GQA_HEAD_DIM, GQA_HEAD_DIM // 2)
            qc, kc, vc = _gqa_project(hc, mod_c, n1, w_qkv, gqa_q_norm[j], gqa_k_norm[j], None, need_ctx)
            ql, k_all, v_all = _gqa_project(h, mod_l, n1, w_qkv, gqa_q_norm[j], gqa_k_norm[j], tabs, True, (kc, vc))
            group = GQA_HEADS // GQA_KV_HEADS
            attn = functools.partial(_flash_attention, kv_heads=GQA_KV_HEADS, group=group, dqk=GQA_HEAD_DIM,
                                     dv=GQA_HEAD_DIM, tq=FLASH_QUERY_COLUMNS // group)
            w_o = gqa_w_o[j].astype(BF16)
        elif kind == 2:
            w_qkv = na_w_qkv[j].astype(BF16)
            qc, kc, vc = _na_project(hc, mod_c, n1, w_qkv, na_q_norm[j], na_k_norm[j], need_ctx)
            ql, k_all, v_all = _na_project(h, mod_l, n1, w_qkv, na_q_norm[j], na_k_norm[j], True, (kc, vc))
            w_o = na_w_o[j].astype(BF16)
        else:
            w_qkvg = ret_w_qkvg[j].astype(BF16)
            tabs = _rope_cos_sin(s, RET_QK_DIM)
            qc, gc, kc, vc = _ret_project(hc, mod_c, n1, w_qkvg, None, need_ctx)
            ql, gl, k_all, v_all = _ret_project(h, mod_l, n1, w_qkvg, tabs, True, (kc, vc))
            w_o = ret_w_o[j].astype(BF16)

        oc = None
        if kind in (0, 1):
            ol = attn(ql, k_all, v_all)
            if need_ctx:
                oc = attn(qc, kc, vc)
        elif kind == 2:
            bias = _na_bias_table(na_rpb[j], s // GRID_W)
            ol = _na_attention(ql, k_all, v_all, bias, lc)
            if need_ctx:
                def heads_major(t):
                    return t.reshape(b, lc, NA_HEADS, NA_HEAD_DIM).transpose(0, 2, 1, 3).reshape(
                        b * NA_HEADS, lc, NA_HEAD_DIM)
                oc = _flash_attention(heads_major(qc), heads_major(kc), heads_major(vc), kv_heads=1, group=1,
                                      dqk=NA_HEAD_DIM, dv=NA_HEAD_DIM, tq=lc)
                oc = oc.reshape(b, NA_HEADS, lc, NA_HEAD_DIM).transpose(0, 2, 1, 3).reshape(b, lc, -1)
        else:
            log_decay = jnp.stack([ret_log_decay_fwd[j], ret_log_decay_bwd[j]]).astype(F32)
            zero_state = jnp.zeros((2, b, RET_HEADS, RET_QK_DIM, RET_V_DIM), F32)
            ctx_state, = _ret_scan(log_decay, None, kc, vc, zero_state, "state", (0, 1), lc)
            y_fwd, _ = _ret_scan(log_decay, ql, k_all, v_all, ctx_state, "y", (0,), s)
            ol, _ = _ret_scan(log_decay, ql, k_all, v_all, ctx_state, "gated", (1,), s,
                              gated=(y_fwd, gl, ret_out_norm[j].reshape(1, -1)))
            if need_ctx:
                raise NotImplementedError("context update after a retention layer is not needed at this depth")

        w13, w2 = ffn_w13[i].astype(BF16), ffn_w2[i].astype(BF16)
        h = _ffn(h, mod_l, n2, w13, w2, ol, w_o)
        if need_ctx:
            hc = _ffn(hc, mod_c, n2, w13, w2, oc, w_o)
    return h
```
